```python
import math
import jax, jax.numpy as jnp
from jax import lax
import numpy as np

D_MODEL = 1024
BATCH = 16
SEQ = 2048
DEPTH = 4

GRID_W = 64
CTX_LEN = 256
BLOCK = 128
ROPE_THETA = 10000.0
EPS = 1e-6
NEG_INF = -1e30
N_MOD = 6

MLA_HEADS = 4
MLA_Q_RANK = 256
MLA_KV_RANK = 128
MLA_NOPE = 64
MLA_ROPE = 32
MLA_V = 64

DIFF_HEADS = 4
DIFF_QK = 32
DIFF_V = 2 * DIFF_QK

SWA_HEADS = 8
SWA_KV_HEADS = 2
SWA_DIM = 64
WINDOW = 128

N_EXPERTS = 32
TOP_K = 4
D_EXPERT = D_MODEL
SWIGLU_LIMIT = 7.0
SWIGLU_ALPHA = 1.702

IN_SPLIT_SIZES = (MLA_Q_RANK, MLA_KV_RANK, MLA_ROPE,
                  DIFF_HEADS * 2 * DIFF_QK, DIFF_HEADS * 2 * DIFF_QK, DIFF_HEADS * DIFF_V,
                  SWA_HEADS * SWA_DIM, SWA_KV_HEADS * SWA_DIM, SWA_KV_HEADS * SWA_DIM)
D_IN = (MLA_Q_RANK + MLA_KV_RANK + MLA_ROPE
        + 2 * DIFF_HEADS * 2 * DIFF_QK + DIFF_HEADS * DIFF_V
        + SWA_HEADS * SWA_DIM + 2 * SWA_KV_HEADS * SWA_DIM)
D_MIX = MLA_HEADS * MLA_V + DIFF_HEADS * DIFF_V + SWA_HEADS * SWA_DIM

kernel_name = 'hybrid_mla_diff_swa_moe_dit_trunk'


def rms_norm(x, g):
    xf = x.astype(jnp.float32)
    y = xf * lax.rsqrt(jnp.mean(xf * xf, axis=-1, keepdims=True) + EPS)
    return (y * g.astype(jnp.float32)).astype(x.dtype)


def modulate(h, g, shift, scale):
    return rms_norm(h, g) * (1 + scale) + shift


def axial_rope_tables(rows, rot_dim):
    n_freq = rot_dim // 4
    inv_freq = ROPE_THETA ** (-jnp.arange(n_freq, dtype=jnp.float32) / n_freq)
    row_pos = jnp.repeat(jnp.arange(rows, dtype=jnp.float32), GRID_W)
    col_pos = jnp.tile(jnp.arange(GRID_W, dtype=jnp.float32), rows)
    ang = jnp.concatenate([row_pos[:, None] * inv_freq, col_pos[:, None] * inv_freq], axis=-1)
    return jnp.cos(ang), jnp.sin(ang)


def apply_rope(x, cos, sin):
    half = x.shape[-1] // 2
    xf = x.astype(jnp.float32)
    x1, x2 = xf[..., :half], xf[..., half:]
    c, s = cos[None, :, None, :], sin[None, :, None, :]
    return jnp.concatenate([x1 * c - x2 * s, x2 * c + x1 * s], axis=-1).astype(x.dtype)


def to_heads(t, h, d):
    return t.reshape(t.shape[0], t.shape[1], h, d)


def merge_heads(t):
    return t.reshape(t.shape[0], t.shape[1], -1)


def softmax_attention(q, k, v, scale):
    s = jnp.einsum('bqhd,bkhd->bhqk', q, k).astype(jnp.float32) * scale
    p = jax.nn.softmax(s, axis=-1).astype(v.dtype)
    return jnp.einsum('bhqk,bkhd->bqhd', p, v)


def dense_blocked_attention(q, k, v, k_ctx, v_ctx, scale):
    b, s_len, h, _ = q.shape
    kk = jnp.concatenate([k_ctx, k], axis=1)
    vv = jnp.concatenate([v_ctx, v], axis=1)

    def one_block(n):
        q_blk = lax.dynamic_slice_in_dim(q, n * BLOCK, BLOCK, axis=1)
        return softmax_attention(q_blk, kk, vv, scale)

    out = lax.map(one_block, jnp.arange(s_len // BLOCK))
    return jnp.moveaxis(out, 0, 1).reshape(b, s_len, h, v.shape[-1])


def sink_softmax(s, sink):
    col = jnp.broadcast_to(sink.astype(jnp.float32).reshape(SWA_KV_HEADS, -1, 1, 1), s.shape[:-1] + (1,))
    p = jax.nn.softmax(jnp.concatenate([s, col], axis=-1), axis=-1)
    return p[..., :-1]


def banded_sink_attention(q, k, v, k_ctx, v_ctx, sink, scale):
    b, s_len, h, d = q.shape
    g = h // SWA_KV_HEADS
    n_ctx = k_ctx.shape[1]
    pad = ((0, 0), (BLOCK, BLOCK), (0, 0), (0, 0))
    kp, vp = jnp.pad(k, pad), jnp.pad(v, pad)

    def one_block(n):
        start = n * BLOCK
        q_blk = lax.dynamic_slice_in_dim(q, start, BLOCK, axis=1).reshape(b, BLOCK, SWA_KV_HEADS, g, d)
        k_blk = lax.dynamic_slice_in_dim(kp, start, 3 * BLOCK, axis=1)
        v_blk = lax.dynamic_slice_in_dim(vp, start, 3 * BLOCK, axis=1)
        q_pos = start + jnp.arange(BLOCK)
        k_pos = start - BLOCK + jnp.arange(3 * BLOCK)
        allowed = ((jnp.abs(k_pos[None, :] - q_pos[:, None]) <= WINDOW)
                   & (k_pos >= 0)[None, :] & (k_pos < s_len)[None, :])
        s_band = jnp.einsum('bqhgd,bkhd->bhgqk', q_blk, k_blk).astype(jnp.float32) * scale
        s_band = jnp.where(allowed, s_band, NEG_INF)
        s_ctx = jnp.einsum('bqhgd,bkhd->bhgqk', q_blk, k_ctx).astype(jnp.float32) * scale
        p = sink_softmax(jnp.concatenate([s_ctx, s_band], axis=-1), sink).astype(v.dtype)
        o = (jnp.einsum('bhgqk,bkhd->bqhgd', p[..., :n_ctx], v_ctx)
             + jnp.einsum('bhgqk,bkhd->bqhgd', p[..., n_ctx:], v_blk))
        return o.reshape(b, BLOCK, h, d)

    out = lax.map(one_block, jnp.arange(s_len // BLOCK))
    return jnp.moveaxis(out, 0, 1).reshape(b, s_len, h, d)


def context_sink_attention(q, k, v, sink, scale):
    b, n, h, d = q.shape
    qg = q.reshape(b, n, SWA_KV_HEADS, h // SWA_KV_HEADS, d)
    s = jnp.einsum('bqhgd,bkhd->bhgqk', qg, k).astype(jnp.float32) * scale
    p = sink_softmax(s, sink).astype(v.dtype)
    return jnp.einsum('bhgqk,bkhd->bqhgd', p, v).reshape(b, n, h, d)


def mla_queries(c_q, q_norm_g, w_uq, cos, sin):
    q = to_heads(rms_norm(c_q, q_norm_g) @ w_uq, MLA_HEADS, MLA_NOPE + MLA_ROPE)
    q_nope, q_rope = q[..., :MLA_NOPE], q[..., MLA_NOPE:]
    if cos is not None:
        q_rope = apply_rope(q_rope, cos, sin)
    return jnp.concatenate([q_nope, q_rope], axis=-1)


def mla_keys_values(c_kv, k_rope, kv_norm_g, w_ukv, cos, sin):
    b, n, _ = c_kv.shape
    kv = to_heads(rms_norm(c_kv, kv_norm_g) @ w_ukv, MLA_HEADS, MLA_NOPE + MLA_V)
    k_nope, v = kv[..., :MLA_NOPE], kv[..., MLA_NOPE:]
    k_rope = k_rope[:, :, None, :]
    if cos is not None:
        k_rope = apply_rope(k_rope, cos, sin)
    k = jnp.concatenate([k_nope, jnp.broadcast_to(k_rope, (b, n, MLA_HEADS, MLA_ROPE))], axis=-1)
    return k, v


def mla_mixer(pl, pc, q_norm_g, w_uq, kv_norm_g, w_ukv, cos, sin, need_ctx):
    scale = (MLA_NOPE + MLA_ROPE) ** -0.5
    kc, vc = mla_keys_values(pc[1], pc[2], kv_norm_g, w_ukv, None, None)
    ql = mla_queries(pl[0], q_norm_g, w_uq, cos, sin)
    kl, vl = mla_keys_values(pl[1], pl[2], kv_norm_g, w_ukv, cos, sin)
    out_l = merge_heads(dense_blocked_attention(ql, kl, vl, kc, vc, scale))
    out_c = None
    if need_ctx:
        qc = mla_queries(pc[0], q_norm_g, w_uq, None, None)
        out_c = merge_heads(softmax_attention(qc, kc, vc, scale))
    return out_l, out_c


def diff_heads(t, cos, sin):
    b, n, _ = t.shape
    t = t.reshape(b, n, DIFF_HEADS * 2, DIFF_QK)
    if cos is not None:
        t = apply_rope(t, cos, sin)
    return t.reshape(b, n, DIFF_HEADS, 2, DIFF_QK)


def diff_mixer(pl, pc, lam, subln_g, lambda_init, cos, sin, need_ctx):
    scale = DIFF_QK ** -0.5
    lam = lam.astype(jnp.float32)
    lam_full = jnp.exp(jnp.sum(lam[0] * lam[1])) - jnp.exp(jnp.sum(lam[2] * lam[3])) + lambda_init

    def finish(a1, a2):
        o = a1.astype(jnp.float32) - lam_full * a2.astype(jnp.float32)
        o = rms_norm(o, subln_g) * (1.0 - lambda_init)
        return merge_heads(o.astype(a1.dtype))

    kc = diff_heads(pc[1], None, None)
    vc = to_heads(pc[2], DIFF_HEADS, DIFF_V)
    ql = diff_heads(pl[0], cos, sin)
    kl = diff_heads(pl[1], cos, sin)
    vl = to_heads(pl[2], DIFF_HEADS, DIFF_V)
    out_l = finish(dense_blocked_attention(ql[..., 0, :], kl[..., 0, :], vl, kc[..., 0, :], vc, scale),
                   dense_blocked_attention(ql[..., 1, :], kl[..., 1, :], vl, kc[..., 1, :], vc, scale))
    out_c = None
    if need_ctx:
        qc = diff_heads(pc[0], None, None)
        out_c = finish(softmax_attention(qc[..., 0, :], kc[..., 0, :], vc, scale),
                       softmax_attention(qc[..., 1, :], kc[..., 1, :], vc, scale))
    return out_l, out_c


def swa_mixer(pl, pc, sink, cos, sin, need_ctx):
    scale = SWA_DIM ** -0.5
    kc = to_heads(pc[1], SWA_KV_HEADS, SWA_DIM)
    vc = to_heads(pc[2], SWA_KV_HEADS, SWA_DIM)
    ql = apply_rope(to_heads(pl[0], SWA_HEADS, SWA_DIM), cos, sin)
    kl = apply_rope(to_heads(pl[1], SWA_KV_HEADS, SWA_DIM), cos, sin)
    vl = to_heads(pl[2], SWA_KV_HEADS, SWA_DIM)
    out_l = merge_heads(banded_sink_attention(ql, kl, vl, kc, vc, sink, scale))
    out_c = None
    if need_ctx:
        qc = to_heads(pc[0], SWA_HEADS, SWA_DIM)
        out_c = merge_heads(context_sink_attention(qc, kc, vc, sink, scale))
    return out_l, out_c


def clamped_swiglu(gate, up):
    gate = jnp.minimum(gate, SWIGLU_LIMIT)
    up = jnp.clip(up, -SWIGLU_LIMIT, SWIGLU_LIMIT)
    return gate * jax.nn.sigmoid(SWIGLU_ALPHA * gate) * (up + 1)


def moe_ffn(h, w_router, b_router, w_gate, b_gate, w_up, b_up, w_down, b_down):
    shape = h.shape
    hf = h.reshape(-1, shape[-1])
    logits = (hf @ w_router + b_router).astype(jnp.float32)
    top_val, top_idx = lax.top_k(logits, TOP_K)
    top_w = jax.nn.softmax(top_val, axis=-1)
    gates = jnp.sum(jax.nn.one_hot(top_idx, N_EXPERTS, dtype=jnp.float32) * top_w[..., None], axis=1).astype(hf.dtype)
    out = jnp.zeros_like(hf)
    for e in range(N_EXPERTS):
        act = clamped_swiglu(hf @ w_gate[e] + b_gate[e], hf @ w_up[e] + b_up[e])
        out = out + gates[:, e:e + 1] * (act @ w_down[e] + b_down[e])
    return out.reshape(shape)


def setup_inputs(seed: int = 0) -> dict:
    key = jax.random.key(seed)
    ks = jax.random.split(key, 26)
    L, D, E, F = DEPTH, D_MODEL, N_EXPERTS, D_EXPERT

    def nrm(k, shape, scale):
        return jax.random.normal(k, shape, jnp.float32) * scale

    def gain(k, shape):
        return 1.0 + nrm(k, shape, 0.05)

    return {
        'x': nrm(ks[0], (BATCH, SEQ, D), 1.0),
        'c': nrm(ks[1], (BATCH, D), 1.0),
        'ctx': nrm(ks[2], (BATCH, CTX_LEN, D), 1.0),
        'c_ctx': nrm(ks[3], (D,), 1.0),
        'w_ada': nrm(ks[4], (L, D, N_MOD * D), 0.5 * D ** -0.5),
        'b_ada': nrm(ks[5], (L, N_MOD * D), 0.02),
        'g_attn': gain(ks[6], (L, D)),
        'w_in': nrm(ks[7], (L, D, D_IN), D ** -0.5),
        'mla_q_norm': gain(ks[8], (L, MLA_Q_RANK)),
        'mla_w_uq': nrm(ks[9], (L, MLA_Q_RANK, MLA_HEADS * (MLA_NOPE + MLA_ROPE)), MLA_Q_RANK ** -0.5),
        'mla_kv_norm': gain(ks[10], (L, MLA_KV_RANK)),
        'mla_w_ukv': nrm(ks[11], (L, MLA_KV_RANK, MLA_HEADS * (MLA_NOPE + MLA_V)), MLA_KV_RANK ** -0.5),
        'diff_lambda': nrm(ks[12], (L, 4, DIFF_QK), 0.1),
        'diff_subln': gain(ks[13], (L, DIFF_V)),
        'swa_sink': nrm(ks[14], (L, SWA_HEADS), 0.5),
        'w_out': nrm(ks[15], (L, D_MIX, D), D_MIX ** -0.5),
        'g_ffn': gain(ks[16], (L, D)),
        'w_router': nrm(ks[17], (L, D, E), D ** -0.5),
        'b_router': nrm(ks[18], (L, E), 0.01),
        'w_gate': nrm(ks[19], (L, E, D, F), D ** -0.5),
        'b_gate': nrm(ks[20], (L, E, F), 0.02),
        'w_up': nrm(ks[21], (L, E, D, F), D ** -0.5),
        'b_up': nrm(ks[22], (L, E, F), 0.02),
        'w_down': nrm(ks[23], (L, E, F, D), F ** -0.5),
        'b_down': nrm(ks[24], (L, E, D), 0.02),
        'g_final': gain(ks[25], (D,)),
    }


def reference(x, c, ctx, c_ctx, w_ada, b_ada, g_attn, w_in, mla_q_norm, mla_w_uq, mla_kv_norm, mla_w_ukv,
              diff_lambda, diff_subln, swa_sink, w_out, g_ffn, w_router, b_router, w_gate, b_gate, w_up, b_up,
              w_down, b_down, g_final):
    b, s_len, d = x.shape
    n_ctx = ctx.shape[1]
    rows = s_len // GRID_W
    cos_r, sin_r = axial_rope_tables(rows, MLA_ROPE)
    cos_w, sin_w = axial_rope_tables(rows, SWA_DIM)
    split_points = [int(p) for p in np.cumsum(IN_SPLIT_SIZES)[:-1]]
    silu_c = jax.nn.silu(c)
    silu_cc = jax.nn.silu(c_ctx)[None, :]
    xl, xc = x, ctx
    for i in range(DEPTH):
        need_ctx = i < DEPTH - 1
        lambda_init = 0.8 - 0.6 * math.exp(-0.3 * i)
        mod_l = jnp.split((silu_c @ w_ada[i] + b_ada[i]).reshape(b, N_MOD, d), N_MOD, axis=1)
        mod_c = jnp.split((silu_cc @ w_ada[i] + b_ada[i]).reshape(1, N_MOD, d), N_MOD, axis=1)
        hl = modulate(xl, g_attn[i], mod_l[0], mod_l[1])
        hc = modulate(xc, g_attn[i], mod_c[0], mod_c[1])
        pl = jnp.split(hl @ w_in[i], split_points, axis=-1)
        pc = jnp.split(hc @ w_in[i], split_points, axis=-1)
        mla_l, mla_c = mla_mixer(pl[0:3], pc[0:3], mla_q_norm[i], mla_w_uq[i], mla_kv_norm[i], mla_w_ukv[i],
                                 cos_r, sin_r, need_ctx)
        diff_l, diff_c = diff_mixer(pl[3:6], pc[3:6], diff_lambda[i], diff_subln[i], lambda_init,
                                    cos_r, sin_r, need_ctx)
        swa_l, swa_c = swa_mixer(pl[6:9], pc[6:9], swa_sink[i], cos_w, sin_w, need_ctx)
        xl = xl + mod_l[2] * (jnp.concatenate([mla_l, diff_l, swa_l], axis=-1) @ w_out[i])
        fl = modulate(xl, g_ffn[i], mod_l[3], mod_l[4])
        if need_ctx:
            xc = xc + mod_c[2] * (jnp.concatenate([mla_c, diff_c, swa_c], axis=-1) @ w_out[i])
            fc = modulate(xc, g_ffn[i], mod_c[3], mod_c[4])
            y = moe_ffn(jnp.concatenate([fc, fl], axis=1), w_router[i], b_router[i], w_gate[i], b_gate[i],
                        w_up[i], b_up[i], w_down[i], b_down[i])
            xc = xc + mod_c[5] * y[:, :n_ctx]
            xl = xl + mod_l[5] * y[:, n_ctx:]
        else:
            xl = xl + mod_l[5] * moe_ffn(fl, w_router[i], b_router[i], w_gate[i], b_gate[i],
                                         w_up[i], b_up[i], w_down[i], b_down[i])
    return rms_norm(xl, g_final)
```

```python
import functools
import math

import jax
import jax.numpy as jnp
import numpy as np
from jax import lax
from jax.experimental import pallas as pl
from jax.experimental.pallas import tpu as pltpu

F32 = jnp.float32
BF16 = jnp.bfloat16
I32 = jnp.int32
HIGHEST = lax.Precision.HIGHEST

LANES = 128
SUBLANES = 8
VMEM_LIMIT = 56 * 1024 * 1024

GRID_W = 64
ROPE_THETA = 10000.0
EPS = 1e-6
NEG_INF = -1e30
N_MOD = 6

MLA_HEADS, MLA_Q_RANK, MLA_KV_RANK, MLA_NOPE, MLA_ROPE, MLA_V = 4, 256, 128, 64, 32, 64
DIFF_HEADS, DIFF_QK = 4, 32
DIFF_V = 2 * DIFF_QK
SWA_HEADS, SWA_KV_HEADS, SWA_DIM, WINDOW = 8, 2, 64, 128
TOP_K = 4
SWIGLU_LIMIT = 7.0
SWIGLU_ALPHA = 1.702

TM = 256
TMOE = 256
TD = 1024

G_CQ, G_CKV, G_KROPE, G_DQ, G_DK, G_DV, G_SQ, G_SK, G_SV, N_GROUPS = 0, 2, 3, 4, 6, 8, 10, 14, 16, 18


def _params(*sem):
    return pltpu.CompilerParams(dimension_semantics=sem, vmem_limit_bytes=VMEM_LIMIT)


def _in_proj_columns():
    src = -np.ones(N_GROUPS * LANES, np.int64)
    o_ckv = MLA_Q_RANK
    o_kr = o_ckv + MLA_KV_RANK
    o_dq = o_kr + MLA_ROPE
    o_dk = o_dq + DIFF_HEADS * 2 * DIFF_QK
    o_dv = o_dk + DIFF_HEADS * 2 * DIFF_QK
    o_sq = o_dv + DIFF_HEADS * DIFF_V
    o_sk = o_sq + SWA_HEADS * SWA_DIM
    o_sv = o_sk + SWA_KV_HEADS * SWA_DIM
    src[G_CQ * LANES:G_CQ * LANES + MLA_Q_RANK] = np.arange(MLA_Q_RANK)
    src[G_CKV * LANES:G_CKV * LANES + MLA_KV_RANK] = o_ckv + np.arange(MLA_KV_RANK)
    src[G_KROPE * LANES + MLA_NOPE:G_KROPE * LANES + MLA_NOPE + MLA_ROPE] = o_kr + np.arange(MLA_ROPE)
    src[G_DQ * LANES:G_DQ * LANES + 256] = o_dq + np.arange(256)
    src[G_DK * LANES:G_DK * LANES + 256] = o_dk + np.arange(256)
    src[G_DV * LANES:G_DV * LANES + 256] = o_dv + np.arange(256)
    src[G_SQ * LANES:G_SQ * LANES + 512] = o_sq + np.arange(512)
    for kv in range(SWA_KV_HEADS):
        for half in range(2):
            lo = half * SWA_DIM
            src[(G_SK + kv) * LANES + lo:(G_SK + kv) * LANES + lo + SWA_DIM] = o_sk + kv * SWA_DIM + np.arange(SWA_DIM)
            src[(G_SV + kv) * LANES + lo:(G_SV + kv) * LANES + lo + SWA_DIM] = o_sv + kv * SWA_DIM + np.arange(SWA_DIM)
    return src


def _gather_columns(w, src):
    cols = jnp.take(w, jnp.asarray(np.maximum(src, 0), I32), axis=-1)
    return jnp.where(jnp.asarray(src >= 0), cols, 0.0)


def _uq_columns():
    src = -np.ones(MLA_HEADS * LANES, np.int64)
    hd = MLA_NOPE + MLA_ROPE
    for h in range(MLA_HEADS):
        src[h * LANES:h * LANES + hd] = h * hd + np.arange(hd)
    return src


def _ukv_columns():
    src = -np.ones(MLA_HEADS * LANES + MLA_HEADS * MLA_V, np.int64)
    hd = MLA_NOPE + MLA_V
    for h in range(MLA_HEADS):
        src[h * LANES:h * LANES + MLA_NOPE] = h * hd + np.arange(MLA_NOPE)
        src[MLA_HEADS * LANES + h * MLA_V:MLA_HEADS * LANES + (h + 1) * MLA_V] = h * hd + MLA_NOPE + np.arange(MLA_V)
    return src


def _rope_tables(n_ctx, s_len):
    rows = s_len // GRID_W

    def axial(rot_dim):
        n_freq = rot_dim // 4
        inv_freq = ROPE_THETA ** (-jnp.arange(n_freq, dtype=F32) / n_freq)
        row_pos = jnp.repeat(jnp.arange(rows, dtype=F32), GRID_W)
        col_pos = jnp.tile(jnp.arange(GRID_W, dtype=F32), rows)
        ang = jnp.concatenate([row_pos[:, None] * inv_freq, col_pos[:, None] * inv_freq], axis=-1)
        return jnp.cos(ang), jnp.sin(ang)

    def expand(cos, sin, lane_rot):
        half = cos.shape[1]
        lane_rot = np.asarray(lane_rot)
        idx = np.maximum(lane_rot, 0) % half
        is_rot = lane_rot >= 0
        lo = is_rot & (lane_rot < half)
        hi = is_rot & (lane_rot >= half)
        c = jnp.where(jnp.asarray(is_rot), cos[:, idx], 1.0)
        s_lo = jnp.where(jnp.asarray(lo), -sin[:, idx], 0.0)
        s_hi = jnp.where(jnp.asarray(hi), sin[:, idx], 0.0)
        ident = [jnp.ones((n_ctx, LANES), F32), jnp.zeros((n_ctx, LANES), F32), jnp.zeros((n_ctx, LANES), F32)]
        return [jnp.concatenate([i, t], axis=0) for i, t in zip(ident, (c, s_lo, s_hi))]

    cos_r, sin_r = axial(MLA_ROPE)
    cos_w, sin_w = axial(SWA_DIM)
    lanes = np.arange(LANES)
    mla_rot = np.where((lanes >= MLA_NOPE) & (lanes < MLA_NOPE + MLA_ROPE), lanes - MLA_NOPE, -1)
    return (expand(cos_r, sin_r, mla_rot) + expand(cos_r, sin_r, lanes % DIFF_QK)
            + expand(cos_w, sin_w, lanes % SWA_DIM))


def _rms(x):
    return x * lax.rsqrt(jnp.mean(x * x, axis=-1, keepdims=True) + EPS)


def _modulate(x, g, shift, scale):
    return (_rms(x) * g) * (1.0 + scale) + shift


def _rope(v, c, s_lo, s_hi, half):
    return v * c + pltpu.roll(v, half, 1) * s_hi + pltpu.roll(v, LANES - half, 1) * s_lo


def _qk(q, k):
    return lax.dot_general(q, k, (((1,), (1,)), ((), ())), preferred_element_type=F32)


def _lane_iota(shape):
    return lax.broadcasted_iota(I32, shape, len(shape) - 1)


def _ada_kernel(c_ref, w_ref, b_ref, o_ref):
    c = c_ref[...]
    a = c * (1.0 / (1.0 + jnp.exp(-c)))
    o_ref[0] = jnp.dot(a, w_ref[0], precision=HIGHEST, preferred_element_type=F32) + b_ref[0]


def _ada(cc, w_ada, b_ada):
    n_layers, d, n_out = w_ada.shape
    rows = cc.shape[0]
    tn = d
    return pl.pallas_call(
        _ada_kernel,
        grid=(n_layers, n_out // tn),
        in_specs=[pl.BlockSpec((rows, d), lambda l, n: (0, 0)),
                  pl.BlockSpec((1, d, tn), lambda l, n: (l, 0, n)),
                  pl.BlockSpec((1, 1, tn), lambda l, n: (l, 0, n))],
        out_specs=pl.BlockSpec((1, rows, tn), lambda l, n: (l, 0, n)),
        out_shape=jax.ShapeDtypeStruct((n_layers, rows, n_out), F32),
        compiler_params=_params("arbitrary", "arbitrary"),
        name="ada",
    )(cc, w_ada, b_ada.reshape(n_layers, 1, n_out))


def _pre_attn_kernel(x_ref, mod_ref, g_ref, win_ref, qn_ref, wuq_ref, kvn_ref, wukv_ref,
                     mc_ref, ml_ref, mh_ref, dc_ref, dl_ref, dh_ref, wc_ref, wl_ref, wh_ref,
                     mq_ref, mk_ref, mv_ref, dq_ref, dk_ref, dv_ref, sq_ref, sk_ref, sv_ref):
    x = x_ref[0]
    mod = mod_ref[0, 0]
    h = _modulate(x, g_ref[0], mod[0:1], mod[1:2])
    p = jnp.dot(h.astype(BF16), win_ref[0], preferred_element_type=F32)

    def grp(g, n=1):
        return p[:, g * LANES:(g + n) * LANES]

    mla_scale = (MLA_NOPE + MLA_ROPE) ** -0.5
    diff_scale = DIFF_QK ** -0.5
    swa_scale = SWA_DIM ** -0.5
    mla_tab = (mc_ref[...], ml_ref[...], mh_ref[...])
    diff_tab = (dc_ref[...], dl_ref[...], dh_ref[...])
    swa_tab = (wc_ref[...], wl_ref[...], wh_ref[...])

    cq = (_rms(grp(G_CQ, 2)) * qn_ref[0]).astype(BF16)
    q = jnp.dot(cq, wuq_ref[0], preferred_element_type=F32)
    ckv = (_rms(grp(G_CKV)) * kvn_ref[0]).astype(BF16)
    kv = jnp.dot(ckv, wukv_ref[0], preferred_element_type=F32)
    k_rope = _rope(grp(G_KROPE), *mla_tab, MLA_ROPE // 2)
    for hd in range(MLA_HEADS):
        sl = slice(hd * LANES, (hd + 1) * LANES)
        mq_ref[0, :, sl] = (_rope(q[:, sl], *mla_tab, MLA_ROPE // 2) * mla_scale).astype(BF16)
        mk_ref[0, :, sl] = (kv[:, sl] + k_rope).astype(BF16)
    mv_ref[0] = kv[:, MLA_HEADS * LANES:].astype(BF16)

    for g in range(2):
        sl = slice(g * LANES, (g + 1) * LANES)
        dq_ref[0, :, sl] = (_rope(grp(G_DQ + g), *diff_tab, DIFF_QK // 2) * diff_scale).astype(BF16)
        dk_ref[0, :, sl] = _rope(grp(G_DK + g), *diff_tab, DIFF_QK // 2).astype(BF16)
    dv_ref[0] = grp(G_DV, 2).astype(BF16)

    for g in range(4):
        sl = slice(g * LANES, (g + 1) * LANES)
        sq_ref[0, :, sl] = (_rope(grp(G_SQ + g), *swa_tab, SWA_DIM // 2) * swa_scale).astype(BF16)
    for g in range(2):
        sl = slice(g * LANES, (g + 1) * LANES)
        sk_ref[0, :, sl] = _rope(grp(G_SK + g), *swa_tab, SWA_DIM // 2).astype(BF16)
    sv_ref[0] = grp(G_SV, 2).astype(BF16)


def _pre_attn(layer, xs, mods, g_attn, w_in_p, q_norm, w_uq_p, kv_norm, w_ukv_p, tables, n_ctx):
    b, nt, d = xs.shape
    n_ctx_tiles = n_ctx // TM
    ctx_row = b

    def tok(w):
        return pl.BlockSpec((1, TM, w), lambda j, bi: (bi, j, 0))

    def lay(shape):
        return pl.BlockSpec((1,) + shape, lambda j, bi: (layer,) + (0,) * len(shape))

    tab = pl.BlockSpec((TM, LANES), lambda j, bi: (j, 0))
    mod_spec = pl.BlockSpec((1, 1, N_MOD, d), lambda j, bi: (layer, jnp.where(j < n_ctx_tiles, ctx_row, bi), 0, 0))
    widths = (512, 512, 256, 256, 256, 256, 512, 256, 256)
    return pl.pallas_call(
        _pre_attn_kernel,
        grid=(nt // TM, b),
        in_specs=[tok(d), mod_spec, lay((1, d)), lay(w_in_p.shape[1:]), lay((1, MLA_Q_RANK)),
                  lay(w_uq_p.shape[1:]), lay((1, MLA_KV_RANK)), lay(w_ukv_p.shape[1:])] + [tab] * 9,
        out_specs=[tok(w) for w in widths],
        out_shape=[jax.ShapeDtypeStruct((b, nt, w), BF16) for w in widths],
        compiler_params=_params("arbitrary", "arbitrary"),
        name="pre_attn",
    )(xs, mods, g_attn, w_in_p, q_norm, w_uq_p, kv_norm, w_ukv_p, *tables)


def _softmax_pv(s_parts, v_parts, sink=None):
    m = functools.reduce(jnp.maximum, [jnp.max(s, axis=-1, keepdims=True) for s in s_parts])
    if sink is not None:
        m = jnp.maximum(m, sink)
    ps = [jnp.exp(s - m) for s in s_parts]
    l = functools.reduce(jnp.add, [jnp.sum(p, axis=-1, keepdims=True) for p in ps])
    if sink is not None:
        l = l + jnp.exp(sink - m)
    o = functools.reduce(jnp.add, [jnp.dot(p.astype(BF16), v, preferred_element_type=F32)
                                   for p, v in zip(ps, v_parts)])
    return o / l


def _mla_attn_kernel(n_ctx, q_ref, k_ref, v_ref, o_ref):
    j = pl.program_id(1)

    def run(nk):
        lane = _lane_iota((TM, LANES))
        outs = []
        for hd in range(MLA_HEADS):
            sl = slice(hd * LANES, (hd + 1) * LANES)
            vs = slice((hd // 2) * LANES, (hd // 2 + 1) * LANES)
            s = _qk(q_ref[0, :, sl], k_ref[0, :nk, sl])
            outs.append(_softmax_pv([s], [v_ref[0, :nk, vs]]))
        for g in range(2):
            o_ref[0, :, g * LANES:(g + 1) * LANES] = jnp.where(lane < MLA_V, outs[2 * g], outs[2 * g + 1]).astype(BF16)

    @pl.when(j < n_ctx // TM)
    def _():
        run(n_ctx)

    @pl.when(j >= n_ctx // TM)
    def _():
        run(k_ref.shape[1])


def _diff_attn_kernel(n_ctx, lambda_init, q_ref, k_ref, v_ref, lam_ref, g_ref, seg_ref, o_ref):
    j = pl.program_id(1)
    lam = lam_ref[0]
    lam_full = (jnp.exp(jnp.sum(lam[0:1] * lam[1:2], axis=-1, keepdims=True))
                - jnp.exp(jnp.sum(lam[2:3] * lam[3:4], axis=-1, keepdims=True)) + lambda_init)

    def run(nk):
        lane = _lane_iota((TM, LANES))
        heads = []
        for hd in range(DIFF_HEADS):
            sl = slice((hd // 2) * LANES, (hd // 2 + 1) * LANES)
            q = q_ref[0, :, sl]
            k = k_ref[0, :nk, sl]
            v = v_ref[0, :nk, sl]
            a = []
            for comp in range(2):
                seg = (hd % 2) * 2 + comp
                qm = jnp.where((lane >= seg * DIFF_QK) & (lane < (seg + 1) * DIFF_QK), q, jnp.zeros_like(q))
                a.append(_softmax_pv([_qk(qm, k)], [v]))
            heads.append(a[0] - lam_full * a[1])
        for g in range(2):
            o = jnp.where(lane < DIFF_V, heads[2 * g], heads[2 * g + 1])
            ms = jnp.dot(o * o, seg_ref[...], precision=HIGHEST, preferred_element_type=F32) * (1.0 / DIFF_V)
            o = o * lax.rsqrt(ms + EPS) * g_ref[...] * (1.0 - lambda_init)
            o_ref[0, :, g * LANES:(g + 1) * LANES] = o.astype(BF16)

    @pl.when(j < n_ctx // TM)
    def _():
        run(n_ctx)

    @pl.when(j >= n_ctx // TM)
    def _():
        run(k_ref.shape[1])


def _swa_attn_kernel(n_ctx, q_ref, k_ref, v_ref, sink_ref, o_ref):
    j = pl.program_id(1)
    nt = k_ref.shape[1]
    band = TM + 2 * WINDOW
    lane = _lane_iota((TM, LANES))

    def heads(group):
        sl = slice(group * LANES, (group + 1) * LANES)
        kv = group // 2
        q = q_ref[0, :, sl]
        for half in range(2):
            hd = 2 * group + half
            qm = jnp.where((lane >= half * SWA_DIM) & (lane < (half + 1) * SWA_DIM), q, jnp.zeros_like(q))
            yield hd, kv, qm

    def store(group, outs):
        o_ref[0, :, group * LANES:(group + 1) * LANES] = jnp.where(lane < SWA_DIM, outs[0], outs[1]).astype(BF16)

    @pl.when(j < n_ctx // TM)
    def _():
        for group in range(4):
            outs = []
            for hd, kv, qm in heads(group):
                ks = slice(kv * LANES, (kv + 1) * LANES)
                s = _qk(qm, k_ref[0, :n_ctx, ks])
                outs.append(_softmax_pv([s], [v_ref[0, :n_ctx, ks]], sink_ref[0, hd:hd + 1, 0:1]))
            store(group, outs)

    @pl.when(j >= n_ctx // TM)
    def _():
        q0 = j * TM
        w0 = pl.multiple_of(jnp.clip(q0 - WINDOW, n_ctx, nt - band), WINDOW)
        q_pos = q0 + lax.broadcasted_iota(I32, (TM, band), 0)
        k_pos = w0 + lax.broadcasted_iota(I32, (TM, band), 1)
        allowed = jnp.abs(k_pos - q_pos) <= WINDOW
        for group in range(4):
            outs = []
            for hd, kv, qm in heads(group):
                ks = slice(kv * LANES, (kv + 1) * LANES)
                s_ctx = _qk(qm, k_ref[0, :n_ctx, ks])
                s_band = jnp.where(allowed, _qk(qm, k_ref[0, pl.ds(w0, band), ks]), NEG_INF)
                outs.append(_softmax_pv([s_ctx, s_band],
                                        [v_ref[0, :n_ctx, ks], v_ref[0, pl.ds(w0, band), ks]],
                                        sink_ref[0, hd:hd + 1, 0:1]))
            store(group, outs)


def _attention(kernel, q, k, v, extra, out_width, name):
    b, nt, _ = q.shape

    def kv_spec(a):
        return pl.BlockSpec((1, nt, a.shape[2]), lambda bi, j: (bi, 0, 0))

    extra_specs = [pl.BlockSpec(e.shape, lambda bi, j, nd=e.ndim: (0,) * nd) for e in extra]
    return pl.pallas_call(
        kernel,
        grid=(b, nt // TM),
        in_specs=[pl.BlockSpec((1, TM, q.shape[2]), lambda bi, j: (bi, j, 0)), kv_spec(k), kv_spec(v)] + extra_specs,
        out_specs=pl.BlockSpec((1, TM, out_width), lambda bi, j: (bi, j, 0)),
        out_shape=jax.ShapeDtypeStruct((b, nt, out_width), BF16),
        compiler_params=_params("arbitrary", "arbitrary"),
        name=name,
    )(q, k, v, *extra)


def _post_attn_kernel(n_exp, x_ref, mod_ref, mla_ref, diff_ref, swa_ref, wout_ref, g_ref, wr_ref, br_ref,
                      x1_ref, f_ref, route_ref, rw_ref, cnt_ref, carry_ref):
    first = (pl.program_id(0) == 0) & (pl.program_id(1) == 0)

    @pl.when(first)
    def _():
        carry_ref[...] = jnp.zeros_like(carry_ref)

    x = x_ref[0]
    mod = mod_ref[0, 0]
    a = jnp.concatenate([mla_ref[0], diff_ref[0], swa_ref[0]], axis=-1)
    x1 = x + mod[2:3] * jnp.dot(a, wout_ref[0], preferred_element_type=F32)
    x1_ref[0] = x1
    f = _modulate(x1, g_ref[0], mod[3:4], mod[4:5])
    nch = f.shape[1] // LANES
    for c in range(nch):
        f_ref[pl.ds(c, TM, stride=nch), :] = f[:, c * LANES:(c + 1) * LANES]

    logits = jnp.dot(f.astype(BF16), wr_ref[0], preferred_element_type=F32) + br_ref[0]
    lane = _lane_iota((TM, LANES))
    vals, hots = [], []
    for _ in range(TOP_K):
        m = jnp.max(logits, axis=-1, keepdims=True)
        idx = jnp.min(jnp.where(logits == m, lane, LANES), axis=-1, keepdims=True)
        hot = lane == idx
        logits = jnp.where(hot, -3e38, logits)
        vals.append(m)
        hots.append((idx, hot))
    es = [jnp.exp(v - vals[0]) for v in vals]
    denom = functools.reduce(jnp.add, es)
    sel = functools.reduce(jnp.add, [jnp.where(hot, 1.0, 0.0) for _, hot in hots])

    r_io = lax.broadcasted_iota(I32, (TM, TM), 0)
    c_io = lax.broadcasted_iota(I32, (TM, TM), 1)
    tril = jnp.where(c_io < r_io, 1.0, 0.0).astype(BF16)
    rank = carry_ref[0:1, :] + jnp.dot(tril, sel.astype(BF16), preferred_element_type=F32)
    carry_ref[0:1, :] = carry_ref[0:1, :] + jnp.sum(sel, axis=0, keepdims=True)
    cnt_ref[...] = jnp.broadcast_to(carry_ref[0:1, :], cnt_ref.shape)

    route = jnp.zeros((TM, LANES), I32)
    rw = jnp.zeros((TM, LANES), F32)
    for k, (idx, hot) in enumerate(hots):
        rk = jnp.sum(jnp.where(hot, rank, 0.0), axis=-1, keepdims=True).astype(I32)
        route = jnp.where(lane == k, idx, route)
        route = jnp.where(lane == TOP_K + k, rk, route)
        rw = jnp.where(lane == k, es[k] / denom, rw)
    route_ref[...] = route
    rw_ref[...] = rw


def _post_attn(layer, xs, mods, mla_o, diff_o, swa_o, w_out_b, g_ffn, w_router_p, b_router_p, n_ctx, n_exp):
    b, nt, d = xs.shape
    t = b * nt
    nch = d // LANES
    n_ctx_tiles = n_ctx // TM
    tiles = nt // TM

    def tok(w):
        return pl.BlockSpec((1, TM, w), lambda bi, j: (bi, j, 0))

    def lay(shape):
        return pl.BlockSpec((1,) + shape, lambda bi, j: (layer,) + (0,) * len(shape))

    def flat(rows, w):
        return pl.BlockSpec((rows, w), lambda bi, j: (bi * tiles + j, 0))

    mod_spec = pl.BlockSpec((1, 1, N_MOD, d), lambda bi, j: (layer, jnp.where(j < n_ctx_tiles, b, bi), 0, 0))
    return pl.pallas_call(
        functools.partial(_post_attn_kernel, n_exp),
        grid=(b, tiles),
        in_specs=[tok(d), mod_spec, tok(256), tok(256), tok(512), lay(w_out_b.shape[1:]), lay((1, d)),
                  lay((d, LANES)), lay((1, LANES))],
        out_specs=[tok(d), flat(TM * nch, LANES), flat(TM, LANES), flat(TM, LANES),
                   pl.BlockSpec((SUBLANES, LANES), lambda bi, j: (0, 0))],
        out_shape=[jax.ShapeDtypeStruct((b, nt, d), F32), jax.ShapeDtypeStruct((t * nch, LANES), F32),
                   jax.ShapeDtypeStruct((t, LANES), I32), jax.ShapeDtypeStruct((t, LANES), F32),
                   jax.ShapeDtypeStruct((SUBLANES, LANES), F32)],
        scratch_shapes=[pltpu.VMEM((SUBLANES, LANES), F32)],
        compiler_params=_params("arbitrary", "arbitrary"),
        name="post_attn",
    )(xs, mods, mla_o, diff_o, swa_o, w_out_b, g_ffn, w_router_p, b_router_p)


def _dispatch_kernel(n_exp, nch, zstart_ref, zflag_ref, slot_ref, f_ref, xs_ref, zbuf, zsem, sem):
    i = pl.program_id(0)

    @pl.when(i == 0)
    def _():
        zbuf[...] = jnp.zeros_like(zbuf)
        for e in range(n_exp):
            @pl.when(zflag_ref[e] > 0)
            def _():
                start = pl.multiple_of(zstart_ref[e] * nch, SUBLANES)
                cp = pltpu.make_async_copy(zbuf, xs_ref.at[pl.ds(start, TMOE * nch)], zsem)
                cp.start()
                cp.wait()

    def body(r, carry):
        src = f_ref.at[pl.ds(pl.multiple_of((i * TD + r) * nch, nch), nch)]
        for k in range(TOP_K):
            dst = pl.multiple_of(slot_ref[0, 0, k * TD + r] * nch, nch)
            pltpu.make_async_copy(src, xs_ref.at[pl.ds(dst, nch)], sem).start()
        return carry

    lax.fori_loop(0, TD, body, 0)
    n_rows = TOP_K * TD * nch
    pltpu.make_async_copy(f_ref.at[pl.ds(0, n_rows)], xs_ref.at[pl.ds(0, n_rows)], sem).wait()


def _dispatch(f_rows, slots, zstart, zflag, n_slots, n_exp):
    nch_t = f_rows.shape[0]
    t = slots.shape[0]
    nch = nch_t // t
    slots_km = slots.reshape(t // TD, TD, TOP_K).transpose(0, 2, 1).reshape(t // TD, 1, TOP_K * TD)
    return pl.pallas_call(
        functools.partial(_dispatch_kernel, n_exp, nch),
        grid_spec=pltpu.PrefetchScalarGridSpec(
            num_scalar_prefetch=2,
            grid=(t // TD,),
            in_specs=[pl.BlockSpec((1, 1, TOP_K * TD), lambda i, zs, zf: (i, 0, 0), memory_space=pltpu.SMEM),
                      pl.BlockSpec(memory_space=pl.ANY)],
            out_specs=pl.BlockSpec(memory_space=pl.ANY),
            scratch_shapes=[pltpu.VMEM((TMOE * nch, LANES), F32), pltpu.SemaphoreType.DMA, pltpu.SemaphoreType.DMA]),
        out_shape=jax.ShapeDtypeStruct((n_slots * nch, LANES), F32),
        compiler_params=_params("arbitrary"),
        name="dispatch",
    )(zstart, zflag, slots_km, f_rows)


def _moe_kernel(nch, te_ref, nu_ref, x_ref, wg_ref, bg_ref, wu_ref, bu_ref, wd_ref, bd_ref, y_ref):
    @pl.when(pl.program_id(0) < nu_ref[0])
    def _():
        x = jnp.concatenate([x_ref[pl.ds(c, TMOE, stride=nch), :] for c in range(nch)], axis=-1).astype(BF16)
        gate = jnp.dot(x, wg_ref[0], preferred_element_type=F32) + bg_ref[0]
        up = jnp.dot(x, wu_ref[0], preferred_element_type=F32) + bu_ref[0]
        gate = jnp.minimum(gate, SWIGLU_LIMIT)
        up = jnp.clip(up, -SWIGLU_LIMIT, SWIGLU_LIMIT)
        act = gate * (1.0 / (1.0 + jnp.exp(-SWIGLU_ALPHA * gate))) * (up + 1.0)
        y = jnp.dot(act.astype(BF16), wd_ref[0], preferred_element_type=F32) + bd_ref[0]
        for c in range(nch):
            y_ref[pl.ds(c, TMOE, stride=nch), :] = y[:, c * LANES:(c + 1) * LANES]


def _moe(layer, xs_rows, tile_e, n_used, wg, bg, wu, bu, wd, bd, n_exp):
    d, f = wg.shape[1], wg.shape[2]
    nch = d // LANES
    n_tiles = xs_rows.shape[0] // (TMOE * nch)

    def rows(i, te, nu):
        return (jnp.minimum(i, nu[0] - 1), 0)

    def exp(i, te, nu):
        return (layer * n_exp + te[i], 0, 0)

    return pl.pallas_call(
        functools.partial(_moe_kernel, nch),
        grid_spec=pltpu.PrefetchScalarGridSpec(
            num_scalar_prefetch=2,
            grid=(n_tiles,),
            in_specs=[pl.BlockSpec((TMOE * nch, LANES), rows),
                      pl.BlockSpec((1, d, f), exp), pl.BlockSpec((1, 1, f), exp),
                      pl.BlockSpec((1, d, f), exp), pl.BlockSpec((1, 1, f), exp),
                      pl.BlockSpec((1, f, d), exp), pl.BlockSpec((1, 1, d), exp)],
            out_specs=pl.BlockSpec((TMOE * nch, LANES), rows)),
        out_shape=jax.ShapeDtypeStruct(xs_rows.shape, F32),
        compiler_params=_params("arbitrary"),
        name="moe",
    )(tile_e, n_used, xs_rows, wg, bg, wu, bu, wd, bd)


def _combine_kernel(nch, slot_ref, y_ref, x1_ref, rw_ref, mod_ref, o_ref, ybuf, sem):
    def body(r, carry):
        for k in range(TOP_K):
            src = pl.multiple_of(slot_ref[0, 0, k * TM + r] * nch, nch)
            dst = pl.multiple_of((k * TM + r) * nch, nch)
            pltpu.make_async_copy(y_ref.at[pl.ds(src, nch)], ybuf.at[pl.ds(dst, nch)], sem).start()
        return carry

    lax.fori_loop(0, TM, body, 0)
    n_rows = TOP_K * TM * nch
    pltpu.make_async_copy(y_ref.at[pl.ds(0, n_rows)], ybuf, sem).wait()
    rw = rw_ref[...]
    acc = None
    for k in range(TOP_K):
        yk = jnp.concatenate([ybuf[pl.ds(k * TM * nch + c, TM, stride=nch), :] for c in range(nch)], axis=-1)
        term = rw[:, k:k + 1] * yk
        acc = term if acc is None else acc + term
    o_ref[0] = x1_ref[0] + mod_ref[0, 0][5:6] * acc


def _combine(layer, y_rows, slots, x1, rw, mods, n_ctx):
    b, nt, d = x1.shape
    nch = d // LANES
    tiles = nt // TM
    n_ctx_tiles = n_ctx // TM
    t = b * nt
    slots_km = slots.reshape(t // TM, TM, TOP_K).transpose(0, 2, 1).reshape(t // TM, 1, TOP_K * TM)
    return pl.pallas_call(
        functools.partial(_combine_kernel, nch),
        grid=(b, tiles),
        in_specs=[pl.BlockSpec((1, 1, TOP_K * TM), lambda bi, j: (bi * tiles + j, 0, 0), memory_space=pltpu.SMEM),
                  pl.BlockSpec(memory_space=pl.ANY),
                  pl.BlockSpec((1, TM, d), lambda bi, j: (bi, j, 0)),
                  pl.BlockSpec((TM, LANES), lambda bi, j: (bi * tiles + j, 0)),
                  pl.BlockSpec((1, 1, N_MOD, d), lambda bi, j: (layer, jnp.where(j < n_ctx_tiles, b, bi), 0, 0))],
        out_specs=pl.BlockSpec((1, TM, d), lambda bi, j: (bi, j, 0)),
        out_shape=jax.ShapeDtypeStruct((b, nt, d), F32),
        scratch_shapes=[pltpu.VMEM((TOP_K * TM * nch, LANES), F32), pltpu.SemaphoreType.DMA],
        compiler_params=_params("arbitrary", "arbitrary"),
        name="combine",
    )(slots_km, y_rows, x1, rw, mods)


def _final_kernel(x_ref, g_ref, o_ref):
    o_ref[0] = _rms(x_ref[0]) * g_ref[...]


def _final_norm(xs, g_final, n_ctx, s_len):
    b, nt, d = xs.shape
    off = n_ctx // TM
    return pl.pallas_call(
        _final_kernel,
        grid=(b, s_len // TM),
        in_specs=[pl.BlockSpec((1, TM, d), lambda bi, j: (bi, j + off, 0)), pl.BlockSpec((1, d), lambda bi, j: (0, 0))],
        out_specs=pl.BlockSpec((1, TM, d), lambda bi, j: (bi, j, 0)),
        out_shape=jax.ShapeDtypeStruct((b, s_len, d), F32),
        compiler_params=_params("arbitrary", "arbitrary"),
        name="final_norm",
    )(xs, g_final.reshape(1, d))


def _routing_plan(route, counts, n_exp, n_tiles):
    idx = route[:, 0:TOP_K]
    rank = route[:, TOP_K:2 * TOP_K]
    counts = counts[0, :n_exp].astype(I32)
    padded = ((counts + TMOE - 1) // TMOE) * TMOE
    ends = jnp.cumsum(padded)
    starts = ends - padded
    onehot = idx[..., None] == jnp.arange(n_exp, dtype=I32)
    slots = jnp.sum(jnp.where(onehot, starts, 0), axis=-1) + rank
    n_used = (ends[-1] // TMOE).reshape(1)
    tile_ids = jnp.minimum(jnp.arange(n_tiles, dtype=I32), n_used[0] - 1)
    tile_e = jnp.sum((tile_ids[:, None] >= (ends // TMOE)[None, :]).astype(I32), axis=-1)
    tile_e = jnp.minimum(tile_e, n_exp - 1)
    zstart = jnp.maximum(ends - TMOE, 0)
    zflag = (padded > 0).astype(I32)
    return slots, tile_e, n_used, zstart, zflag


def kernel(x, c, ctx, c_ctx, w_ada, b_ada, g_attn, w_in, mla_q_norm, mla_w_uq, mla_kv_norm, mla_w_ukv,
           diff_lambda, diff_subln, swa_sink, w_out, g_ffn, w_router, b_router, w_gate, b_gate, w_up, b_up,
           w_down, b_down, g_final):
    b, s_len, d = x.shape
    n_ctx = ctx.shape[1]
    n_layers = w_ada.shape[0]
    n_exp = w_router.shape[2]
    nt = n_ctx + s_len
    t = b * nt
    assert d % LANES == 0 and n_ctx % TM == 0 and s_len % TM == 0 and t % TD == 0 and s_len % GRID_W == 0
    assert n_exp <= LANES and s_len >= TM + 2 * WINDOW

    w_in_p = _gather_columns(w_in, _in_proj_columns()).astype(BF16)
    w_uq_p = _gather_columns(mla_w_uq, _uq_columns()).astype(BF16)
    w_ukv_p = _gather_columns(mla_w_ukv, _ukv_columns()).astype(BF16)
    w_out_b = w_out.astype(BF16)
    w_router_p = jnp.pad(w_router, ((0, 0), (0, 0), (0, LANES - n_exp))).astype(BF16)
    b_router_p = jnp.pad(b_router, ((0, 0), (0, LANES - n_exp)), constant_values=NEG_INF).reshape(n_layers, 1, LANES)
    wg = w_gate.astype(BF16).reshape(n_layers * n_exp, d, -1)
    wu = w_up.astype(BF16).reshape(n_layers * n_exp, d, -1)
    wd = w_down.astype(BF16).reshape(n_layers * n_exp, -1, d)
    bg = b_gate.reshape(n_layers * n_exp, 1, -1)
    bu = b_up.reshape(n_layers * n_exp, 1, -1)
    bd = b_down.reshape(n_layers * n_exp, 1, d)
    lam_p = jnp.pad(diff_lambda, ((0, 0), (0, 0), (0, LANES - DIFF_QK)))
    subln_p = jnp.tile(diff_subln, (1, LANES // DIFF_V)).reshape(n_layers, 1, LANES)
    sink_p = jnp.broadcast_to(swa_sink[:, :, None], (n_layers, SWA_HEADS, LANES))
    seg = jnp.asarray((np.arange(LANES)[:, None] // DIFF_V == np.arange(LANES)[None, :] // DIFF_V), F32)
    tables = _rope_tables(n_ctx, s_len)

    ada_rows = -(-(b + 1) // SUBLANES) * SUBLANES
    cc = jnp.concatenate([c, c_ctx[None, :], jnp.zeros((ada_rows - b - 1, d), F32)], axis=0)
    mods = _ada(cc, w_ada, b_ada).reshape(n_layers, ada_rows, N_MOD, d)

    n_slots = t * TOP_K + n_exp * TMOE
    n_tiles = n_slots // TMOE
    xs = jnp.concatenate([ctx, x], axis=1)
    for layer in range(n_layers):
        lambda_init = 0.8 - 0.6 * math.exp(-0.3 * layer)
        mq, mk, mv, dq, dk, dv, sq, sk, sv = _pre_attn(
            layer, xs, mods, g_attn.reshape(n_layers, 1, d), w_in_p, mla_q_norm.reshape(n_layers, 1, -1), w_uq_p,
            mla_kv_norm.reshape(n_layers, 1, -1), w_ukv_p, tables, n_ctx)
        mla_o = _attention(functools.partial(_mla_attn_kernel, n_ctx), mq, mk, mv, [], 256, "mla_attn")
        diff_o = _attention(functools.partial(_diff_attn_kernel, n_ctx, lambda_init), dq, dk, dv,
                            [lam_p[layer:layer + 1], subln_p[layer], seg], 256, "diff_attn")
        swa_o = _attention(functools.partial(_swa_attn_kernel, n_ctx), sq, sk, sv, [sink_p[layer:layer + 1]],
                           512, "swa_attn")
        x1, f_rows, route, rw, counts = _post_attn(
            layer, xs, mods, mla_o, diff_o, swa_o, w_out_b, g_ffn.reshape(n_layers, 1, d), w_router_p, b_router_p,
            n_ctx, n_exp)
        slots, tile_e, n_used, zstart, zflag = _routing_plan(route, counts, n_exp, n_tiles)
        xs_rows = _dispatch(f_rows, slots, zstart, zflag, n_slots, n_exp)
        y_rows = _moe(layer, xs_rows, tile_e, n_used, wg, bg, wu, bu, wd, bd, n_exp)
        xs = _combine(layer, y_rows, slots, x1, rw, mods, n_ctx)
    return _final_norm(xs, g_final, n_ctx, s_len)
```

```python
import functools
import math

import jax
import jax.numpy as jnp
import numpy as np
from jax import lax
from jax.experimental import pallas as pl
from jax.experimental.pallas import tpu as pltpu

F32 = jnp.float32
BF16 = jnp.bfloat16
I32 = jnp.int32
HIGHEST = lax.Precision.HIGHEST

LANES = 128
SUBLANES = 8
VMEM_LIMIT = 56 * 1024 * 1024

GRID_W = 64
ROPE_THETA = 10000.0
EPS = 1e-6
NEG_INF = -1e30
N_MOD = 6

MLA_HEADS, MLA_Q_RANK, MLA_KV_RANK, MLA_NOPE, MLA_ROPE, MLA_V = 4, 256, 128, 64, 32, 64
DIFF_HEADS, DIFF_QK = 4, 32
DIFF_V = 2 * DIFF_QK
SWA_HEADS, SWA_KV_HEADS, SWA_DIM, WINDOW = 8, 2, 64, 128
TOP_K = 4
SWIGLU_LIMIT = 7.0
SWIGLU_ALPHA = 1.702

TM = 256
TMOE = 256
TD = 1024

G_CQ, G_CKV, G_KROPE, G_DQ, G_DK, G_DV, G_SQ, G_SK, G_SV, N_GROUPS = 0, 2, 3, 4, 6, 8, 10, 14, 16, 18


def _params(*sem):
    return pltpu.CompilerParams(dimension_semantics=sem, vmem_limit_bytes=VMEM_LIMIT)


def _in_proj_columns():
    src = -np.ones(N_GROUPS * LANES, np.int64)
    o_ckv = MLA_Q_RANK
    o_kr = o_ckv + MLA_KV_RANK
    o_dq = o_kr + MLA_ROPE
    o_dk = o_dq + DIFF_HEADS * 2 * DIFF_QK
    o_dv = o_dk + DIFF_HEADS * 2 * DIFF_QK
    o_sq = o_dv + DIFF_HEADS * DIFF_V
    o_sk = o_sq + SWA_HEADS * SWA_DIM
    o_sv = o_sk + SWA_KV_HEADS * SWA_DIM
    src[G_CQ * LANES:G_CQ * LANES + MLA_Q_RANK] = np.arange(MLA_Q_RANK)
    src[G_CKV * LANES:G_CKV * LANES + MLA_KV_RANK] = o_ckv + np.arange(MLA_KV_RANK)
    src[G_KROPE * LANES + MLA_NOPE:G_KROPE * LANES + MLA_NOPE + MLA_ROPE] = o_kr + np.arange(MLA_ROPE)
    src[G_DQ * LANES:G_DQ * LANES + 256] = o_dq + np.arange(256)
    src[G_DK * LANES:G_DK * LANES + 256] = o_dk + np.arange(256)
    src[G_DV * LANES:G_DV * LANES + 256] = o_dv + np.arange(256)
    src[G_SQ * LANES:G_SQ * LANES + 512] = o_sq + np.arange(512)
    for kv in range(SWA_KV_HEADS):
        for half in range(2):
            lo = half * SWA_DIM
            src[(G_SK + kv) * LANES + lo:(G_SK + kv) * LANES + lo + SWA_DIM] = o_sk + kv * SWA_DIM + np.arange(SWA_DIM)
            src[(G_SV + kv) * LANES + lo:(G_SV + kv) * LANES + lo + SWA_DIM] = o_sv + kv * SWA_DIM + np.arange(SWA_DIM)
    return src


def _gather_columns(w, src):
    cols = jnp.take(w, jnp.asarray(np.maximum(src, 0), I32), axis=-1)
    return jnp.where(jnp.asarray(src >= 0), cols, 0.0)


def _uq_columns():
    src = -np.ones(MLA_HEADS * LANES, np.int64)
    hd = MLA_NOPE + MLA_ROPE
    for h in range(MLA_HEADS):
        src[h * LANES:h * LANES + hd] = h * hd + np.arange(hd)
    return src


def _ukv_columns():
    src = -np.ones(MLA_HEADS * LANES + MLA_HEADS * MLA_V, np.int64)
    hd = MLA_NOPE + MLA_V
    for h in range(MLA_HEADS):
        src[h * LANES:h * LANES + MLA_NOPE] = h * hd + np.arange(MLA_NOPE)
        src[MLA_HEADS * LANES + h * MLA_V:MLA_HEADS * LANES + (h + 1) * MLA_V] = h * hd + MLA_NOPE + np.arange(MLA_V)
    return src


def _rope_tables(n_ctx, s_len):
    rows = s_len // GRID_W

    def axial(rot_dim):
        n_freq = rot_dim // 4
        inv_freq = ROPE_THETA ** (-jnp.arange(n_freq, dtype=F32) / n_freq)
        row_pos = jnp.repeat(jnp.arange(rows, dtype=F32), GRID_W)
        col_pos = jnp.tile(jnp.arange(GRID_W, dtype=F32), rows)
        ang = jnp.concatenate([row_pos[:, None] * inv_freq, col_pos[:, None] * inv_freq], axis=-1)
        return jnp.cos(ang), jnp.sin(ang)

    def expand(cos, sin, lane_rot):
        half = cos.shape[1]
        lane_rot = np.asarray(lane_rot)
        idx = np.maximum(lane_rot, 0) % half
        is_rot = lane_rot >= 0
        lo = is_rot & (lane_rot < half)
        hi = is_rot & (lane_rot >= half)
        c = jnp.where(jnp.asarray(is_rot), cos[:, idx], 1.0)
        s_lo = jnp.where(jnp.asarray(lo), -sin[:, idx], 0.0)
        s_hi = jnp.where(jnp.asarray(hi), sin[:, idx], 0.0)
        ident = [jnp.ones((n_ctx, LANES), F32), jnp.zeros((n_ctx, LANES), F32), jnp.zeros((n_ctx, LANES), F32)]
        return [jnp.concatenate([i, t], axis=0) for i, t in zip(ident, (c, s_lo, s_hi))]

    cos_r, sin_r = axial(MLA_ROPE)
    cos_w, sin_w = axial(SWA_DIM)
    lanes = np.arange(LANES)
    mla_rot = np.where((lanes >= MLA_NOPE) & (lanes < MLA_NOPE + MLA_ROPE), lanes - MLA_NOPE, -1)
    return (expand(cos_r, sin_r, mla_rot) + expand(cos_r, sin_r, lanes % DIFF_QK)
            + expand(cos_w, sin_w, lanes % SWA_DIM))


def _rms(x):
    return x * lax.rsqrt(jnp.mean(x * x, axis=-1, keepdims=True) + EPS)


def _modulate(x, g, shift, scale):
    return (_rms(x) * g) * (1.0 + scale) + shift


def _rope(v, c, s_lo, s_hi, half):
    return v * c + pltpu.roll(v, half, 1) * s_hi + pltpu.roll(v, LANES - half, 1) * s_lo


def _qk(q, k):
    return lax.dot_general(q, k, (((1,), (1,)), ((), ())), preferred_element_type=F32)


def _lane_iota(shape):
    return lax.broadcasted_iota(I32, shape, len(shape) - 1)


def _ada_kernel(c_ref, w_ref, b_ref, o_ref):
    c = c_ref[...]
    a = c * (1.0 / (1.0 + jnp.exp(-c)))
    o_ref[0] = jnp.dot(a, w_ref[0], precision=HIGHEST, preferred_element_type=F32) + b_ref[0]


def _ada(cc, w_ada, b_ada):
    n_layers, d, n_out = w_ada.shape
    rows = cc.shape[0]
    tn = d
    return pl.pallas_call(
        _ada_kernel,
        grid=(n_layers, n_out // tn),
        in_specs=[pl.BlockSpec((rows, d), lambda l, n: (0, 0)),
                  pl.BlockSpec((1, d, tn), lambda l, n: (l, 0, n)),
                  pl.BlockSpec((1, 1, tn), lambda l, n: (l, 0, n))],
        out_specs=pl.BlockSpec((1, rows, tn), lambda l, n: (l, 0, n)),
        out_shape=jax.ShapeDtypeStruct((n_layers, rows, n_out), F32),
        compiler_params=_params("arbitrary", "arbitrary"),
        name="ada",
    )(cc, w_ada, b_ada.reshape(n_layers, 1, n_out))


def _pre_attn_kernel(x_ref, mod_ref, g_ref, win_ref, qn_ref, wuq_ref, kvn_ref, wukv_ref,
                     mc_ref, ml_ref, mh_ref, dc_ref, dl_ref, dh_ref, wc_ref, wl_ref, wh_ref,
                     mq_ref, mk_ref, mv_ref, dq_ref, dk_ref, dv_ref, sq_ref, sk_ref, sv_ref):
    x = x_ref[0]
    mod = mod_ref[0, 0]
    h = _modulate(x, g_ref[0], mod[0:1], mod[1:2])
    p = jnp.dot(h.astype(BF16), win_ref[0], preferred_element_type=F32)

    def grp(g, n=1):
        return p[:, g * LANES:(g + n) * LANES]

    mla_scale = (MLA_NOPE + MLA_ROPE) ** -0.5
    diff_scale = DIFF_QK ** -0.5
    swa_scale = SWA_DIM ** -0.5
    mla_tab = (mc_ref[...], ml_ref[...], mh_ref[...])
    diff_tab = (dc_ref[...], dl_ref[...], dh_ref[...])
    swa_tab = (wc_ref[...], wl_ref[...], wh_ref[...])

    cq = (_rms(grp(G_CQ, 2)) * qn_ref[0]).astype(BF16)
    q = jnp.dot(cq, wuq_ref[0], preferred_element_type=F32)
    ckv = (_rms(grp(G_CKV)) * kvn_ref[0]).astype(BF16)
    kv = jnp.dot(ckv, wukv_ref[0], preferred_element_type=F32)
    k_rope = _rope(grp(G_KROPE), *mla_tab, MLA_ROPE // 2)
    for hd in range(MLA_HEADS):
        sl = slice(hd * LANES, (hd + 1) * LANES)
        mq_ref[0, :, sl] = (_rope(q[:, sl], *mla_tab, MLA_ROPE // 2) * mla_scale).astype(BF16)
        mk_ref[0, :, sl] = (kv[:, sl] + k_rope).astype(BF16)
    mv_ref[0] = kv[:, MLA_HEADS * LANES:].astype(BF16)

    for g in range(2):
        sl = slice(g * LANES, (g + 1) * LANES)
        dq_ref[0, :, sl] = (_rope(grp(G_DQ + g), *diff_tab, DIFF_QK // 2) * diff_scale).astype(BF16)
        dk_ref[0, :, sl] = _rope(grp(G_DK + g), *diff_tab, DIFF_QK // 2).astype(BF16)
    dv_ref[0] = grp(G_DV, 2).astype(BF16)

    for g in range(4):
        sl = slice(g * LANES, (g + 1) * LANES)
        sq_ref[0, :, sl] = (_rope(grp(G_SQ + g), *swa_tab, SWA_DIM // 2) * swa_scale).astype(BF16)
    for g in range(2):
        sl = slice(g * LANES, (g + 1) * LANES)
        sk_ref[0, :, sl] = _rope(grp(G_SK + g), *swa_tab, SWA_DIM // 2).astype(BF16)
    sv_ref[0] = grp(G_SV, 2).astype(BF16)


def _pre_attn(layer, xs, mods, g_attn, w_in_p, q_norm, w_uq_p, kv_norm, w_ukv_p, tables, n_ctx):
    b, nt, d = xs.shape
    n_ctx_tiles = n_ctx // TM
    ctx_row = b

    def tok(w):
        return pl.BlockSpec((1, TM, w), lambda j, bi: (bi, j, 0))

    def lay(shape):
        return pl.BlockSpec((1,) + shape, lambda j, bi: (layer,) + (0,) * len(shape))

    tab = pl.BlockSpec((TM, LANES), lambda j, bi: (j, 0))
    mod_spec = pl.BlockSpec((1, 1, N_MOD, d), lambda j, bi: (layer, jnp.where(j < n_ctx_tiles, ctx_row, bi), 0, 0))
    widths = (512, 512, 256, 256, 256, 256, 512, 256, 256)
    return pl.pallas_call(
        _pre_attn_kernel,
        grid=(nt // TM, b),
        in_specs=[tok(d), mod_spec, lay((1, d)), lay(w_in_p.shape[1:]), lay((1, MLA_Q_RANK)),
                  lay(w_uq_p.shape[1:]), lay((1, MLA_KV_RANK)), lay(w_ukv_p.shape[1:])] + [tab] * 9,
        out_specs=[tok(w) for w in widths],
        out_shape=[jax.ShapeDtypeStruct((b, nt, w), BF16) for w in widths],
        compiler_params=_params("arbitrary", "arbitrary"),
        name="pre_attn",
    )(xs, mods, g_attn, w_in_p, q_norm, w_uq_p, kv_norm, w_ukv_p, *tables)


def _softmax_pv(s_parts, v_parts, sink=None):
    m = functools.reduce(jnp.maximum, [jnp.max(s, axis=-1, keepdims=True) for s in s_parts])
    if sink is not None:
        m = jnp.maximum(m, sink)
    ps = [jnp.exp(s - m) for s in s_parts]
    l = functools.reduce(jnp.add, [jnp.sum(p, axis=-1, keepdims=True) for p in ps])
    if sink is not None:
        l = l + jnp.exp(sink - m)
    o = functools.reduce(jnp.add, [jnp.dot(p.astype(BF16), v, preferred_element_type=F32)
                                   for p, v in zip(ps, v_parts)])
    return o / l


def _mla_attn_kernel(n_ctx, q_ref, k_ref, v_ref, o_ref):
    j = pl.program_id(1)

    def run(nk):
        lane = _lane_iota((TM, LANES))
        outs = []
        for hd in range(MLA_HEADS):
            sl = slice(hd * LANES, (hd + 1) * LANES)
            vs = slice((hd // 2) * LANES, (hd // 2 + 1) * LANES)
            s = _qk(q_ref[0, :, sl], k_ref[0, :nk, sl])
            outs.append(_softmax_pv([s], [v_ref[0, :nk, vs]]))
        for g in range(2):
            o_ref[0, :, g * LANES:(g + 1) * LANES] = jnp.where(lane < MLA_V, outs[2 * g], outs[2 * g + 1]).astype(BF16)

    @pl.when(j < n_ctx // TM)
    def _():
        run(n_ctx)

    @pl.when(j >= n_ctx // TM)
    def _():
        run(k_ref.shape[1])


def _diff_attn_kernel(n_ctx, lambda_init, q_ref, k_ref, v_ref, lam_ref, g_ref, seg_ref, o_ref):
    j = pl.program_id(1)
    lam = lam_ref[0]
    lam_full = (jnp.exp(jnp.sum(lam[0:1] * lam[1:2], axis=-1, keepdims=True))
                - jnp.exp(jnp.sum(lam[2:3] * lam[3:4], axis=-1, keepdims=True)) + lambda_init)

    def run(nk):
        lane = _lane_iota((TM, LANES))
        heads = []
        for hd in range(DIFF_HEADS):
            sl = slice((hd // 2) * LANES, (hd // 2 + 1) * LANES)
            q = q_ref[0, :, sl]
            k = k_ref[0, :nk, sl]
            v = v_ref[0, :nk, sl]
            a = []
            for comp in range(2):
                seg = (hd % 2) * 2 + comp
                qm = jnp.where((lane >= seg * DIFF_QK) & (lane < (seg + 1) * DIFF_QK), q, jnp.zeros_like(q))
                a.append(_softmax_pv([_qk(qm, k)], [v]))
            heads.append(a[0] - lam_full * a[1])
        for g in range(2):
            o = jnp.where(lane < DIFF_V, heads[2 * g], heads[2 * g + 1])
            ms = jnp.dot(o * o, seg_ref[...], precision=HIGHEST, preferred_element_type=F32) * (1.0 / DIFF_V)
            o = o * lax.rsqrt(ms + EPS) * g_ref[...] * (1.0 - lambda_init)
            o_ref[0, :, g * LANES:(g + 1) * LANES] = o.astype(BF16)

    @pl.when(j < n_ctx // TM)
    def _():
        run(n_ctx)

    @pl.when(j >= n_ctx // TM)
    def _():
        run(k_ref.shape[1])


def _swa_attn_kernel(n_ctx, q_ref, k_ref, v_ref, sink_ref, o_ref):
    j = pl.program_id(1)
    nt = k_ref.shape[1]
    band = TM + 2 * WINDOW
    lane = _lane_iota((TM, LANES))

    def heads(group):
        sl = slice(group * LANES, (group + 1) * LANES)
        kv = group // 2
        q = q_ref[0, :, sl]
        for half in range(2):
            hd = 2 * group + half
            qm = jnp.where((lane >= half * SWA_DIM) & (lane < (half + 1) * SWA_DIM), q, jnp.zeros_like(q))
            yield hd, kv, qm

    def store(group, outs):
        o_ref[0, :, group * LANES:(group + 1) * LANES] = jnp.where(lane < SWA_DIM, outs[0], outs[1]).astype(BF16)

    @pl.when(j < n_ctx // TM)
    def _():
        for group in range(4):
            outs = []
            for hd, kv, qm in heads(group):
                ks = slice(kv * LANES, (kv + 1) * LANES)
                s = _qk(qm, k_ref[0, :n_ctx, ks])
                outs.append(_softmax_pv([s], [v_ref[0, :n_ctx, ks]], sink_ref[0, hd:hd + 1, 0:1]))
            store(group, outs)

    @pl.when(j >= n_ctx // TM)
    def _():
        q0 = j * TM
        w0 = pl.multiple_of(jnp.clip(q0 - WINDOW, n_ctx, nt - band), WINDOW)
        q_pos = q0 + lax.broadcasted_iota(I32, (TM, band), 0)
        k_pos = w0 + lax.broadcasted_iota(I32, (TM, band), 1)
        allowed = jnp.abs(k_pos - q_pos) <= WINDOW
        for group in range(4):
            outs = []
            for hd, kv, qm in heads(group):
                ks = slice(kv * LANES, (kv + 1) * LANES)
                s_ctx = _qk(qm, k_ref[0, :n_ctx, ks])
                s_band = jnp.where(allowed, _qk(qm, k_ref[0, pl.ds(w0, band), ks]), NEG_INF)
                outs.append(_softmax_pv([s_ctx, s_band],
                                        [v_ref[0, :n_ctx, ks], v_ref[0, pl.ds(w0, band), ks]],
                                        sink_ref[0, hd:hd + 1, 0:1]))
            store(group, outs)


def _attention(kernel, q, k, v, extra, out_width, name):
    b, nt, _ = q.shape

    def kv_spec(a):
        return pl.BlockSpec((1, nt, a.shape[2]), lambda bi, j: (bi, 0, 0))

    extra_specs = [pl.BlockSpec(e.shape, lambda bi, j, nd=e.ndim: (0,) * nd) for e in extra]
    return pl.pallas_call(
        kernel,
        grid=(b, nt // TM),
        in_specs=[pl.BlockSpec((1, TM, q.shape[2]), lambda bi, j: (bi, j, 0)), kv_spec(k), kv_spec(v)] + extra_specs,
        out_specs=pl.BlockSpec((1, TM, out_width), lambda bi, j: (bi, j, 0)),
        out_shape=jax.ShapeDtypeStruct((b, nt, out_width), BF16),
        compiler_params=_params("arbitrary", "arbitrary"),
        name=name,
    )(q, k, v, *extra)


def _post_attn_kernel(n_exp, x_ref, mod_ref, mla_ref, diff_ref, swa_ref, wout_ref, g_ref, wr_ref, br_ref,
                      x1_ref, f_ref, route_ref, rw_ref, cnt_ref, carry_ref):
    first = (pl.program_id(0) == 0) & (pl.program_id(1) == 0)

    @pl.when(first)
    def _():
        carry_ref[...] = jnp.zeros_like(carry_ref)

    x = x_ref[0]
    mod = mod_ref[0, 0]
    a = jnp.concatenate([mla_ref[0], diff_ref[0], swa_ref[0]], axis=-1)
    x1 = x + mod[2:3] * jnp.dot(a, wout_ref[0], preferred_element_type=F32)
    x1_ref[0] = x1
    f = _modulate(x1, g_ref[0], mod[3:4], mod[4:5])
    nch = f.shape[1] // LANES
    for c in range(nch):
        f_ref[pl.ds(c, TM, stride=nch), :] = f[:, c * LANES:(c + 1) * LANES]

    logits = jnp.dot(f.astype(BF16), wr_ref[0], preferred_element_type=F32) + br_ref[0]
    lane = _lane_iota((TM, LANES))
    vals, hots = [], []
    for _ in range(TOP_K):
        m = jnp.max(logits, axis=-1, keepdims=True)
        idx = jnp.min(jnp.where(logits == m, lane, LANES), axis=-1, keepdims=True)
        hot = lane == idx
        logits = jnp.where(hot, -3e38, logits)
        vals.append(m)
        hots.append((idx, hot))
    es = [jnp.exp(v - vals[0]) for v in vals]
    denom = functools.reduce(jnp.add, es)
    sel = functools.reduce(jnp.add, [jnp.where(hot, 1.0, 0.0) for _, hot in hots])

    r_io = lax.broadcasted_iota(I32, (TM, TM), 0)
    c_io = lax.broadcasted_iota(I32, (TM, TM), 1)
    tril = jnp.where(c_io < r_io, 1.0, 0.0).astype(BF16)
    rank = carry_ref[0:1, :] + jnp.dot(tril, sel.astype(BF16), preferred_element_type=F32)
    carry_ref[0:1, :] = carry_ref[0:1, :] + jnp.sum(sel, axis=0, keepdims=True)
    cnt_ref[...] = jnp.broadcast_to(carry_ref[0:1, :], cnt_ref.shape)

    route = jnp.zeros((TM, LANES), I32)
    rw = jnp.zeros((TM, LANES), F32)
    for k, (idx, hot) in enumerate(hots):
        rk = jnp.sum(jnp.where(hot, rank, 0.0), axis=-1, keepdims=True).astype(I32)
        route = jnp.where(lane == k, idx, route)
        route = jnp.where(lane == TOP_K + k, rk, route)
        rw = jnp.where(lane == k, es[k] / denom, rw)
    route_ref[...] = route
    rw_ref[...] = rw


def _post_attn(layer, xs, mods, mla_o, diff_o, swa_o, w_out_b, g_ffn, w_router_p, b_router_p, n_ctx, n_exp):
    b, nt, d = xs.shape
    t = b * nt
    nch = d // LANES
    n_ctx_tiles = n_ctx // TM
    tiles = nt // TM

    def tok(w):
        return pl.BlockSpec((1, TM, w), lambda bi, j: (bi, j, 0))

    def lay(shape):
        return pl.BlockSpec((1,) + shape, lambda bi, j: (layer,) + (0,) * len(shape))

    def flat(rows, w):
        return pl.BlockSpec((rows, w), lambda bi, j: (bi * tiles + j, 0))

    mod_spec = pl.BlockSpec((1, 1, N_MOD, d), lambda bi, j: (layer, jnp.where(j < n_ctx_tiles, b, bi), 0, 0))
    return pl.pallas_call(
        functools.partial(_post_attn_kernel, n_exp),
        grid=(b, tiles),
        in_specs=[tok(d), mod_spec, tok(256), tok(256), tok(512), lay(w_out_b.shape[1:]), lay((1, d)),
                  lay((d, LANES)), lay((1, LANES))],
        out_specs=[tok(d), flat(TM * nch, LANES), flat(TM, LANES), flat(TM, LANES),
                   pl.BlockSpec((SUBLANES, LANES), lambda bi, j: (0, 0))],
        out_shape=[jax.ShapeDtypeStruct((b, nt, d), F32), jax.ShapeDtypeStruct((t * nch, LANES), F32),
                   jax.ShapeDtypeStruct((t, LANES), I32), jax.ShapeDtypeStruct((t, LANES), F32),
                   jax.ShapeDtypeStruct((SUBLANES, LANES), F32)],
        scratch_shapes=[pltpu.VMEM((SUBLANES, LANES), F32)],
        compiler_params=_params("arbitrary", "arbitrary"),
        name="post_attn",
    )(xs, mods, mla_o, diff_o, swa_o, w_out_b, g_ffn, w_router_p, b_router_p)


def _dispatch_kernel(n_exp, nch, n_tiles, zstart_ref, zflag_ref, nu_ref, slot_ref, f_ref, xs_ref, zbuf, zsem, sem):
    @pl.when(pl.program_id(0) == 0)
    def _():
        zbuf[...] = jnp.zeros_like(zbuf)

        def zero_tile(row_start):
            dst = xs_ref.at[pl.ds(pl.multiple_of(row_start * nch, SUBLANES), TMOE * nch)]
            cp = pltpu.make_async_copy(zbuf, dst, zsem)
            cp.start()
            cp.wait()

        for e in range(n_exp):
            @pl.when(zflag_ref[e] > 0)
            def _():
                zero_tile(zstart_ref[e])

        def tail(tile, carry):
            zero_tile(tile * TMOE)
            return carry

        lax.fori_loop(nu_ref[0], n_tiles, tail, 0)

    def body(r, carry):
        src = f_ref.at[pl.ds(pl.multiple_of(r * nch, nch), nch)]
        for k in range(TOP_K):
            dst = pl.multiple_of(slot_ref[0, 0, k * TD + r] * nch, nch)
            pltpu.make_async_copy(src, xs_ref.at[pl.ds(dst, nch)], sem).start()
        return carry

    lax.fori_loop(0, TD, body, 0)
    for _ in range(TOP_K):
        pltpu.make_async_copy(f_ref, xs_ref.at[pl.ds(0, TD * nch)], sem).wait()


def _dispatch(f_rows, slots, zstart, zflag, n_used, n_slots, n_exp):
    nch_t = f_rows.shape[0]
    t = slots.shape[0]
    nch = nch_t // t
    slots_km = slots.reshape(t // TD, TD, TOP_K).transpose(0, 2, 1).reshape(t // TD, 1, TOP_K * TD)
    return pl.pallas_call(
        functools.partial(_dispatch_kernel, n_exp, nch, n_slots // TMOE),
        grid_spec=pltpu.PrefetchScalarGridSpec(
            num_scalar_prefetch=3,
            grid=(t // TD,),
            in_specs=[pl.BlockSpec((1, 1, TOP_K * TD), lambda i, zs, zf, nu: (i, 0, 0), memory_space=pltpu.SMEM),
                      pl.BlockSpec((TD * nch, LANES), lambda i, zs, zf, nu: (i, 0))],
            out_specs=pl.BlockSpec(memory_space=pl.ANY),
            scratch_shapes=[pltpu.VMEM((TMOE * nch, LANES), F32), pltpu.SemaphoreType.DMA, pltpu.SemaphoreType.DMA]),
        out_shape=jax.ShapeDtypeStruct((n_slots * nch, LANES), F32),
        compiler_params=_params("arbitrary"),
        name="dispatch",
    )(zstart, zflag, n_used, slots_km, f_rows)


def _moe_kernel(nch, te_ref, nu_ref, x_ref, wg_ref, bg_ref, wu_ref, bu_ref, wd_ref, bd_ref, y_ref,
                wg_b, wu_b, wd_b):
    i = pl.program_id(0)

    @pl.when((i == 0) | (te_ref[i] != te_ref[jnp.maximum(i - 1, 0)]))
    def _():
        wg_b[...] = wg_ref[0].astype(BF16)
        wu_b[...] = wu_ref[0].astype(BF16)
        wd_b[...] = wd_ref[0].astype(BF16)

    @pl.when(i < nu_ref[0])
    def _():
        x = jnp.concatenate([x_ref[pl.ds(c, TMOE, stride=nch), :] for c in range(nch)], axis=-1).astype(BF16)
        gate = jnp.dot(x, wg_b[...], preferred_element_type=F32) + bg_ref[0]
        up = jnp.dot(x, wu_b[...], preferred_element_type=F32) + bu_ref[0]
        gate = jnp.minimum(gate, SWIGLU_LIMIT)
        up = jnp.clip(up, -SWIGLU_LIMIT, SWIGLU_LIMIT)
        act = gate * (1.0 / (1.0 + jnp.exp(-SWIGLU_ALPHA * gate))) * (up + 1.0)
        y = jnp.dot(act.astype(BF16), wd_b[...], preferred_element_type=F32) + bd_ref[0]
        for c in range(nch):
            y_ref[pl.ds(c, TMOE, stride=nch), :] = y[:, c * LANES:(c + 1) * LANES]

    @pl.when(pl.program_id(0) >= nu_ref[0])
    def _():
        y_ref[...] = jnp.zeros_like(y_ref)


def _moe(layer, xs_rows, tile_e, n_used, wg, bg, wu, bu, wd, bd, n_exp):
    d, f = wg.shape[1], wg.shape[2]
    nch = d // LANES
    n_tiles = xs_rows.shape[0] // (TMOE * nch)

    def rows(i, te, nu):
        return (jnp.minimum(i, nu[0] - 1), 0)

    def exp(i, te, nu):
        return (layer * n_exp + te[i], 0, 0)

    return pl.pallas_call(
        functools.partial(_moe_kernel, nch),
        grid_spec=pltpu.PrefetchScalarGridSpec(
            num_scalar_prefetch=2,
            grid=(n_tiles,),
            in_specs=[pl.BlockSpec((TMOE * nch, LANES), rows),
                      pl.BlockSpec((1, d, f), exp), pl.BlockSpec((1, 1, f), exp),
                      pl.BlockSpec((1, d, f), exp), pl.BlockSpec((1, 1, f), exp),
                      pl.BlockSpec((1, f, d), exp), pl.BlockSpec((1, 1, d), exp)],
            out_specs=pl.BlockSpec((TMOE * nch, LANES), lambda i, te, nu: (i, 0)),
            scratch_shapes=[pltpu.VMEM((d, f), BF16), pltpu.VMEM((d, f), BF16), pltpu.VMEM((f, d), BF16)]),
        out_shape=jax.ShapeDtypeStruct(xs_rows.shape, F32),
        compiler_params=_params("arbitrary"),
        name="moe",
    )(tile_e, n_used, xs_rows, wg, bg, wu, bu, wd, bd)


def _combine_kernel(nch, slot_ref, y_ref, x1_ref, rw_ref, mod_ref, o_ref, ybuf, sem):
    def body(r, carry):
        for k in range(TOP_K):
            src = pl.multiple_of(slot_ref[0, 0, k * TM + r] * nch, nch)
            dst = pl.multiple_of((k * TM + r) * nch, nch)
            pltpu.make_async_copy(y_ref.at[pl.ds(src, nch)], ybuf.at[pl.ds(dst, nch)], sem).start()
        return carry

    lax.fori_loop(0, TM, body, 0)
    n_rows = TOP_K * TM * nch
    pltpu.make_async_copy(y_ref.at[pl.ds(0, n_rows)], ybuf, sem).wait()
    rw = rw_ref[...]
    acc = None
    for k in range(TOP_K):
        yk = jnp.concatenate([ybuf[pl.ds(k * TM * nch + c, TM, stride=nch), :] for c in range(nch)], axis=-1)
        term = rw[:, k:k + 1] * yk
        acc = term if acc is None else acc + term
    o_ref[0] = x1_ref[0] + mod_ref[0, 0][5:6] * acc


def _combine(layer, y_rows, slots, x1, rw, mods, n_ctx):
    b, nt, d = x1.shape
    nch = d // LANES
    tiles = nt // TM
    n_ctx_tiles = n_ctx // TM
    t = b * nt
    slots_km = slots.reshape(t // TM, TM, TOP_K).transpose(0, 2, 1).reshape(t // TM, 1, TOP_K * TM)
    return pl.pallas_call(
        functools.partial(_combine_kernel, nch),
        grid=(b, tiles),
        in_specs=[pl.BlockSpec((1, 1, TOP_K * TM), lambda bi, j: (bi * tiles + j, 0, 0), memory_space=pltpu.SMEM),
                  pl.BlockSpec(memory_space=pl.ANY),
                  pl.BlockSpec((1, TM, d), lambda bi, j: (bi, j, 0)),
                  pl.BlockSpec((TM, LANES), lambda bi, j: (bi * tiles + j, 0)),
                  pl.BlockSpec((1, 1, N_MOD, d), lambda bi, j: (layer, jnp.where(j < n_ctx_tiles, b, bi), 0, 0))],
        out_specs=pl.BlockSpec((1, TM, d), lambda bi, j: (bi, j, 0)),
        out_shape=jax.ShapeDtypeStruct((b, nt, d), F32),
        scratch_shapes=[pltpu.VMEM((TOP_K * TM * nch, LANES), F32), pltpu.SemaphoreType.DMA],
        compiler_params=_params("arbitrary", "arbitrary"),
        name="combine",
    )(slots_km, y_rows, x1, rw, mods)


def _final_kernel(x_ref, g_ref, o_ref):
    o_ref[0] = _rms(x_ref[0]) * g_ref[...]


def _final_norm(xs, g_final, n_ctx, s_len):
    b, nt, d = xs.shape
    off = n_ctx // TM
    return pl.pallas_call(
        _final_kernel,
        grid=(b, s_len // TM),
        in_specs=[pl.BlockSpec((1, TM, d), lambda bi, j: (bi, j + off, 0)), pl.BlockSpec((1, d), lambda bi, j: (0, 0))],
        out_specs=pl.BlockSpec((1, TM, d), lambda bi, j: (bi, j, 0)),
        out_shape=jax.ShapeDtypeStruct((b, s_len, d), F32),
        compiler_params=_params("arbitrary", "arbitrary"),
        name="final_norm",
    )(xs, g_final.reshape(1, d))


def _routing_plan(route, counts, n_exp, n_tiles):
    idx = route[:, 0:TOP_K]
    rank = route[:, TOP_K:2 * TOP_K]
    counts = counts[0, :n_exp].astype(I32)
    padded = ((counts + TMOE - 1) // TMOE) * TMOE
    ends = jnp.cumsum(padded)
    starts = ends - padded
    onehot = idx[..., None] == jnp.arange(n_exp, dtype=I32)
    slots = jnp.sum(jnp.where(onehot, starts, 0), axis=-1) + rank
    n_used = (ends[-1] // TMOE).reshape(1)
    tile_ids = jnp.minimum(jnp.arange(n_tiles, dtype=I32), n_used[0] - 1)
    tile_e = jnp.sum((tile_ids[:, None] >= (ends // TMOE)[None, :]).astype(I32), axis=-1)
    tile_e = jnp.minimum(tile_e, n_exp - 1)
    zstart = jnp.maximum(ends - TMOE, 0)
    zflag = (padded > 0).astype(I32)
    return slots, tile_e, n_used, zstart, zflag


def kernel(x, c, ctx, c_ctx, w_ada, b_ada, g_attn, w_in, mla_q_norm, mla_w_uq, mla_kv_norm, mla_w_ukv,
           diff_lambda, diff_subln, swa_sink, w_out, g_ffn, w_router, b_router, w_gate, b_gate, w_up, b_up,
           w_down, b_down, g_final):
    b, s_len, d = x.shape
    n_ctx = ctx.shape[1]
    n_layers = w_ada.shape[0]
    n_exp = w_router.shape[2]
    nt = n_ctx + s_len
    t = b * nt
    assert d % LANES == 0 and n_ctx % TM == 0 and s_len % TM == 0 and t % TD == 0 and s_len % GRID_W == 0
    assert n_exp <= LANES and s_len >= TM + 2 * WINDOW

    w_in_p = _gather_columns(w_in, _in_proj_columns()).astype(BF16)
    w_uq_p = _gather_columns(mla_w_uq, _uq_columns()).astype(BF16)
    w_ukv_p = _gather_columns(mla_w_ukv, _ukv_columns()).astype(BF16)
    w_out_b = w_out.astype(BF16)
    w_router_p = jnp.pad(w_router, ((0, 0), (0, 0), (0, LANES - n_exp))).astype(BF16)
    b_router_p = jnp.pad(b_router, ((0, 0), (0, LANES - n_exp)), constant_values=NEG_INF).reshape(n_layers, 1, LANES)
    wg = w_gate.reshape(n_layers * n_exp, d, -1)
    wu = w_up.reshape(n_layers * n_exp, d, -1)
    wd = w_down.reshape(n_layers * n_exp, -1, d)
    bg = b_gate.reshape(n_layers * n_exp, 1, -1)
    bu = b_up.reshape(n_layers * n_exp, 1, -1)
    bd = b_down.reshape(n_layers * n_exp, 1, d)
    lam_p = jnp.pad(diff_lambda, ((0, 0), (0, 0), (0, LANES - DIFF_QK)))
    subln_p = jnp.tile(diff_subln, (1, LANES // DIFF_V)).reshape(n_layers, 1, LANES)
    sink_p = jnp.broadcast_to(swa_sink[:, :, None], (n_layers, SWA_HEADS, LANES))
    seg = jnp.asarray((np.arange(LANES)[:, None] // DIFF_V == np.arange(LANES)[None, :] // DIFF_V), F32)
    tables = _rope_tables(n_ctx, s_len)

    ada_rows = -(-(b + 1) // SUBLANES) * SUBLANES
    cc = jnp.concatenate([c, c_ctx[None, :], jnp.zeros((ada_rows - b - 1, d), F32)], axis=0)
    mods = _ada(cc, w_ada, b_ada).reshape(n_layers, ada_rows, N_MOD, d)

    n_slots = t * TOP_K + n_exp * TMOE
    n_tiles = n_slots // TMOE
    xs = jnp.concatenate([ctx, x], axis=1)
    for layer in range(n_layers):
        lambda_init = 0.8 - 0.6 * math.exp(-0.3 * layer)
        mq, mk, mv, dq, dk, dv, sq, sk, sv = _pre_attn(
            layer, xs, mods, g_attn.reshape(n_layers, 1, d), w_in_p, mla_q_norm.reshape(n_layers, 1, -1), w_uq_p,
            mla_kv_norm.reshape(n_layers, 1, -1), w_ukv_p, tables, n_ctx)
        mla_o = _attention(functools.partial(_mla_attn_kernel, n_ctx), mq, mk, mv, [], 256, "mla_attn")
        diff_o = _attention(functools.partial(_diff_attn_kernel, n_ctx, lambda_init), dq, dk, dv,
                            [lam_p[layer:layer + 1], subln_p[layer], seg], 256, "diff_attn")
        swa_o = _attention(functools.partial(_swa_attn_kernel, n_ctx), sq, sk, sv, [sink_p[layer:layer + 1]],
                           512, "swa_attn")
        x1, f_rows, route, rw, counts = _post_attn(
            layer, xs, mods, mla_o, diff_o, swa_o, w_out_b, g_ffn.reshape(n_layers, 1, d), w_router_p, b_router_p,
            n_ctx, n_exp)
        slots, tile_e, n_used, zstart, zflag = _routing_plan(route, counts, n_exp, n_tiles)
        xs_rows = _dispatch(f_rows, slots, zstart, zflag, n_used, n_slots, n_exp)
        y_rows = _moe(layer, xs_rows, tile_e, n_used, wg, bg, wu, bu, wd, bd, n_exp)
        xs = _combine(layer, y_rows, slots, x1, rw, mods, n_ctx)
    return _final_norm(xs, g_final, n_ctx, s_len)
```

```python
import functools
import math

import jax
import jax.numpy as jnp
import numpy as np
from jax import lax
from jax.experimental import pallas as pl
from jax.experimental.pallas import tpu as pltpu

F32 = jnp.float32
BF16 = jnp.bfloat16
I32 = jnp.int32
HIGHEST = lax.Precision.HIGHEST
LOG2E = math.log2(math.e)

LANES = 128
SUBLANES = 8
VMEM_LIMIT = 56 * 1024 * 1024

GRID_W = 64
ROPE_THETA = 10000.0
EPS = 1e-6
NEG_INF = -1e30
N_MOD = 6

MLA_HEADS, MLA_Q_RANK, MLA_KV_RANK, MLA_NOPE, MLA_ROPE, MLA_V = 4, 256, 128, 64, 32, 64
DIFF_HEADS, DIFF_QK = 4, 32
DIFF_V = 2 * DIFF_QK
SWA_HEADS, SWA_KV_HEADS, SWA_DIM, WINDOW = 8, 2, 64, 128
TOP_K = 4
SWIGLU_LIMIT = 7.0
SWIGLU_ALPHA = 1.702

TM = 256
TQ = 512
TMOE = 256
TD = 1024

G_CQ, G_CKV, G_KROPE, G_DQ, G_DK, G_DV, G_SQ, G_SK, G_SV, N_GROUPS = 0, 2, 3, 4, 6, 8, 10, 14, 16, 18


def _params(*sem):
    return pltpu.CompilerParams(dimension_semantics=sem, vmem_limit_bytes=VMEM_LIMIT)


def _in_proj_columns():
    src = -np.ones(N_GROUPS * LANES, np.int64)
    o_ckv = MLA_Q_RANK
    o_kr = o_ckv + MLA_KV_RANK
    o_dq = o_kr + MLA_ROPE
    o_dk = o_dq + DIFF_HEADS * 2 * DIFF_QK
    o_dv = o_dk + DIFF_HEADS * 2 * DIFF_QK
    o_sq = o_dv + DIFF_HEADS * DIFF_V
    o_sk = o_sq + SWA_HEADS * SWA_DIM
    o_sv = o_sk + SWA_KV_HEADS * SWA_DIM
    src[G_CQ * LANES:G_CQ * LANES + MLA_Q_RANK] = np.arange(MLA_Q_RANK)
    src[G_CKV * LANES:G_CKV * LANES + MLA_KV_RANK] = o_ckv + np.arange(MLA_KV_RANK)
    src[G_KROPE * LANES + MLA_NOPE:G_KROPE * LANES + MLA_NOPE + MLA_ROPE] = o_kr + np.arange(MLA_ROPE)
    src[G_DQ * LANES:G_DQ * LANES + 256] = o_dq + np.arange(256)
    src[G_DK * LANES:G_DK * LANES + 256] = o_dk + np.arange(256)
    src[G_DV * LANES:G_DV * LANES + 256] = o_dv + np.arange(256)
    src[G_SQ * LANES:G_SQ * LANES + 512] = o_sq + np.arange(512)
    for kv in range(SWA_KV_HEADS):
        for half in range(2):
            lo = half * SWA_DIM
            src[(G_SK + kv) * LANES + lo:(G_SK + kv) * LANES + lo + SWA_DIM] = o_sk + kv * SWA_DIM + np.arange(SWA_DIM)
            src[(G_SV + kv) * LANES + lo:(G_SV + kv) * LANES + lo + SWA_DIM] = o_sv + kv * SWA_DIM + np.arange(SWA_DIM)
    return src


def _gather_columns(w, src):
    cols = jnp.take(w, jnp.asarray(np.maximum(src, 0), I32), axis=-1)
    return jnp.where(jnp.asarray(src >= 0), cols, 0.0)


def _uq_columns():
    src = -np.ones(MLA_HEADS * LANES, np.int64)
    hd = MLA_NOPE + MLA_ROPE
    for h in range(MLA_HEADS):
        src[h * LANES:h * LANES + hd] = h * hd + np.arange(hd)
    return src


def _ukv_columns():
    src = -np.ones(MLA_HEADS * LANES + MLA_HEADS * MLA_V, np.int64)
    hd = MLA_NOPE + MLA_V
    for h in range(MLA_HEADS):
        src[h * LANES:h * LANES + MLA_NOPE] = h * hd + np.arange(MLA_NOPE)
        src[MLA_HEADS * LANES + h * MLA_V:MLA_HEADS * LANES + (h + 1) * MLA_V] = h * hd + MLA_NOPE + np.arange(MLA_V)
    return src


def _rope_tables(n_ctx, s_len):
    rows = s_len // GRID_W

    def axial(rot_dim):
        n_freq = rot_dim // 4
        inv_freq = ROPE_THETA ** (-jnp.arange(n_freq, dtype=F32) / n_freq)
        row_pos = jnp.repeat(jnp.arange(rows, dtype=F32), GRID_W)
        col_pos = jnp.tile(jnp.arange(GRID_W, dtype=F32), rows)
        ang = jnp.concatenate([row_pos[:, None] * inv_freq, col_pos[:, None] * inv_freq], axis=-1)
        return jnp.cos(ang), jnp.sin(ang)

    def expand(cos, sin, lane_rot):
        half = cos.shape[1]
        lane_rot = np.asarray(lane_rot)
        idx = np.maximum(lane_rot, 0) % half
        is_rot = lane_rot >= 0
        lo = is_rot & (lane_rot < half)
        hi = is_rot & (lane_rot >= half)
        c = jnp.where(jnp.asarray(is_rot), cos[:, idx], 1.0)
        s_lo = jnp.where(jnp.asarray(lo), -sin[:, idx], 0.0)
        s_hi = jnp.where(jnp.asarray(hi), sin[:, idx], 0.0)
        ident = [jnp.ones((n_ctx, LANES), F32), jnp.zeros((n_ctx, LANES), F32), jnp.zeros((n_ctx, LANES), F32)]
        return [jnp.concatenate([i, t], axis=0) for i, t in zip(ident, (c, s_lo, s_hi))]

    cos_r, sin_r = axial(MLA_ROPE)
    cos_w, sin_w = axial(SWA_DIM)
    lanes = np.arange(LANES)
    mla_rot = np.where((lanes >= MLA_NOPE) & (lanes < MLA_NOPE + MLA_ROPE), lanes - MLA_NOPE, -1)
    return (expand(cos_r, sin_r, mla_rot) + expand(cos_r, sin_r, lanes % DIFF_QK)
            + expand(cos_w, sin_w, lanes % SWA_DIM))


def _rms(x):
    return x * lax.rsqrt(jnp.mean(x * x, axis=-1, keepdims=True) + EPS)


def _modulate(x, g, shift, scale):
    return (_rms(x) * g) * (1.0 + scale) + shift


def _rope(v, c, s_lo, s_hi, half):
    return v * c + pltpu.roll(v, half, 1) * s_hi + pltpu.roll(v, LANES - half, 1) * s_lo


def _qk(q, k):
    return lax.dot_general(q, k, (((1,), (1,)), ((), ())), preferred_element_type=F32)


def _lane_iota(shape):
    return lax.broadcasted_iota(I32, shape, len(shape) - 1)


def _masked_rows(q, segs, width):
    lane = _lane_iota(q.shape)
    zero = jnp.zeros_like(q)
    return jnp.concatenate([jnp.where((lane >= sg * width) & (lane < (sg + 1) * width), q, zero) for sg in segs], axis=0)


def _ada_kernel(c_ref, w_ref, b_ref, o_ref):
    c = c_ref[...]
    a = c * (1.0 / (1.0 + jnp.exp(-c)))
    o_ref[0] = jnp.dot(a, w_ref[0], precision=HIGHEST, preferred_element_type=F32) + b_ref[0]


def _ada(cc, w_ada, b_ada):
    n_layers, d, n_out = w_ada.shape
    rows = cc.shape[0]
    tn = d
    return pl.pallas_call(
        _ada_kernel,
        grid=(n_layers, n_out // tn),
        in_specs=[pl.BlockSpec((rows, d), lambda l, n: (0, 0)),
                  pl.BlockSpec((1, d, tn), lambda l, n: (l, 0, n)),
                  pl.BlockSpec((1, 1, tn), lambda l, n: (l, 0, n))],
        out_specs=pl.BlockSpec((1, rows, tn), lambda l, n: (l, 0, n)),
        out_shape=jax.ShapeDtypeStruct((n_layers, rows, n_out), F32),
        compiler_params=_params("arbitrary", "arbitrary"),
        name="ada",
    )(cc, w_ada, b_ada.reshape(n_layers, 1, n_out))


def _pre_attn_kernel(x_ref, mod_ref, g_ref, win_ref, qn_ref, wuq_ref, kvn_ref, wukv_ref,
                     mc_ref, ml_ref, mh_ref, dc_ref, dl_ref, dh_ref, wc_ref, wl_ref, wh_ref,
                     mq_ref, mk_ref, mv_ref, dq_ref, dk_ref, dv_ref, sq_ref, sk_ref, sv_ref):
    x = x_ref[0]
    mod = mod_ref[0, 0]
    h = _modulate(x, g_ref[0], mod[0:1], mod[1:2])
    p = jnp.dot(h.astype(BF16), win_ref[0], preferred_element_type=F32)

    def grp(g, n=1):
        return p[:, g * LANES:(g + n) * LANES]

    mla_scale = LOG2E * (MLA_NOPE + MLA_ROPE) ** -0.5
    diff_scale = LOG2E * DIFF_QK ** -0.5
    swa_scale = LOG2E * SWA_DIM ** -0.5
    mla_tab = (mc_ref[...], ml_ref[...], mh_ref[...])
    diff_tab = (dc_ref[...], dl_ref[...], dh_ref[...])
    swa_tab = (wc_ref[...], wl_ref[...], wh_ref[...])

    cq = (_rms(grp(G_CQ, 2)) * qn_ref[0]).astype(BF16)
    q = jnp.dot(cq, wuq_ref[0], preferred_element_type=F32)
    ckv = (_rms(grp(G_CKV)) * kvn_ref[0]).astype(BF16)
    kv = jnp.dot(ckv, wukv_ref[0], preferred_element_type=F32)
    k_rope = _rope(grp(G_KROPE), *mla_tab, MLA_ROPE // 2)
    for hd in range(MLA_HEADS):
        sl = slice(hd * LANES, (hd + 1) * LANES)
        mq_ref[0, :, sl] = (_rope(q[:, sl], *mla_tab, MLA_ROPE // 2) * mla_scale).astype(BF16)
        mk_ref[0, :, sl] = (kv[:, sl] + k_rope).astype(BF16)
    mv_ref[0] = kv[:, MLA_HEADS * LANES:].astype(BF16)

    for g in range(2):
        sl = slice(g * LANES, (g + 1) * LANES)
        dq_ref[0, :, sl] = (_rope(grp(G_DQ + g), *diff_tab, DIFF_QK // 2) * diff_scale).astype(BF16)
        dk_ref[0, :, sl] = _rope(grp(G_DK + g), *diff_tab, DIFF_QK // 2).astype(BF16)
    dv_ref[0] = grp(G_DV, 2).astype(BF16)

    for g in range(4):
        sl = slice(g * LANES, (g + 1) * LANES)
        sq_ref[0, :, sl] = (_rope(grp(G_SQ + g), *swa_tab, SWA_DIM // 2) * swa_scale).astype(BF16)
    for g in range(2):
        sl = slice(g * LANES, (g + 1) * LANES)
        sk_ref[0, :, sl] = _rope(grp(G_SK + g), *swa_tab, SWA_DIM // 2).astype(BF16)
    sv_ref[0] = grp(G_SV, 2).astype(BF16)


def _pre_attn(layer, xs, mods, g_attn, w_in_p, q_norm, w_uq_p, kv_norm, w_ukv_p, tables, n_ctx):
    b, nt, d = xs.shape
    n_ctx_tiles = n_ctx // TM
    ctx_row = b

    def tok(w):
        return pl.BlockSpec((1, TM, w), lambda j, bi: (bi, j, 0))

    def lay(shape):
        return pl.BlockSpec((1,) + shape, lambda j, bi: (layer,) + (0,) * len(shape))

    tab = pl.BlockSpec((TM, LANES), lambda j, bi: (j, 0))
    mod_spec = pl.BlockSpec((1, 1, N_MOD, d), lambda j, bi: (layer, jnp.where(j < n_ctx_tiles, ctx_row, bi), 0, 0))
    widths = (512, 512, 256, 256, 256, 256, 512, 256, 256)
    return pl.pallas_call(
        _pre_attn_kernel,
        grid=(nt // TM, b),
        in_specs=[tok(d), mod_spec, lay((1, d)), lay(w_in_p.shape[1:]), lay((1, MLA_Q_RANK)),
                  lay(w_uq_p.shape[1:]), lay((1, MLA_KV_RANK)), lay(w_ukv_p.shape[1:])] + [tab] * 9,
        out_specs=[tok(w) for w in widths],
        out_shape=[jax.ShapeDtypeStruct((b, nt, w), BF16) for w in widths],
        compiler_params=_params("arbitrary", "arbitrary"),
        name="pre_attn",
    )(xs, mods, g_attn, w_in_p, q_norm, w_uq_p, kv_norm, w_ukv_p, *tables)


def _softmax_pv(s, v):
    p = jnp.exp2(s - jnp.max(s, axis=-1, keepdims=True))
    l = jnp.sum(p, axis=-1, keepdims=True)
    return jnp.dot(p.astype(BF16), v, preferred_element_type=F32) / l


def _dense_steps(n_ctx, nt, run):
    j = pl.program_id(1)

    @pl.when(j == 0)
    def _():
        run(0, n_ctx, n_ctx)

    @pl.when(j > 0)
    def _():
        run(pl.multiple_of(n_ctx + (j - 1) * TQ, TM), TQ, nt)


def _mla_attn_kernel(n_ctx, q_ref, k_ref, v_ref, o_ref):
    def run(row0, rows, nk):
        lane = _lane_iota((rows, LANES))
        outs = []
        for hd in range(MLA_HEADS):
            sl = slice(hd * LANES, (hd + 1) * LANES)
            vs = slice((hd // 2) * LANES, (hd // 2 + 1) * LANES)
            s = _qk(q_ref[0, pl.ds(row0, rows), sl], k_ref[0, :nk, sl])
            outs.append(_softmax_pv(s, v_ref[0, :nk, vs]))
        for g in range(2):
            o_ref[0, pl.ds(row0, rows), g * LANES:(g + 1) * LANES] = jnp.where(
                lane < MLA_V, outs[2 * g], outs[2 * g + 1]).astype(BF16)

    _dense_steps(n_ctx, k_ref.shape[1], run)


def _diff_attn_kernel(n_ctx, lambda_init, q_ref, k_ref, v_ref, lam_ref, g_ref, seg_ref, o_ref):
    lam = lam_ref[0]
    lam_full = (jnp.exp(jnp.sum(lam[0:1] * lam[1:2], axis=-1, keepdims=True))
                - jnp.exp(jnp.sum(lam[2:3] * lam[3:4], axis=-1, keepdims=True)) + lambda_init)

    def run(row0, rows, nk):
        lane = _lane_iota((rows, LANES))
        heads = []
        for hd in range(DIFF_HEADS):
            sl = slice((hd // 2) * LANES, (hd // 2 + 1) * LANES)
            q = q_ref[0, pl.ds(row0, rows), sl]
            k = k_ref[0, :nk, sl]
            v = v_ref[0, :nk, sl]
            a = []
            for comp in range(2):
                seg = (hd % 2) * 2 + comp
                qm = jnp.where((lane >= seg * DIFF_QK) & (lane < (seg + 1) * DIFF_QK), q, jnp.zeros_like(q))
                a.append(_softmax_pv(_qk(qm, k), v))
            heads.append(a[0] - lam_full * a[1])
        for g in range(2):
            o = jnp.where(lane < DIFF_V, heads[2 * g], heads[2 * g + 1])
            ms = jnp.dot(o * o, seg_ref[...], precision=HIGHEST, preferred_element_type=F32) * (1.0 / DIFF_V)
            o = o * lax.rsqrt(ms + EPS) * g_ref[...] * (1.0 - lambda_init)
            o_ref[0, pl.ds(row0, rows), g * LANES:(g + 1) * LANES] = o.astype(BF16)

    _dense_steps(n_ctx, k_ref.shape[1], run)


def _swa_attn_kernel(n_ctx, q_ref, k_ref, v_ref, sink_ref, o_ref):
    j = pl.program_id(1)
    nt = k_ref.shape[1]
    band = TM + 2 * WINDOW
    lane = _lane_iota((TM, LANES))
    group_heads = SWA_HEADS // SWA_KV_HEADS

    def run(kv, k, v, allowed):
        qs = jnp.concatenate([_masked_rows(q_ref[0, :, g * LANES:(g + 1) * LANES], range(2), SWA_DIM)
                              for g in (2 * kv, 2 * kv + 1)], axis=0)
        s = _qk(qs, k)
        if allowed is not None:
            s = jnp.where(jnp.concatenate([allowed] * group_heads, axis=0), s, NEG_INF)
        sink = jnp.concatenate([jnp.broadcast_to(sink_ref[0, hd:hd + 1, 0:1] * LOG2E, (TM, 1))
                                for hd in range(kv * group_heads, (kv + 1) * group_heads)], axis=0)
        m = jnp.maximum(jnp.max(s, axis=-1, keepdims=True), sink)
        p = jnp.exp2(s - m)
        l = jnp.sum(p, axis=-1, keepdims=True) + jnp.exp2(sink - m)
        o = jnp.dot(p.astype(BF16), v, preferred_element_type=F32) / l
        for i in range(2):
            g = 2 * kv + i
            o_ref[0, :, g * LANES:(g + 1) * LANES] = jnp.where(
                lane < SWA_DIM, o[2 * i * TM:(2 * i + 1) * TM], o[(2 * i + 1) * TM:(2 * i + 2) * TM]).astype(BF16)

    @pl.when(j < n_ctx // TM)
    def _():
        for kv in range(SWA_KV_HEADS):
            ks = slice(kv * LANES, (kv + 1) * LANES)
            run(kv, k_ref[0, :n_ctx, ks], v_ref[0, :n_ctx, ks], None)

    @pl.when(j >= n_ctx // TM)
    def _():
        q0 = j * TM
        w0 = pl.multiple_of(jnp.clip(q0 - WINDOW, n_ctx, nt - band), WINDOW)
        q_pos = q0 + lax.broadcasted_iota(I32, (TM, n_ctx + band), 0)
        col = lax.broadcasted_iota(I32, (TM, n_ctx + band), 1)
        k_pos = w0 + col - n_ctx
        allowed = (col < n_ctx) | (jnp.abs(k_pos - q_pos) <= WINDOW)
        for kv in range(SWA_KV_HEADS):
            ks = slice(kv * LANES, (kv + 1) * LANES)
            k = jnp.concatenate([k_ref[0, :n_ctx, ks], k_ref[0, pl.ds(w0, band), ks]], axis=0)
            v = jnp.concatenate([v_ref[0, :n_ctx, ks], v_ref[0, pl.ds(w0, band), ks]], axis=0)
            run(kv, k, v, allowed)


def _attention(kernel, q, k, v, extra, out_width, name, dense_steps=None):
    b, nt, _ = q.shape

    def whole(width):
        return pl.BlockSpec((1, nt, width), lambda bi, j: (bi, 0, 0))

    def tile(width):
        return pl.BlockSpec((1, TM, width), lambda bi, j: (bi, j, 0))

    q_spec, o_spec, steps = (tile, tile, nt // TM) if dense_steps is None else (whole, whole, dense_steps)
    extra_specs = [pl.BlockSpec(e.shape, lambda bi, j, nd=e.ndim: (0,) * nd) for e in extra]
    return pl.pallas_call(
        kernel,
        grid=(b, steps),
        in_specs=[q_spec(q.shape[2]), whole(k.shape[2]), whole(v.shape[2])] + extra_specs,
        out_specs=o_spec(out_width),
        out_shape=jax.ShapeDtypeStruct((b, nt, out_width), BF16),
        compiler_params=_params("arbitrary", "arbitrary"),
        name=name,
    )(q, k, v, *extra)


def _post_attn_kernel(n_exp, x_ref, mod_ref, mla_ref, diff_ref, swa_ref, wout_ref, g_ref, wr_ref, br_ref,
                      x1_ref, f_ref, route_ref, rw_ref, cnt_ref, carry_ref):
    first = (pl.program_id(0) == 0) & (pl.program_id(1) == 0)

    @pl.when(first)
    def _():
        carry_ref[...] = jnp.zeros_like(carry_ref)

    x = x_ref[0]
    mod = mod_ref[0, 0]
    a = jnp.concatenate([mla_ref[0], diff_ref[0], swa_ref[0]], axis=-1)
    x1 = x + mod[2:3] * jnp.dot(a, wout_ref[0], preferred_element_type=F32)
    x1_ref[0] = x1
    f = _modulate(x1, g_ref[0], mod[3:4], mod[4:5])
    nch = f.shape[1] // LANES
    for c in range(nch):
        f_ref[pl.ds(c, TM, stride=nch), :] = f[:, c * LANES:(c + 1) * LANES]

    logits = jnp.dot(f.astype(BF16), wr_ref[0], preferred_element_type=F32) + br_ref[0]
    lane = _lane_iota((TM, LANES))
    vals, hots = [], []
    for _ in range(TOP_K):
        m = jnp.max(logits, axis=-1, keepdims=True)
        idx = jnp.min(jnp.where(logits == m, lane, LANES), axis=-1, keepdims=True)
        hot = lane == idx
        logits = jnp.where(hot, -3e38, logits)
        vals.append(m)
        hots.append((idx, hot))
    es = [jnp.exp(v - vals[0]) for v in vals]
    denom = functools.reduce(jnp.add, es)
    sel = functools.reduce(jnp.add, [jnp.where(hot, 1.0, 0.0) for _, hot in hots])

    r_io = lax.broadcasted_iota(I32, (TM, TM), 0)
    c_io = lax.broadcasted_iota(I32, (TM, TM), 1)
    tril = jnp.where(c_io < r_io, 1.0, 0.0).astype(BF16)
    rank = carry_ref[0:1, :] + jnp.dot(tril, sel.astype(BF16), preferred_element_type=F32)
    carry_ref[0:1, :] = carry_ref[0:1, :] + jnp.sum(sel, axis=0, keepdims=True)
    cnt_ref[...] = jnp.broadcast_to(carry_ref[0:1, :], cnt_ref.shape)

    route = jnp.zeros((TM, LANES), I32)
    rw = jnp.zeros((TM, LANES), F32)
    for k, (idx, hot) in enumerate(hots):
        rk = jnp.sum(jnp.where(hot, rank, 0.0), axis=-1, keepdims=True).astype(I32)
        route = jnp.where(lane == k, idx, route)
        route = jnp.where(lane == TOP_K + k, rk, route)
        rw = jnp.where(lane == k, es[k] / denom, rw)
    route_ref[...] = route
    rw_ref[...] = rw


def _post_attn(layer, xs, mods, mla_o, diff_o, swa_o, w_out_b, g_ffn, w_router_p, b_router_p, n_ctx, n_exp):
    b, nt, d = xs.shape
    t = b * nt
    nch = d // LANES
    n_ctx_tiles = n_ctx // TM
    tiles = nt // TM

    def tok(w):
        return pl.BlockSpec((1, TM, w), lambda bi, j: (bi, j, 0))

    def lay(shape):
        return pl.BlockSpec((1,) + shape, lambda bi, j: (layer,) + (0,) * len(shape))

    def flat(rows, w):
        return pl.BlockSpec((rows, w), lambda bi, j: (bi * tiles + j, 0))

    mod_spec = pl.BlockSpec((1, 1, N_MOD, d), lambda bi, j: (layer, jnp.where(j < n_ctx_tiles, b, bi), 0, 0))
    return pl.pallas_call(
        functools.partial(_post_attn_kernel, n_exp),
        grid=(b, tiles),
        in_specs=[tok(d), mod_spec, tok(256), tok(256), tok(512), lay(w_out_b.shape[1:]), lay((1, d)),
                  lay((d, LANES)), lay((1, LANES))],
        out_specs=[tok(d), flat(TM * nch, LANES), flat(TM, LANES), flat(TM, LANES),
                   pl.BlockSpec((SUBLANES, LANES), lambda bi, j: (0, 0))],
        out_shape=[jax.ShapeDtypeStruct((b, nt, d), F32), jax.ShapeDtypeStruct((t * nch, LANES), F32),
                   jax.ShapeDtypeStruct((t, LANES), I32), jax.ShapeDtypeStruct((t, LANES), F32),
                   jax.ShapeDtypeStruct((SUBLANES, LANES), F32)],
        scratch_shapes=[pltpu.VMEM((SUBLANES, LANES), F32)],
        compiler_params=_params("arbitrary", "arbitrary"),
        name="post_attn",
    )(xs, mods, mla_o, diff_o, swa_o, w_out_b, g_ffn, w_router_p, b_router_p)


def _dispatch_kernel(n_exp, nch, n_tiles, zstart_ref, zflag_ref, nu_ref, slot_ref, f_ref, xs_ref, zbuf, zsem, sem):
    @pl.when(pl.program_id(0) == 0)
    def _():
        zbuf[...] = jnp.zeros_like(zbuf)

        def zero_tile(row_start):
            dst = xs_ref.at[pl.ds(pl.multiple_of(row_start * nch, SUBLANES), TMOE * nch)]
            cp = pltpu.make_async_copy(zbuf, dst, zsem)
            cp.start()
            cp.wait()

        for e in range(n_exp):
            @pl.when(zflag_ref[e] > 0)
            def _():
                zero_tile(zstart_ref[e])

        def tail(tile, carry):
            zero_tile(tile * TMOE)
            return carry

        lax.fori_loop(nu_ref[0], n_tiles, tail, 0)

    def body(r, carry):
        src = f_ref.at[pl.ds(pl.multiple_of(r * nch, nch), nch)]
        for k in range(TOP_K):
            dst = pl.multiple_of(slot_ref[0, 0, k * TD + r] * nch, nch)
            pltpu.make_async_copy(src, xs_ref.at[pl.ds(dst, nch)], sem).start()
        return carry

    lax.fori_loop(0, TD, body, 0)
    for _ in range(TOP_K):
        pltpu.make_async_copy(f_ref, xs_ref.at[pl.ds(0, TD * nch)], sem).wait()


def _dispatch(f_rows, slots, zstart, zflag, n_used, n_slots, n_exp):
    nch_t = f_rows.shape[0]
    t = slots.shape[0]
    nch = nch_t // t
    slots_km = slots.reshape(t // TD, TD, TOP_K).transpose(0, 2, 1).reshape(t // TD, 1, TOP_K * TD)
    return pl.pallas_call(
        functools.partial(_dispatch_kernel, n_exp, nch, n_slots // TMOE),
        grid_spec=pltpu.PrefetchScalarGridSpec(
            num_scalar_prefetch=3,
            grid=(t // TD,),
            in_specs=[pl.BlockSpec((1, 1, TOP_K * TD), lambda i, zs, zf, nu: (i, 0, 0), memory_space=pltpu.SMEM),
                      pl.BlockSpec((TD * nch, LANES), lambda i, zs, zf, nu: (i, 0))],
            out_specs=pl.BlockSpec(memory_space=pl.ANY),
            scratch_shapes=[pltpu.VMEM((TMOE * nch, LANES), F32), pltpu.SemaphoreType.DMA, pltpu.SemaphoreType.DMA]),
        out_shape=jax.ShapeDtypeStruct((n_slots * nch, LANES), F32),
        compiler_params=_params("arbitrary"),
        name="dispatch",
    )(zstart, zflag, n_used, slots_km, f_rows)


def _moe_kernel(nch, te_ref, nu_ref, x_ref, wg_ref, bg_ref, wu_ref, bu_ref, wd_ref, bd_ref, y_ref,
                wg_b, wu_b, wd_b):
    i = pl.program_id(0)

    @pl.when((i == 0) | (te_ref[i] != te_ref[jnp.maximum(i - 1, 0)]))
    def _():
        wg_b[...] = wg_ref[0].astype(BF16)
        wu_b[...] = wu_ref[0].astype(BF16)
        wd_b[...] = wd_ref[0].astype(BF16)

    @pl.when(i < nu_ref[0])
    def _():
        x = jnp.concatenate([x_ref[pl.ds(c, TMOE, stride=nch), :] for c in range(nch)], axis=-1).astype(BF16)
        gate = jnp.dot(x, wg_b[...], preferred_element_type=F32) + bg_ref[0]
        up = jnp.dot(x, wu_b[...], preferred_element_type=F32) + bu_ref[0]
        gate = jnp.minimum(gate, SWIGLU_LIMIT)
        up = jnp.clip(up, -SWIGLU_LIMIT, SWIGLU_LIMIT)
        act = gate * (1.0 / (1.0 + jnp.exp(-SWIGLU_ALPHA * gate))) * (up + 1.0)
        y = jnp.dot(act.astype(BF16), wd_b[...], preferred_element_type=F32) + bd_ref[0]
        for c in range(nch):
            y_ref[pl.ds(c, TMOE, stride=nch), :] = y[:, c * LANES:(c + 1) * LANES]

    @pl.when(pl.program_id(0) >= nu_ref[0])
    def _():
        y_ref[...] = jnp.zeros_like(y_ref)


def _moe(layer, xs_rows, tile_e, n_used, wg, bg, wu, bu, wd, bd, n_exp):
    d, f = wg.shape[1], wg.shape[2]
    nch = d // LANES
    n_tiles = xs_rows.shape[0] // (TMOE * nch)

    def rows(i, te, nu):
        return (jnp.minimum(i, nu[0] - 1), 0)

    def exp(i, te, nu):
        return (layer * n_exp + te[i], 0, 0)

    return pl.pallas_call(
        functools.partial(_moe_kernel, nch),
        grid_spec=pltpu.PrefetchScalarGridSpec(
            num_scalar_prefetch=2,
            grid=(n_tiles,),
            in_specs=[pl.BlockSpec((TMOE * nch, LANES), rows),
                      pl.BlockSpec((1, d, f), exp), pl.BlockSpec((1, 1, f), exp),
                      pl.BlockSpec((1, d, f), exp), pl.BlockSpec((1, 1, f), exp),
                      pl.BlockSpec((1, f, d), exp), pl.BlockSpec((1, 1, d), exp)],
            out_specs=pl.BlockSpec((TMOE * nch, LANES), lambda i, te, nu: (i, 0)),
            scratch_shapes=[pltpu.VMEM((d, f), BF16), pltpu.VMEM((d, f), BF16), pltpu.VMEM((f, d), BF16)]),
        out_shape=jax.ShapeDtypeStruct(xs_rows.shape, F32),
        compiler_params=_params("arbitrary"),
        name="moe",
    )(tile_e, n_used, xs_rows, wg, bg, wu, bu, wd, bd)


def _combine_kernel(nch, slot_ref, next_slot_ref, y_ref, x1_ref, rw_ref, mod_ref, o_ref, ybuf, sem):
    step = pl.program_id(0) * pl.num_programs(1) + pl.program_id(1)
    n_steps = pl.num_programs(0) * pl.num_programs(1)
    cur = step % 2
    n_rows = TOP_K * TM * nch

    def gather(slots, buf):
        def body(r, carry):
            for k in range(TOP_K):
                src = pl.multiple_of(slots[0, 0, k * TM + r] * nch, nch)
                dst = pl.multiple_of((k * TM + r) * nch, nch)
                pltpu.make_async_copy(y_ref.at[pl.ds(src, nch)], ybuf.at[buf, pl.ds(dst, nch)], sem.at[buf]).start()
            return carry

        lax.fori_loop(0, TM, body, 0)

    @pl.when(step == 0)
    def _():
        gather(slot_ref, 0)

    @pl.when(step + 1 < n_steps)
    def _():
        gather(next_slot_ref, 1 - cur)

    pltpu.make_async_copy(y_ref.at[pl.ds(0, n_rows)], ybuf.at[cur], sem.at[cur]).wait()
    rw = rw_ref[...]
    acc = None
    for k in range(TOP_K):
        yk = jnp.concatenate([ybuf[cur, pl.ds(k * TM * nch + c, TM, stride=nch), :] for c in range(nch)], axis=-1)
        term = rw[:, k:k + 1] * yk
        acc = term if acc is None else acc + term
    o_ref[0] = x1_ref[0] + mod_ref[0, 0][5:6] * acc


def _combine(layer, y_rows, slots, x1, rw, mods, n_ctx):
    b, nt, d = x1.shape
    nch = d // LANES
    tiles = nt // TM
    n_ctx_tiles = n_ctx // TM
    t = b * nt
    slots_km = slots.reshape(t // TM, TM, TOP_K).transpose(0, 2, 1).reshape(t // TM, 1, TOP_K * TM)
    return pl.pallas_call(
        functools.partial(_combine_kernel, nch),
        grid=(b, tiles),
        in_specs=[pl.BlockSpec((1, 1, TOP_K * TM), lambda bi, j: (bi * tiles + j, 0, 0), memory_space=pltpu.SMEM),
                  pl.BlockSpec((1, 1, TOP_K * TM), lambda bi, j: (jnp.minimum(bi * tiles + j + 1, t // TM - 1), 0, 0),
                               memory_space=pltpu.SMEM),
                  pl.BlockSpec(memory_space=pl.ANY),
                  pl.BlockSpec((1, TM, d), lambda bi, j: (bi, j, 0)),
                  pl.BlockSpec((TM, LANES), lambda bi, j: (bi * tiles + j, 0)),
                  pl.BlockSpec((1, 1, N_MOD, d), lambda bi, j: (layer, jnp.where(j < n_ctx_tiles, b, bi), 0, 0))],
        out_specs=pl.BlockSpec((1, TM, d), lambda bi, j: (bi, j, 0)),
        out_shape=jax.ShapeDtypeStruct((b, nt, d), F32),
        scratch_shapes=[pltpu.VMEM((2, TOP_K * TM * nch, LANES), F32), pltpu.SemaphoreType.DMA((2,))],
        compiler_params=_params("arbitrary", "arbitrary"),
        name="combine",
    )(slots_km, slots_km, y_rows, x1, rw, mods)


def _final_kernel(x_ref, g_ref, o_ref):
    o_ref[0] = _rms(x_ref[0]) * g_ref[...]


def _final_norm(xs, g_final, n_ctx, s_len):
    b, nt, d = xs.shape
    off = n_ctx // TM
    return pl.pallas_call(
        _final_kernel,
        grid=(b, s_len // TM),
        in_specs=[pl.BlockSpec((1, TM, d), lambda bi, j: (bi, j + off, 0)), pl.BlockSpec((1, d), lambda bi, j: (0, 0))],
        out_specs=pl.BlockSpec((1, TM, d), lambda bi, j: (bi, j, 0)),
        out_shape=jax.ShapeDtypeStruct((b, s_len, d), F32),
        compiler_params=_params("arbitrary", "arbitrary"),
        name="final_norm",
    )(xs, g_final.reshape(1, d))


def _routing_plan(route, counts, n_exp, n_tiles):
    idx = route[:, 0:TOP_K]
    rank = route[:, TOP_K:2 * TOP_K]
    counts = counts[0, :n_exp].astype(I32)
    padded = ((counts + TMOE - 1) // TMOE) * TMOE
    ends = jnp.cumsum(padded)
    starts = ends - padded
    onehot = idx[..., None] == jnp.arange(n_exp, dtype=I32)
    slots = jnp.sum(jnp.where(onehot, starts, 0), axis=-1) + rank
    n_used = (ends[-1] // TMOE).reshape(1)
    tile_ids = jnp.minimum(jnp.arange(n_tiles, dtype=I32), n_used[0] - 1)
    tile_e = jnp.sum((tile_ids[:, None] >= (ends // TMOE)[None, :]).astype(I32), axis=-1)
    tile_e = jnp.minimum(tile_e, n_exp - 1)
    zstart = jnp.maximum(ends - TMOE, 0)
    zflag = (padded > 0).astype(I32)
    return slots, tile_e, n_used, zstart, zflag


def kernel(x, c, ctx, c_ctx, w_ada, b_ada, g_attn, w_in, mla_q_norm, mla_w_uq, mla_kv_norm, mla_w_ukv,
           diff_lambda, diff_subln, swa_sink, w_out, g_ffn, w_router, b_router, w_gate, b_gate, w_up, b_up,
           w_down, b_down, g_final):
    b, s_len, d = x.shape
    n_ctx = ctx.shape[1]
    n_layers = w_ada.shape[0]
    n_exp = w_router.shape[2]
    nt = n_ctx + s_len
    t = b * nt
    assert d % LANES == 0 and n_ctx % TM == 0 and s_len % TQ == 0 and t % TD == 0 and s_len % GRID_W == 0
    assert n_exp <= LANES and s_len >= TM + 2 * WINDOW

    w_in_p = _gather_columns(w_in, _in_proj_columns()).astype(BF16)
    w_uq_p = _gather_columns(mla_w_uq, _uq_columns()).astype(BF16)
    w_ukv_p = _gather_columns(mla_w_ukv, _ukv_columns()).astype(BF16)
    w_out_b = w_out.astype(BF16)
    w_router_p = jnp.pad(w_router, ((0, 0), (0, 0), (0, LANES - n_exp))).astype(BF16)
    b_router_p = jnp.pad(b_router, ((0, 0), (0, LANES - n_exp)), constant_values=NEG_INF).reshape(n_layers, 1, LANES)
    wg = w_gate.reshape(n_layers * n_exp, d, -1)
    wu = w_up.reshape(n_layers * n_exp, d, -1)
    wd = w_down.reshape(n_layers * n_exp, -1, d)
    bg = b_gate.reshape(n_layers * n_exp, 1, -1)
    bu = b_up.reshape(n_layers * n_exp, 1, -1)
    bd = b_down.reshape(n_layers * n_exp, 1, d)
    lam_p = jnp.pad(diff_lambda, ((0, 0), (0, 0), (0, LANES - DIFF_QK)))
    subln_p = jnp.tile(diff_subln, (1, LANES // DIFF_V)).reshape(n_layers, 1, LANES)
    sink_p = jnp.broadcast_to(swa_sink[:, :, None], (n_layers, SWA_HEADS, LANES))
    seg = jnp.asarray((np.arange(LANES)[:, None] // DIFF_V == np.arange(LANES)[None, :] // DIFF_V), F32)
    tables = _rope_tables(n_ctx, s_len)

    ada_rows = -(-(b + 1) // SUBLANES) * SUBLANES
    cc = jnp.concatenate([c, c_ctx[None, :], jnp.zeros((ada_rows - b - 1, d), F32)], axis=0)
    mods = _ada(cc, w_ada, b_ada).reshape(n_layers, ada_rows, N_MOD, d)

    n_slots = t * TOP_K + n_exp * TMOE
    n_tiles = n_slots // TMOE
    xs = jnp.concatenate([ctx, x], axis=1)
    for layer in range(n_layers):
        lambda_init = 0.8 - 0.6 * math.exp(-0.3 * layer)
        mq, mk, mv, dq, dk, dv, sq, sk, sv = _pre_attn(
            layer, xs, mods, g_attn.reshape(n_layers, 1, d), w_in_p, mla_q_norm.reshape(n_layers, 1, -1), w_uq_p,
            mla_kv_norm.reshape(n_layers, 1, -1), w_ukv_p, tables, n_ctx)
        dense_steps = 1 + s_len // TQ
        mla_o = _attention(functools.partial(_mla_attn_kernel, n_ctx), mq, mk, mv, [], 256, "mla_attn", dense_steps)
        diff_o = _attention(functools.partial(_diff_attn_kernel, n_ctx, lambda_init), dq, dk, dv,
                            [lam_p[layer:layer + 1], subln_p[layer], seg], 256, "diff_attn", dense_steps)
        swa_o = _attention(functools.partial(_swa_attn_kernel, n_ctx), sq, sk, sv, [sink_p[layer:layer + 1]],
                           512, "swa_attn")
        x1, f_rows, route, rw, counts = _post_attn(
            layer, xs, mods, mla_o, diff_o, swa_o, w_out_b, g_ffn.reshape(n_layers, 1, d), w_router_p, b_router_p,
            n_ctx, n_exp)
        slots, tile_e, n_used, zstart, zflag = _routing_plan(route, counts, n_exp, n_tiles)
        xs_rows = _dispatch(f_rows, slots, zstart, zflag, n_used, n_slots, n_exp)
        y_rows = _moe(layer, xs_rows, tile_e, n_used, wg, bg, wu, bu, wd, bd, n_exp)
        xs = _combine(layer, y_rows, slots, x1, rw, mods, n_ctx)
    return _final_norm(xs, g_final, n_ctx, s_len)
```

```python
import functools
import math

import jax
import jax.numpy as jnp
import numpy as np
from jax import lax
from jax.experimental import pallas as pl
from jax.experimental.pallas import tpu as pltpu

F32 = jnp.float32
BF16 = jnp.bfloat16
I32 = jnp.int32
HIGHEST = lax.Precision.HIGHEST
LOG2E = math.log2(math.e)

LANES = 128
SUBLANES = 8
VMEM_LIMIT = 56 * 1024 * 1024

GRID_W = 64
ROPE_THETA = 10000.0
EPS = 1e-6
NEG_INF = -1e30
N_MOD = 6

MLA_HEADS, MLA_Q_RANK, MLA_KV_RANK, MLA_NOPE, MLA_ROPE, MLA_V = 4, 256, 128, 64, 32, 64
DIFF_HEADS, DIFF_QK = 4, 32
DIFF_V = 2 * DIFF_QK
SWA_HEADS, SWA_KV_HEADS, SWA_DIM, WINDOW = 8, 2, 64, 128
TOP_K = 4
SWIGLU_LIMIT = 7.0
SWIGLU_ALPHA = 1.702

TM = 256
TQ = 512
TMOE = 256
TD = 1024
GATHER_UNROLL = 8

G_CQ, G_CKV, G_KROPE, G_DQ, G_DK, G_DV, G_SQ, G_SK, G_SV, N_GROUPS = 0, 2, 3, 4, 6, 8, 10, 14, 16, 18


def _params(*sem):
    return pltpu.CompilerParams(dimension_semantics=sem, vmem_limit_bytes=VMEM_LIMIT)


def _in_proj_columns():
    src = -np.ones(N_GROUPS * LANES, np.int64)
    o_ckv = MLA_Q_RANK
    o_kr = o_ckv + MLA_KV_RANK
    o_dq = o_kr + MLA_ROPE
    o_dk = o_dq + DIFF_HEADS * 2 * DIFF_QK
    o_dv = o_dk + DIFF_HEADS * 2 * DIFF_QK
    o_sq = o_dv + DIFF_HEADS * DIFF_V
    o_sk = o_sq + SWA_HEADS * SWA_DIM
    o_sv = o_sk + SWA_KV_HEADS * SWA_DIM
    src[G_CQ * LANES:G_CQ * LANES + MLA_Q_RANK] = np.arange(MLA_Q_RANK)
    src[G_CKV * LANES:G_CKV * LANES + MLA_KV_RANK] = o_ckv + np.arange(MLA_KV_RANK)
    src[G_KROPE * LANES + MLA_NOPE:G_KROPE * LANES + MLA_NOPE + MLA_ROPE] = o_kr + np.arange(MLA_ROPE)
    src[G_DQ * LANES:G_DQ * LANES + 256] = o_dq + np.arange(256)
    src[G_DK * LANES:G_DK * LANES + 256] = o_dk + np.arange(256)
    src[G_DV * LANES:G_DV * LANES + 256] = o_dv + np.arange(256)
    src[G_SQ * LANES:G_SQ * LANES + 512] = o_sq + np.arange(512)
    for kv in range(SWA_KV_HEADS):
        for half in range(2):
            lo = half * SWA_DIM
            src[(G_SK + kv) * LANES + lo:(G_SK + kv) * LANES + lo + SWA_DIM] = o_sk + kv * SWA_DIM + np.arange(SWA_DIM)
            src[(G_SV + kv) * LANES + lo:(G_SV + kv) * LANES + lo + SWA_DIM] = o_sv + kv * SWA_DIM + np.arange(SWA_DIM)
    return src


def _gather_columns(w, src):
    cols = jnp.take(w, jnp.asarray(np.maximum(src, 0), I32), axis=-1)
    return jnp.where(jnp.asarray(src >= 0), cols, 0.0)


def _uq_columns():
    src = -np.ones(MLA_HEADS * LANES, np.int64)
    hd = MLA_NOPE + MLA_ROPE
    for h in range(MLA_HEADS):
        src[h * LANES:h * LANES + hd] = h * hd + np.arange(hd)
    return src


def _ukv_columns():
    src = -np.ones(MLA_HEADS * LANES + MLA_HEADS * MLA_V, np.int64)
    hd = MLA_NOPE + MLA_V
    for h in range(MLA_HEADS):
        src[h * LANES:h * LANES + MLA_NOPE] = h * hd + np.arange(MLA_NOPE)
        src[MLA_HEADS * LANES + h * MLA_V:MLA_HEADS * LANES + (h + 1) * MLA_V] = h * hd + MLA_NOPE + np.arange(MLA_V)
    return src


def _rope_tables(n_ctx, s_len):
    rows = s_len // GRID_W

    def axial(rot_dim):
        n_freq = rot_dim // 4
        inv_freq = ROPE_THETA ** (-jnp.arange(n_freq, dtype=F32) / n_freq)
        row_pos = jnp.repeat(jnp.arange(rows, dtype=F32), GRID_W)
        col_pos = jnp.tile(jnp.arange(GRID_W, dtype=F32), rows)
        ang = jnp.concatenate([row_pos[:, None] * inv_freq, col_pos[:, None] * inv_freq], axis=-1)
        return jnp.cos(ang), jnp.sin(ang)

    def expand(cos, sin, lane_rot):
        half = cos.shape[1]
        lane_rot = np.asarray(lane_rot)
        idx = np.maximum(lane_rot, 0) % half
        is_rot = lane_rot >= 0
        lo = is_rot & (lane_rot < half)
        hi = is_rot & (lane_rot >= half)
        c = jnp.where(jnp.asarray(is_rot), cos[:, idx], 1.0)
        s_lo = jnp.where(jnp.asarray(lo), -sin[:, idx], 0.0)
        s_hi = jnp.where(jnp.asarray(hi), sin[:, idx], 0.0)
        ident = [jnp.ones((n_ctx, LANES), F32), jnp.zeros((n_ctx, LANES), F32), jnp.zeros((n_ctx, LANES), F32)]
        return [jnp.concatenate([i, t], axis=0) for i, t in zip(ident, (c, s_lo, s_hi))]

    cos_r, sin_r = axial(MLA_ROPE)
    cos_w, sin_w = axial(SWA_DIM)
    lanes = np.arange(LANES)
    mla_rot = np.where((lanes >= MLA_NOPE) & (lanes < MLA_NOPE + MLA_ROPE), lanes - MLA_NOPE, -1)
    return (expand(cos_r, sin_r, mla_rot) + expand(cos_r, sin_r, lanes % DIFF_QK)
            + expand(cos_w, sin_w, lanes % SWA_DIM))


def _rms(x):
    return x * lax.rsqrt(jnp.mean(x * x, axis=-1, keepdims=True) + EPS)


def _modulate(x, g, shift, scale):
    return (_rms(x) * g) * (1.0 + scale) + shift


def _rope(v, c, s_lo, s_hi, half):
    return v * c + pltpu.roll(v, half, 1) * s_hi + pltpu.roll(v, LANES - half, 1) * s_lo


def _qk(q, k):
    return lax.dot_general(q, k, (((1,), (1,)), ((), ())), preferred_element_type=F32)


def _lane_iota(shape):
    return lax.broadcasted_iota(I32, shape, len(shape) - 1)


def _masked_rows(q, segs, width):
    lane = _lane_iota(q.shape)
    zero = jnp.zeros_like(q)
    return jnp.concatenate([jnp.where((lane >= sg * width) & (lane < (sg + 1) * width), q, zero) for sg in segs], axis=0)


def _ada_kernel(c_ref, w_ref, b_ref, o_ref):
    c = c_ref[...]
    a = c * (1.0 / (1.0 + jnp.exp(-c)))
    o_ref[0] = jnp.dot(a, w_ref[0], precision=HIGHEST, preferred_element_type=F32) + b_ref[0]


def _ada(cc, w_ada, b_ada):
    n_layers, d, n_out = w_ada.shape
    rows = cc.shape[0]
    tn = d
    return pl.pallas_call(
        _ada_kernel,
        grid=(n_layers, n_out // tn),
        in_specs=[pl.BlockSpec((rows, d), lambda l, n: (0, 0)),
                  pl.BlockSpec((1, d, tn), lambda l, n: (l, 0, n)),
                  pl.BlockSpec((1, 1, tn), lambda l, n: (l, 0, n))],
        out_specs=pl.BlockSpec((1, rows, tn), lambda l, n: (l, 0, n)),
        out_shape=jax.ShapeDtypeStruct((n_layers, rows, n_out), F32),
        compiler_params=_params("arbitrary", "arbitrary"),
        name="ada",
    )(cc, w_ada, b_ada.reshape(n_layers, 1, n_out))


def _moe_combine(step, n_steps, slot_ref, next_slot_ref, y_ref, rw_ref, ybuf, sem):
    nch = ybuf.shape[1] // (TOP_K * TM)
    cur = step % 2

    def start(slots, buf):
        def body(i, carry):
            for u in range(GATHER_UNROLL):
                r = i * GATHER_UNROLL + u
                for k in range(TOP_K):
                    src = pl.multiple_of(slots[0, 0, k * TM + r] * nch, nch)
                    dst = pl.multiple_of((k * TM + r) * nch, nch)
                    pltpu.make_async_copy(y_ref.at[pl.ds(src, nch)], ybuf.at[buf, pl.ds(dst, nch)], sem.at[buf]).start()
            return carry

        lax.fori_loop(0, TM // GATHER_UNROLL, body, 0)

    @pl.when(step == 0)
    def _():
        start(slot_ref, 0)

    @pl.when(step + 1 < n_steps)
    def _():
        start(next_slot_ref, 1 - cur)

    pltpu.make_async_copy(y_ref.at[pl.ds(0, TOP_K * TM * nch)], ybuf.at[cur], sem.at[cur]).wait()
    rw = rw_ref[...]
    acc = None
    for k in range(TOP_K):
        yk = jnp.concatenate([ybuf[cur, pl.ds(k * TM * nch + c, TM, stride=nch), :] for c in range(nch)], axis=-1)
        term = rw[:, k:k + 1] * yk
        acc = term if acc is None else acc + term
    return acc


def _pre_attn_kernel(has_moe, *refs):
    if has_moe:
        slot_ref, next_slot_ref, y_ref, rw_ref, pmod_ref = refs[:5]
        x2_ref, ybuf, sem = refs[-3:]
        refs = refs[5:-3]
    (x_ref, mod_ref, g_ref, win_ref, qn_ref, wuq_ref, kvn_ref, wukv_ref,
     mc_ref, ml_ref, mh_ref, dc_ref, dl_ref, dh_ref, wc_ref, wl_ref, wh_ref,
     mq_ref, mk_ref, mv_ref, dq_ref, dk_ref, dv_ref, sq_ref, sk_ref, sv_ref) = refs
    x = x_ref[0]
    if has_moe:
        step = pl.program_id(0) * pl.num_programs(1) + pl.program_id(1)
        n_steps = pl.num_programs(0) * pl.num_programs(1)
        x = x + pmod_ref[0, 0][5:6] * _moe_combine(step, n_steps, slot_ref, next_slot_ref, y_ref, rw_ref, ybuf, sem)
        x2_ref[0] = x
    mod = mod_ref[0, 0]
    h = _modulate(x, g_ref[0], mod[0:1], mod[1:2])
    p = jnp.dot(h.astype(BF16), win_ref[0], preferred_element_type=F32)

    def grp(g, n=1):
        return p[:, g * LANES:(g + n) * LANES]

    mla_scale = LOG2E * (MLA_NOPE + MLA_ROPE) ** -0.5
    diff_scale = LOG2E * DIFF_QK ** -0.5
    swa_scale = LOG2E * SWA_DIM ** -0.5
    mla_tab = (mc_ref[...], ml_ref[...], mh_ref[...])
    diff_tab = (dc_ref[...], dl_ref[...], dh_ref[...])
    swa_tab = (wc_ref[...], wl_ref[...], wh_ref[...])

    cq = (_rms(grp(G_CQ, 2)) * qn_ref[0]).astype(BF16)
    q = jnp.dot(cq, wuq_ref[0], preferred_element_type=F32)
    ckv = (_rms(grp(G_CKV)) * kvn_ref[0]).astype(BF16)
    kv = jnp.dot(ckv, wukv_ref[0], preferred_element_type=F32)
    k_rope = _rope(grp(G_KROPE), *mla_tab, MLA_ROPE // 2)
    for hd in range(MLA_HEADS):
        sl = slice(hd * LANES, (hd + 1) * LANES)
        mq_ref[0, :, sl] = (_rope(q[:, sl], *mla_tab, MLA_ROPE // 2) * mla_scale).astype(BF16)
        mk_ref[0, :, sl] = (kv[:, sl] + k_rope).astype(BF16)
    mv_ref[0] = kv[:, MLA_HEADS * LANES:].astype(BF16)

    for g in range(2):
        sl = slice(g * LANES, (g + 1) * LANES)
        dq_ref[0, :, sl] = (_rope(grp(G_DQ + g), *diff_tab, DIFF_QK // 2) * diff_scale).astype(BF16)
        dk_ref[0, :, sl] = _rope(grp(G_DK + g), *diff_tab, DIFF_QK // 2).astype(BF16)
    dv_ref[0] = grp(G_DV, 2).astype(BF16)

    for g in range(4):
        sl = slice(g * LANES, (g + 1) * LANES)
        sq_ref[0, :, sl] = (_rope(grp(G_SQ + g), *swa_tab, SWA_DIM // 2) * swa_scale).astype(BF16)
    for g in range(2):
        sl = slice(g * LANES, (g + 1) * LANES)
        sk_ref[0, :, sl] = _rope(grp(G_SK + g), *swa_tab, SWA_DIM // 2).astype(BF16)
    sv_ref[0] = grp(G_SV, 2).astype(BF16)


def _slot_tiles(slots):
    t = slots.shape[0]
    return slots.reshape(t // TM, TM, TOP_K).transpose(0, 2, 1).reshape(t // TM, 1, TOP_K * TM)


def _pre_attn(layer, xs, mods, g_attn, w_in_p, q_norm, w_uq_p, kv_norm, w_ukv_p, tables, n_ctx, moe=None):
    b, nt, d = xs.shape
    tiles = nt // TM
    n_ctx_tiles = n_ctx // TM
    ctx_row = b

    def tok(w):
        return pl.BlockSpec((1, TM, w), lambda j, bi: (bi, j, 0))

    def lay(shape):
        return pl.BlockSpec((1,) + shape, lambda j, bi: (layer,) + (0,) * len(shape))

    def mod_spec(lyr):
        return pl.BlockSpec((1, 1, N_MOD, d), lambda j, bi: (lyr, jnp.where(j < n_ctx_tiles, ctx_row, bi), 0, 0))

    tab = pl.BlockSpec((TM, LANES), lambda j, bi: (j, 0))
    widths = (512, 512, 256, 256, 256, 256, 512, 256, 256)
    in_specs = [tok(d), mod_spec(layer), lay((1, d)), lay(w_in_p.shape[1:]), lay((1, MLA_Q_RANK)),
                lay(w_uq_p.shape[1:]), lay((1, MLA_KV_RANK)), lay(w_ukv_p.shape[1:])] + [tab] * 9
    out_specs = [tok(w) for w in widths]
    out_shape = [jax.ShapeDtypeStruct((b, nt, w), BF16) for w in widths]
    args = [xs, mods, g_attn, w_in_p, q_norm, w_uq_p, kv_norm, w_ukv_p, *tables]
    scratch = []
    if moe is not None:
        y_rows, slots, rw = moe
        nch = d // LANES

        def next_tile(j, bi):
            wrap = bi + 1 == b
            nj = jnp.minimum(jnp.where(wrap, j + 1, j), tiles - 1)
            return (jnp.where(wrap, 0, bi + 1) * tiles + nj, 0, 0)

        slot_block = (1, 1, TOP_K * TM)
        in_specs = [pl.BlockSpec(slot_block, lambda j, bi: (bi * tiles + j, 0, 0), memory_space=pltpu.SMEM),
                    pl.BlockSpec(slot_block, next_tile, memory_space=pltpu.SMEM),
                    pl.BlockSpec(memory_space=pl.ANY),
                    pl.BlockSpec((TM, LANES), lambda j, bi: (bi * tiles + j, 0)),
                    mod_spec(layer - 1)] + in_specs
        slot_tiles = _slot_tiles(slots)
        args = [slot_tiles, slot_tiles, y_rows, rw, mods] + args
        out_specs = out_specs + [tok(d)]
        out_shape = out_shape + [jax.ShapeDtypeStruct((b, nt, d), F32)]
        scratch = [pltpu.VMEM((2, TOP_K * TM * nch, LANES), F32), pltpu.SemaphoreType.DMA((2,))]
    outs = pl.pallas_call(
        functools.partial(_pre_attn_kernel, moe is not None),
        grid=(tiles, b),
        in_specs=in_specs,
        out_specs=out_specs,
        out_shape=out_shape,
        scratch_shapes=scratch,
        compiler_params=_params("arbitrary", "arbitrary"),
        name="pre_attn",
    )(*args)
    return (outs[-1], outs[:-1]) if moe is not None else (xs, outs)


def _softmax_pv(s, v):
    p = jnp.exp2(s - jnp.max(s, axis=-1, keepdims=True))
    l = jnp.sum(p, axis=-1, keepdims=True)
    return jnp.dot(p.astype(BF16), v, preferred_element_type=F32) / l


def _dense_steps(n_ctx, nt, run):
    j = pl.program_id(1)

    @pl.when(j == 0)
    def _():
        run(0, n_ctx, n_ctx)

    @pl.when(j > 0)
    def _():
        run(pl.multiple_of(n_ctx + (j - 1) * TQ, TM), TQ, nt)


def _mla_attn_kernel(n_ctx, q_ref, k_ref, v_ref, o_ref):
    def run(row0, rows, nk):
        lane = _lane_iota((rows, LANES))
        outs = []
        for hd in range(MLA_HEADS):
            sl = slice(hd * LANES, (hd + 1) * LANES)
            vs = slice((hd // 2) * LANES, (hd // 2 + 1) * LANES)
            s = _qk(q_ref[0, pl.ds(row0, rows), sl], k_ref[0, :nk, sl])
            outs.append(_softmax_pv(s, v_ref[0, :nk, vs]))
        for g in range(2):
            o_ref[0, pl.ds(row0, rows), g * LANES:(g + 1) * LANES] = jnp.where(
                lane < MLA_V, outs[2 * g], outs[2 * g + 1]).astype(BF16)

    _dense_steps(n_ctx, k_ref.shape[1], run)


def _diff_attn_kernel(n_ctx, lambda_init, q_ref, k_ref, v_ref, lam_ref, g_ref, seg_ref, o_ref):
    lam = lam_ref[0]
    lam_full = (jnp.exp(jnp.sum(lam[0:1] * lam[1:2], axis=-1, keepdims=True))
                - jnp.exp(jnp.sum(lam[2:3] * lam[3:4], axis=-1, keepdims=True)) + lambda_init)

    def run(row0, rows, nk):
        lane = _lane_iota((rows, LANES))
        heads = []
        for hd in range(DIFF_HEADS):
            sl = slice((hd // 2) * LANES, (hd // 2 + 1) * LANES)
            q = q_ref[0, pl.ds(row0, rows), sl]
            k = k_ref[0, :nk, sl]
            v = v_ref[0, :nk, sl]
            a = []
            for comp in range(2):
                seg = (hd % 2) * 2 + comp
                qm = jnp.where((lane >= seg * DIFF_QK) & (lane < (seg + 1) * DIFF_QK), q, jnp.zeros_like(q))
                a.append(_softmax_pv(_qk(qm, k), v))
            heads.append(a[0] - lam_full * a[1])
        for g in range(2):
            o = jnp.where(lane < DIFF_V, heads[2 * g], heads[2 * g + 1])
            ms = jnp.dot(o * o, seg_ref[...], precision=HIGHEST, preferred_element_type=F32) * (1.0 / DIFF_V)
            o = o * lax.rsqrt(ms + EPS) * g_ref[...] * (1.0 - lambda_init)
            o_ref[0, pl.ds(row0, rows), g * LANES:(g + 1) * LANES] = o.astype(BF16)

    _dense_steps(n_ctx, k_ref.shape[1], run)


def _swa_attn_kernel(n_ctx, q_ref, k_ref, v_ref, sink_ref, o_ref):
    j = pl.program_id(1)
    nt = k_ref.shape[1]
    band = TM + 2 * WINDOW
    lane = _lane_iota((TM, LANES))
    group_heads = SWA_HEADS // SWA_KV_HEADS

    def run(kv, k, v, allowed):
        qs = jnp.concatenate([_masked_rows(q_ref[0, :, g * LANES:(g + 1) * LANES], range(2), SWA_DIM)
                              for g in (2 * kv, 2 * kv + 1)], axis=0)
        s = _qk(qs, k)
        if allowed is not None:
            s = jnp.where(jnp.concatenate([allowed] * group_heads, axis=0), s, NEG_INF)
        sink = jnp.concatenate([jnp.broadcast_to(sink_ref[0, hd:hd + 1, 0:1] * LOG2E, (TM, 1))
                                for hd in range(kv * group_heads, (kv + 1) * group_heads)], axis=0)
        m = jnp.maximum(jnp.max(s, axis=-1, keepdims=True), sink)
        p = jnp.exp2(s - m)
        l = jnp.sum(p, axis=-1, keepdims=True) + jnp.exp2(sink - m)
        o = jnp.dot(p.astype(BF16), v, preferred_element_type=F32) / l
        for i in range(2):
            g = 2 * kv + i
            o_ref[0, :, g * LANES:(g + 1) * LANES] = jnp.where(
                lane < SWA_DIM, o[2 * i * TM:(2 * i + 1) * TM], o[(2 * i + 1) * TM:(2 * i + 2) * TM]).astype(BF16)

    @pl.when(j < n_ctx // TM)
    def _():
        for kv in range(SWA_KV_HEADS):
            ks = slice(kv * LANES, (kv + 1) * LANES)
            run(kv, k_ref[0, :n_ctx, ks], v_ref[0, :n_ctx, ks], None)

    @pl.when(j >= n_ctx // TM)
    def _():
        q0 = j * TM
        w0 = pl.multiple_of(jnp.clip(q0 - WINDOW, n_ctx, nt - band), WINDOW)
        q_pos = q0 + lax.broadcasted_iota(I32, (TM, n_ctx + band), 0)
        col = lax.broadcasted_iota(I32, (TM, n_ctx + band), 1)
        k_pos = w0 + col - n_ctx
        allowed = (col < n_ctx) | (jnp.abs(k_pos - q_pos) <= WINDOW)
        for kv in range(SWA_KV_HEADS):
            ks = slice(kv * LANES, (kv + 1) * LANES)
            k = jnp.concatenate([k_ref[0, :n_ctx, ks], k_ref[0, pl.ds(w0, band), ks]], axis=0)
            v = jnp.concatenate([v_ref[0, :n_ctx, ks], v_ref[0, pl.ds(w0, band), ks]], axis=0)
            run(kv, k, v, allowed)


def _attention(kernel, q, k, v, extra, out_width, name, dense_steps=None):
    b, nt, _ = q.shape

    def whole(width):
        return pl.BlockSpec((1, nt, width), lambda bi, j: (bi, 0, 0))

    def tile(width):
        return pl.BlockSpec((1, TM, width), lambda bi, j: (bi, j, 0))

    q_spec, o_spec, steps = (tile, tile, nt // TM) if dense_steps is None else (whole, whole, dense_steps)
    extra_specs = [pl.BlockSpec(e.shape, lambda bi, j, nd=e.ndim: (0,) * nd) for e in extra]
    return pl.pallas_call(
        kernel,
        grid=(b, steps),
        in_specs=[q_spec(q.shape[2]), whole(k.shape[2]), whole(v.shape[2])] + extra_specs,
        out_specs=o_spec(out_width),
        out_shape=jax.ShapeDtypeStruct((b, nt, out_width), BF16),
        compiler_params=_params("arbitrary", "arbitrary"),
        name=name,
    )(q, k, v, *extra)


def _post_attn_kernel(n_exp, x_ref, mod_ref, mla_ref, diff_ref, swa_ref, wout_ref, g_ref, wr_ref, br_ref,
                      x1_ref, f_ref, route_ref, rw_ref, cnt_ref, carry_ref):
    first = (pl.program_id(0) == 0) & (pl.program_id(1) == 0)

    @pl.when(first)
    def _():
        carry_ref[...] = jnp.zeros_like(carry_ref)

    x = x_ref[0]
    mod = mod_ref[0, 0]
    a = jnp.concatenate([mla_ref[0], diff_ref[0], swa_ref[0]], axis=-1)
    x1 = x + mod[2:3] * jnp.dot(a, wout_ref[0], preferred_element_type=F32)
    x1_ref[0] = x1
    f = _modulate(x1, g_ref[0], mod[3:4], mod[4:5])
    nch = f.shape[1] // LANES
    for c in range(nch):
        f_ref[pl.ds(c, TM, stride=nch), :] = f[:, c * LANES:(c + 1) * LANES]

    logits = jnp.dot(f.astype(BF16), wr_ref[0], preferred_element_type=F32) + br_ref[0]
    lane = _lane_iota((TM, LANES))
    vals, hots = [], []
    for _ in range(TOP_K):
        m = jnp.max(logits, axis=-1, keepdims=True)
        idx = jnp.min(jnp.where(logits == m, lane, LANES), axis=-1, keepdims=True)
        hot = lane == idx
        logits = jnp.where(hot, -3e38, logits)
        vals.append(m)
        hots.append((idx, hot))
    es = [jnp.exp(v - vals[0]) for v in vals]
    denom = functools.reduce(jnp.add, es)
    sel = functools.reduce(jnp.add, [jnp.where(hot, 1.0, 0.0) for _, hot in hots])

    r_io = lax.broadcasted_iota(I32, (TM, TM), 0)
    c_io = lax.broadcasted_iota(I32, (TM, TM), 1)
    tril = jnp.where(c_io < r_io, 1.0, 0.0).astype(BF16)
    rank = carry_ref[0:1, :] + jnp.dot(tril, sel.astype(BF16), preferred_element_type=F32)
    carry_ref[0:1, :] = carry_ref[0:1, :] + jnp.sum(sel, axis=0, keepdims=True)
    cnt_ref[...] = jnp.broadcast_to(carry_ref[0:1, :], cnt_ref.shape)

    route = jnp.zeros((TM, LANES), I32)
    rw = jnp.zeros((TM, LANES), F32)
    for k, (idx, hot) in enumerate(hots):
        rk = jnp.sum(jnp.where(hot, rank, 0.0), axis=-1, keepdims=True).astype(I32)
        route = jnp.where(lane == k, idx, route)
        route = jnp.where(lane == TOP_K + k, rk, route)
        rw = jnp.where(lane == k, es[k] / denom, rw)
    route_ref[...] = route
    rw_ref[...] = rw


def _post_attn(layer, xs, mods, mla_o, diff_o, swa_o, w_out_b, g_ffn, w_router_p, b_router_p, n_ctx, n_exp):
    b, nt, d = xs.shape
    t = b * nt
    nch = d // LANES
    n_ctx_tiles = n_ctx // TM
    tiles = nt // TM

    def tok(w):
        return pl.BlockSpec((1, TM, w), lambda bi, j: (bi, j, 0))

    def lay(shape):
        return pl.BlockSpec((1,) + shape, lambda bi, j: (layer,) + (0,) * len(shape))

    def flat(rows, w):
        return pl.BlockSpec((rows, w), lambda bi, j: (bi * tiles + j, 0))

    mod_spec = pl.BlockSpec((1, 1, N_MOD, d), lambda bi, j: (layer, jnp.where(j < n_ctx_tiles, b, bi), 0, 0))
    return pl.pallas_call(
        functools.partial(_post_attn_kernel, n_exp),
        grid=(b, tiles),
        in_specs=[tok(d), mod_spec, tok(256), tok(256), tok(512), lay(w_out_b.shape[1:]), lay((1, d)),
                  lay((d, LANES)), lay((1, LANES))],
        out_specs=[tok(d), flat(TM * nch, LANES), flat(TM, LANES), flat(TM, LANES),
                   pl.BlockSpec((SUBLANES, LANES), lambda bi, j: (0, 0))],
        out_shape=[jax.ShapeDtypeStruct((b, nt, d), F32), jax.ShapeDtypeStruct((t * nch, LANES), F32),
                   jax.ShapeDtypeStruct((t, LANES), I32), jax.ShapeDtypeStruct((t, LANES), F32),
                   jax.ShapeDtypeStruct((SUBLANES, LANES), F32)],
        scratch_shapes=[pltpu.VMEM((SUBLANES, LANES), F32)],
        compiler_params=_params("arbitrary", "arbitrary"),
        name="post_attn",
    )(xs, mods, mla_o, diff_o, swa_o, w_out_b, g_ffn, w_router_p, b_router_p)


def _dispatch_kernel(n_exp, nch, n_tiles, zstart_ref, zflag_ref, nu_ref, slot_ref, f_ref, xs_ref, zbuf, zsem, sem):
    @pl.when(pl.program_id(0) == 0)
    def _():
        zbuf[...] = jnp.zeros_like(zbuf)

        def zero_tile(row_start):
            dst = xs_ref.at[pl.ds(pl.multiple_of(row_start * nch, SUBLANES), TMOE * nch)]
            cp = pltpu.make_async_copy(zbuf, dst, zsem)
            cp.start()
            cp.wait()

        for e in range(n_exp):
            @pl.when(zflag_ref[e] > 0)
            def _():
                zero_tile(zstart_ref[e])

        def tail(tile, carry):
            zero_tile(tile * TMOE)
            return carry

        lax.fori_loop(nu_ref[0], n_tiles, tail, 0)

    def body(r, carry):
        src = f_ref.at[pl.ds(pl.multiple_of(r * nch, nch), nch)]
        for k in range(TOP_K):
            dst = pl.multiple_of(slot_ref[0, 0, k * TD + r] * nch, nch)
            pltpu.make_async_copy(src, xs_ref.at[pl.ds(dst, nch)], sem).start()
        return carry

    lax.fori_loop(0, TD, body, 0)
    for _ in range(TOP_K):
        pltpu.make_async_copy(f_ref, xs_ref.at[pl.ds(0, TD * nch)], sem).wait()


def _dispatch(f_rows, slots, zstart, zflag, n_used, n_slots, n_exp):
    nch_t = f_rows.shape[0]
    t = slots.shape[0]
    nch = nch_t // t
    slots_km = slots.reshape(t // TD, TD, TOP_K).transpose(0, 2, 1).reshape(t // TD, 1, TOP_K * TD)
    return pl.pallas_call(
        functools.partial(_dispatch_kernel, n_exp, nch, n_slots // TMOE),
        grid_spec=pltpu.PrefetchScalarGridSpec(
            num_scalar_prefetch=3,
            grid=(t // TD,),
            in_specs=[pl.BlockSpec((1, 1, TOP_K * TD), lambda i, zs, zf, nu: (i, 0, 0), memory_space=pltpu.SMEM),
                      pl.BlockSpec((TD * nch, LANES), lambda i, zs, zf, nu: (i, 0))],
            out_specs=pl.BlockSpec(memory_space=pl.ANY),
            scratch_shapes=[pltpu.VMEM((TMOE * nch, LANES), F32), pltpu.SemaphoreType.DMA, pltpu.SemaphoreType.DMA]),
        out_shape=jax.ShapeDtypeStruct((n_slots * nch, LANES), F32),
        compiler_params=_params("arbitrary"),
        name="dispatch",
    )(zstart, zflag, n_used, slots_km, f_rows)


def _moe_kernel(nch, te_ref, nu_ref, x_ref, wg_ref, bg_ref, wu_ref, bu_ref, wd_ref, bd_ref, y_ref,
                wg_b, wu_b, wd_b):
    i = pl.program_id(0)

    @pl.when((i == 0) | (te_ref[i] != te_ref[jnp.maximum(i - 1, 0)]))
    def _():
        wg_b[...] = wg_ref[0].astype(BF16)
        wu_b[...] = wu_ref[0].astype(BF16)
        wd_b[...] = wd_ref[0].astype(BF16)

    @pl.when(i < nu_ref[0])
    def _():
        x = jnp.concatenate([x_ref[pl.ds(c, TMOE, stride=nch), :] for c in range(nch)], axis=-1).astype(BF16)
        gate = jnp.dot(x, wg_b[...], preferred_element_type=F32) + bg_ref[0]
        up = jnp.dot(x, wu_b[...], preferred_element_type=F32) + bu_ref[0]
        gate = jnp.minimum(gate, SWIGLU_LIMIT)
        up = jnp.clip(up, -SWIGLU_LIMIT, SWIGLU_LIMIT)
        act = gate * (1.0 / (1.0 + jnp.exp(-SWIGLU_ALPHA * gate))) * (up + 1.0)
        y = jnp.dot(act.astype(BF16), wd_b[...], preferred_element_type=F32) + bd_ref[0]
        for c in range(nch):
            y_ref[pl.ds(c, TMOE, stride=nch), :] = y[:, c * LANES:(c + 1) * LANES]

    @pl.when(pl.program_id(0) >= nu_ref[0])
    def _():
        y_ref[...] = jnp.zeros_like(y_ref)


def _moe(layer, xs_rows, tile_e, n_used, wg, bg, wu, bu, wd, bd, n_exp):
    d, f = wg.shape[1], wg.shape[2]
    nch = d // LANES
    n_tiles = xs_rows.shape[0] // (TMOE * nch)

    def rows(i, te, nu):
        return (jnp.minimum(i, nu[0] - 1), 0)

    def exp(i, te, nu):
        return (layer * n_exp + te[i], 0, 0)

    return pl.pallas_call(
        functools.partial(_moe_kernel, nch),
        grid_spec=pltpu.PrefetchScalarGridSpec(
            num_scalar_prefetch=2,
            grid=(n_tiles,),
            in_specs=[pl.BlockSpec((TMOE * nch, LANES), rows),
                      pl.BlockSpec((1, d, f), exp), pl.BlockSpec((1, 1, f), exp),
                      pl.BlockSpec((1, d, f), exp), pl.BlockSpec((1, 1, f), exp),
                      pl.BlockSpec((1, f, d), exp), pl.BlockSpec((1, 1, d), exp)],
            out_specs=pl.BlockSpec((TMOE * nch, LANES), lambda i, te, nu: (i, 0)),
            scratch_shapes=[pltpu.VMEM((d, f), BF16), pltpu.VMEM((d, f), BF16), pltpu.VMEM((f, d), BF16)]),
        out_shape=jax.ShapeDtypeStruct(xs_rows.shape, F32),
        compiler_params=_params("arbitrary"),
        name="moe",
    )(tile_e, n_used, xs_rows, wg, bg, wu, bu, wd, bd)


def _final_kernel(slot_ref, next_slot_ref, y_ref, rw_ref, mod_ref, x1_ref, g_ref, o_ref, ybuf, sem):
    step = pl.program_id(0) * pl.num_programs(1) + pl.program_id(1)
    n_steps = pl.num_programs(0) * pl.num_programs(1)
    x = x1_ref[0] + mod_ref[0, 0][5:6] * _moe_combine(step, n_steps, slot_ref, next_slot_ref, y_ref, rw_ref, ybuf, sem)
    o_ref[0] = _rms(x) * g_ref[...]


def _final_norm(layer, x1, y_rows, slots, rw, mods, g_final, n_ctx, s_len):
    b, nt, d = x1.shape
    nch = d // LANES
    tiles = nt // TM
    lat_tiles = s_len // TM
    off = n_ctx // TM

    def next_tile(bi, j):
        wrap = j + 1 == lat_tiles
        nb = jnp.minimum(jnp.where(wrap, bi + 1, bi), b - 1)
        return (nb * tiles + off + jnp.where(wrap, 0, j + 1), 0, 0)

    slot_block = (1, 1, TOP_K * TM)
    slot_tiles = _slot_tiles(slots)
    return pl.pallas_call(
        _final_kernel,
        grid=(b, lat_tiles),
        in_specs=[pl.BlockSpec(slot_block, lambda bi, j: (bi * tiles + off + j, 0, 0), memory_space=pltpu.SMEM),
                  pl.BlockSpec(slot_block, next_tile, memory_space=pltpu.SMEM),
                  pl.BlockSpec(memory_space=pl.ANY),
                  pl.BlockSpec((TM, LANES), lambda bi, j: (bi * tiles + off + j, 0)),
                  pl.BlockSpec((1, 1, N_MOD, d), lambda bi, j: (layer, bi, 0, 0)),
                  pl.BlockSpec((1, TM, d), lambda bi, j: (bi, j + off, 0)),
                  pl.BlockSpec((1, d), lambda bi, j: (0, 0))],
        out_specs=pl.BlockSpec((1, TM, d), lambda bi, j: (bi, j, 0)),
        out_shape=jax.ShapeDtypeStruct((b, s_len, d), F32),
        scratch_shapes=[pltpu.VMEM((2, TOP_K * TM * nch, LANES), F32), pltpu.SemaphoreType.DMA((2,))],
        compiler_params=_params("arbitrary", "arbitrary"),
        name="final_norm",
    )(slot_tiles, slot_tiles, y_rows, rw, mods, x1, g_final.reshape(1, d))


def _routing_plan(route, counts, n_exp, n_tiles):
    idx = route[:, 0:TOP_K]
    rank = route[:, TOP_K:2 * TOP_K]
    counts = counts[0, :n_exp].astype(I32)
    padded = ((counts + TMOE - 1) // TMOE) * TMOE
    ends = jnp.cumsum(padded)
    starts = ends - padded
    onehot = idx[..., None] == jnp.arange(n_exp, dtype=I32)
    slots = jnp.sum(jnp.where(onehot, starts, 0), axis=-1) + rank
    n_used = (ends[-1] // TMOE).reshape(1)
    tile_ids = jnp.minimum(jnp.arange(n_tiles, dtype=I32), n_used[0] - 1)
    tile_e = jnp.sum((tile_ids[:, None] >= (ends // TMOE)[None, :]).astype(I32), axis=-1)
    tile_e = jnp.minimum(tile_e, n_exp - 1)
    zstart = jnp.maximum(ends - TMOE, 0)
    zflag = (padded > 0).astype(I32)
    return slots, tile_e, n_used, zstart, zflag


def kernel(x, c, ctx, c_ctx, w_ada, b_ada, g_attn, w_in, mla_q_norm, mla_w_uq, mla_kv_norm, mla_w_ukv,
           diff_lambda, diff_subln, swa_sink, w_out, g_ffn, w_router, b_router, w_gate, b_gate, w_up, b_up,
           w_down, b_down, g_final):
    b, s_len, d = x.shape
    n_ctx = ctx.shape[1]
    n_layers = w_ada.shape[0]
    n_exp = w_router.shape[2]
    nt = n_ctx + s_len
    t = b * nt
    assert d % LANES == 0 and n_ctx % TM == 0 and s_len % TQ == 0 and t % TD == 0 and s_len % GRID_W == 0
    assert n_exp <= LANES and s_len >= TM + 2 * WINDOW

    w_in_p = _gather_columns(w_in, _in_proj_columns()).astype(BF16)
    w_uq_p = _gather_columns(mla_w_uq, _uq_columns()).astype(BF16)
    w_ukv_p = _gather_columns(mla_w_ukv, _ukv_columns()).astype(BF16)
    w_out_b = w_out.astype(BF16)
    w_router_p = jnp.pad(w_router, ((0, 0), (0, 0), (0, LANES - n_exp))).astype(BF16)
    b_router_p = jnp.pad(b_router, ((0, 0), (0, LANES - n_exp)), constant_values=NEG_INF).reshape(n_layers, 1, LANES)
    wg = w_gate.reshape(n_layers * n_exp, d, -1)
    wu = w_up.reshape(n_layers * n_exp, d, -1)
    wd = w_down.reshape(n_layers * n_exp, -1, d)
    bg = b_gate.reshape(n_layers * n_exp, 1, -1)
    bu = b_up.reshape(n_layers * n_exp, 1, -1)
    bd = b_down.reshape(n_layers * n_exp, 1, d)
    lam_p = jnp.pad(diff_lambda, ((0, 0), (0, 0), (0, LANES - DIFF_QK)))
    subln_p = jnp.tile(diff_subln, (1, LANES // DIFF_V)).reshape(n_layers, 1, LANES)
    sink_p = jnp.broadcast_to(swa_sink[:, :, None], (n_layers, SWA_HEADS, LANES))
    seg = jnp.asarray((np.arange(LANES)[:, None] // DIFF_V == np.arange(LANES)[None, :] // DIFF_V), F32)
    tables = _rope_tables(n_ctx, s_len)

    ada_rows = -(-(b + 1) // SUBLANES) * SUBLANES
    cc = jnp.concatenate([c, c_ctx[None, :], jnp.zeros((ada_rows - b - 1, d), F32)], axis=0)
    mods = _ada(cc, w_ada, b_ada).reshape(n_layers, ada_rows, N_MOD, d)

    n_slots = t * TOP_K + n_exp * TMOE
    n_tiles = n_slots // TMOE
    xs = jnp.concatenate([ctx, x], axis=1)
    moe = None
    for layer in range(n_layers):
        lambda_init = 0.8 - 0.6 * math.exp(-0.3 * layer)
        xs, (mq, mk, mv, dq, dk, dv, sq, sk, sv) = _pre_attn(
            layer, xs, mods, g_attn.reshape(n_layers, 1, d), w_in_p, mla_q_norm.reshape(n_layers, 1, -1), w_uq_p,
            mla_kv_norm.reshape(n_layers, 1, -1), w_ukv_p, tables, n_ctx, moe)
        dense_steps = 1 + s_len // TQ
        mla_o = _attention(functools.partial(_mla_attn_kernel, n_ctx), mq, mk, mv, [], 256, "mla_attn", dense_steps)
        diff_o = _attention(functools.partial(_diff_attn_kernel, n_ctx, lambda_init), dq, dk, dv,
                            [lam_p[layer:layer + 1], subln_p[layer], seg], 256, "diff_attn", dense_steps)
        swa_o = _attention(functools.partial(_swa_attn_kernel, n_ctx), sq, sk, sv, [sink_p[layer:layer + 1]],
                           512, "swa_attn")
        xs, f_rows, route, rw, counts = _post_attn(
            layer, xs, mods, mla_o, diff_o, swa_o, w_out_b, g_ffn.reshape(n_layers, 1, d), w_router_p, b_router_p,
            n_ctx, n_exp)
        slots, tile_e, n_used, zstart, zflag = _routing_plan(route, counts, n_exp, n_tiles)
        xs_rows = _dispatch(f_rows, slots, zstart, zflag, n_used, n_slots, n_exp)
        moe = (_moe(layer, xs_rows, tile_e, n_used, wg, bg, wu, bu, wd, bd, n_exp), slots, rw)
    return _final_norm(n_layers - 1, xs, *moe, mods, g_final, n_ctx, s_len)
```

```python
import functools
import math

import jax
import jax.numpy as jnp
import numpy as np
from jax import lax
from jax.experimental import pallas as pl
from jax.experimental.pallas import tpu as pltpu

F32 = jnp.float32
BF16 = jnp.bfloat16
I32 = jnp.int32
HIGHEST = lax.Precision.HIGHEST
LOG2E = math.log2(math.e)

LANES = 128
SUBLANES = 8
VMEM_LIMIT = 56 * 1024 * 1024

GRID_W = 64
ROPE_THETA = 10000.0
EPS = 1e-6
NEG_INF = -1e30
N_MOD = 6

MLA_HEADS, MLA_Q_RANK, MLA_KV_RANK, MLA_NOPE, MLA_ROPE, MLA_V = 4, 256, 128, 64, 32, 64
DIFF_HEADS, DIFF_QK = 4, 32
DIFF_V = 2 * DIFF_QK
SWA_HEADS, SWA_KV_HEADS, SWA_DIM, WINDOW = 8, 2, 64, 128
TOP_K = 4
SWIGLU_LIMIT = 7.0
SWIGLU_ALPHA = 1.702

TM = 256
TQ = 512
TMOE = 512
TD = 1024
GATHER_UNROLL = 8

G_CQ, G_CKV, G_KROPE, G_DQ, G_DK, G_DV, G_SQ, G_SK, G_SV, N_GROUPS = 0, 2, 3, 4, 6, 8, 10, 14, 16, 18


def _params(*sem):
    return pltpu.CompilerParams(dimension_semantics=sem, vmem_limit_bytes=VMEM_LIMIT)


def _in_proj_columns():
    src = -np.ones(N_GROUPS * LANES, np.int64)
    o_ckv = MLA_Q_RANK
    o_kr = o_ckv + MLA_KV_RANK
    o_dq = o_kr + MLA_ROPE
    o_dk = o_dq + DIFF_HEADS * 2 * DIFF_QK
    o_dv = o_dk + DIFF_HEADS * 2 * DIFF_QK
    o_sq = o_dv + DIFF_HEADS * DIFF_V
    o_sk = o_sq + SWA_HEADS * SWA_DIM
    o_sv = o_sk + SWA_KV_HEADS * SWA_DIM
    src[G_CQ * LANES:G_CQ * LANES + MLA_Q_RANK] = np.arange(MLA_Q_RANK)
    src[G_CKV * LANES:G_CKV * LANES + MLA_KV_RANK] = o_ckv + np.arange(MLA_KV_RANK)
    src[G_KROPE * LANES + MLA_NOPE:G_KROPE * LANES + MLA_NOPE + MLA_ROPE] = o_kr + np.arange(MLA_ROPE)
    src[G_DQ * LANES:G_DQ * LANES + 256] = o_dq + np.arange(256)
    src[G_DK * LANES:G_DK * LANES + 256] = o_dk + np.arange(256)
    src[G_DV * LANES:G_DV * LANES + 256] = o_dv + np.arange(256)
    src[G_SQ * LANES:G_SQ * LANES + 512] = o_sq + np.arange(512)
    for kv in range(SWA_KV_HEADS):
        for half in range(2):
            lo = half * SWA_DIM
            src[(G_SK + kv) * LANES + lo:(G_SK + kv) * LANES + lo + SWA_DIM] = o_sk + kv * SWA_DIM + np.arange(SWA_DIM)
            src[(G_SV + kv) * LANES + lo:(G_SV + kv) * LANES + lo + SWA_DIM] = o_sv + kv * SWA_DIM + np.arange(SWA_DIM)
    return src


def _gather_columns(w, src):
    cols = jnp.take(w, jnp.asarray(np.maximum(src, 0), I32), axis=-1)
    return jnp.where(jnp.asarray(src >= 0), cols, 0.0)


def _uq_columns():
    src = -np.ones(MLA_HEADS * LANES, np.int64)
    hd = MLA_NOPE + MLA_ROPE
    for h in range(MLA_HEADS):
        src[h * LANES:h * LANES + hd] = h * hd + np.arange(hd)
    return src


def _ukv_columns():
    src = -np.ones(MLA_HEADS * LANES + MLA_HEADS * MLA_V, np.int64)
    hd = MLA_NOPE + MLA_V
    for h in range(MLA_HEADS):
        src[h * LANES:h * LANES + MLA_NOPE] = h * hd + np.arange(MLA_NOPE)
        src[MLA_HEADS * LANES + h * MLA_V:MLA_HEADS * LANES + (h + 1) * MLA_V] = h * hd + MLA_NOPE + np.arange(MLA_V)
    return src


def _rope_tables(n_ctx, s_len):
    rows = s_len // GRID_W

    def axial(rot_dim):
        n_freq = rot_dim // 4
        inv_freq = ROPE_THETA ** (-jnp.arange(n_freq, dtype=F32) / n_freq)
        row_pos = jnp.repeat(jnp.arange(rows, dtype=F32), GRID_W)
        col_pos = jnp.tile(jnp.arange(GRID_W, dtype=F32), rows)
        ang = jnp.concatenate([row_pos[:, None] * inv_freq, col_pos[:, None] * inv_freq], axis=-1)
        return jnp.cos(ang), jnp.sin(ang)

    def expand(cos, sin, lane_rot):
        half = cos.shape[1]
        lane_rot = np.asarray(lane_rot)
        idx = np.maximum(lane_rot, 0) % half
        is_rot = lane_rot >= 0
        lo = is_rot & (lane_rot < half)
        hi = is_rot & (lane_rot >= half)
        c = jnp.where(jnp.asarray(is_rot), cos[:, idx], 1.0)
        s_lo = jnp.where(jnp.asarray(lo), -sin[:, idx], 0.0)
        s_hi = jnp.where(jnp.asarray(hi), sin[:, idx], 0.0)
        ident = [jnp.ones((n_ctx, LANES), F32), jnp.zeros((n_ctx, LANES), F32), jnp.zeros((n_ctx, LANES), F32)]
        return [jnp.concatenate([i, t], axis=0) for i, t in zip(ident, (c, s_lo, s_hi))]

    cos_r, sin_r = axial(MLA_ROPE)
    cos_w, sin_w = axial(SWA_DIM)
    lanes = np.arange(LANES)
    mla_rot = np.where((lanes >= MLA_NOPE) & (lanes < MLA_NOPE + MLA_ROPE), lanes - MLA_NOPE, -1)
    return (expand(cos_r, sin_r, mla_rot) + expand(cos_r, sin_r, lanes % DIFF_QK)
            + expand(cos_w, sin_w, lanes % SWA_DIM))


def _rms(x):
    return x * lax.rsqrt(jnp.mean(x * x, axis=-1, keepdims=True) + EPS)


def _modulate(x, g, shift, scale):
    return (_rms(x) * g) * (1.0 + scale) + shift


def _rope(v, c, s_lo, s_hi, half):
    return v * c + pltpu.roll(v, half, 1) * s_hi + pltpu.roll(v, LANES - half, 1) * s_lo


def _qk(q, k):
    return lax.dot_general(q, k, (((1,), (1,)), ((), ())), preferred_element_type=F32)


def _lane_iota(shape):
    return lax.broadcasted_iota(I32, shape, len(shape) - 1)


def _masked_rows(q, segs, width):
    lane = _lane_iota(q.shape)
    zero = jnp.zeros_like(q)
    return jnp.concatenate([jnp.where((lane >= sg * width) & (lane < (sg + 1) * width), q, zero) for sg in segs], axis=0)


def _ada_kernel(c_ref, w_ref, b_ref, o_ref):
    c = c_ref[...]
    a = c * (1.0 / (1.0 + jnp.exp(-c)))
    o_ref[0] = jnp.dot(a, w_ref[0], precision=HIGHEST, preferred_element_type=F32) + b_ref[0]


def _ada(cc, w_ada, b_ada):
    n_layers, d, n_out = w_ada.shape
    rows = cc.shape[0]
    tn = d
    return pl.pallas_call(
        _ada_kernel,
        grid=(n_layers, n_out // tn),
        in_specs=[pl.BlockSpec((rows, d), lambda l, n: (0, 0)),
                  pl.BlockSpec((1, d, tn), lambda l, n: (l, 0, n)),
                  pl.BlockSpec((1, 1, tn), lambda l, n: (l, 0, n))],
        out_specs=pl.BlockSpec((1, rows, tn), lambda l, n: (l, 0, n)),
        out_shape=jax.ShapeDtypeStruct((n_layers, rows, n_out), F32),
        compiler_params=_params("arbitrary", "arbitrary"),
        name="ada",
    )(cc, w_ada, b_ada.reshape(n_layers, 1, n_out))


def _moe_combine(step, n_steps, slot_ref, next_slot_ref, y_ref, rw_ref, ybuf, sem):
    nch = ybuf.shape[1] // (TOP_K * TM)
    cur = step % 2

    def start(slots, buf):
        def body(i, carry):
            for u in range(GATHER_UNROLL):
                r = i * GATHER_UNROLL + u
                for k in range(TOP_K):
                    src = pl.multiple_of(slots[0, 0, k * TM + r] * nch, nch)
                    dst = pl.multiple_of((k * TM + r) * nch, nch)
                    pltpu.make_async_copy(y_ref.at[pl.ds(src, nch)], ybuf.at[buf, pl.ds(dst, nch)], sem.at[buf]).start()
            return carry

        lax.fori_loop(0, TM // GATHER_UNROLL, body, 0)

    @pl.when(step == 0)
    def _():
        start(slot_ref, 0)

    @pl.when(step + 1 < n_steps)
    def _():
        start(next_slot_ref, 1 - cur)

    pltpu.make_async_copy(y_ref.at[pl.ds(0, TOP_K * TM * nch)], ybuf.at[cur], sem.at[cur]).wait()
    rw = rw_ref[...]
    acc = None
    for k in range(TOP_K):
        yk = jnp.concatenate([ybuf[cur, pl.ds(k * TM * nch + c, TM, stride=nch), :] for c in range(nch)], axis=-1)
        term = rw[:, k:k + 1] * yk
        acc = term if acc is None else acc + term
    return acc


def _pre_attn_kernel(has_moe, *refs):
    if has_moe:
        slot_ref, next_slot_ref, y_ref, rw_ref, pmod_ref = refs[:5]
        x2_ref, ybuf, sem = refs[-3:]
        refs = refs[5:-3]
    (x_ref, mod_ref, g_ref, win_ref, qn_ref, wuq_ref, kvn_ref, wukv_ref,
     mc_ref, ml_ref, mh_ref, dc_ref, dl_ref, dh_ref, wc_ref, wl_ref, wh_ref,
     mq_ref, mk_ref, mv_ref, dq_ref, dk_ref, dv_ref, sq_ref, sk_ref, sv_ref) = refs
    x = x_ref[0]
    if has_moe:
        step = pl.program_id(0) * pl.num_programs(1) + pl.program_id(1)
        n_steps = pl.num_programs(0) * pl.num_programs(1)
        x = x + pmod_ref[0, 0][5:6] * _moe_combine(step, n_steps, slot_ref, next_slot_ref, y_ref, rw_ref, ybuf, sem)
        x2_ref[0] = x
    mod = mod_ref[0, 0]
    h = _modulate(x, g_ref[0], mod[0:1], mod[1:2])
    p = jnp.dot(h.astype(BF16), win_ref[0], preferred_element_type=F32)

    def grp(g, n=1):
        return p[:, g * LANES:(g + n) * LANES]

    mla_scale = LOG2E * (MLA_NOPE + MLA_ROPE) ** -0.5
    diff_scale = LOG2E * DIFF_QK ** -0.5
    swa_scale = LOG2E * SWA_DIM ** -0.5
    mla_tab = (mc_ref[...], ml_ref[...], mh_ref[...])
    diff_tab = (dc_ref[...], dl_ref[...], dh_ref[...])
    swa_tab = (wc_ref[...], wl_ref[...], wh_ref[...])

    cq = (_rms(grp(G_CQ, 2)) * qn_ref[0]).astype(BF16)
    q = jnp.dot(cq, wuq_ref[0], preferred_element_type=F32)
    ckv = (_rms(grp(G_CKV)) * kvn_ref[0]).astype(BF16)
    kv = jnp.dot(ckv, wukv_ref[0], preferred_element_type=F32)
    k_rope = _rope(grp(G_KROPE), *mla_tab, MLA_ROPE // 2)
    for hd in range(MLA_HEADS):
        sl = slice(hd * LANES, (hd + 1) * LANES)
        mq_ref[0, :, sl] = (_rope(q[:, sl], *mla_tab, MLA_ROPE // 2) * mla_scale).astype(BF16)
        mk_ref[0, :, sl] = (kv[:, sl] + k_rope).astype(BF16)
    def store_values(ref, vals):
        for g in range(2):
            ref[0, :, 2 * g * LANES:(2 * g + 1) * LANES] = vals[:, g * LANES:(g + 1) * LANES].astype(BF16)
            ref[0, :, (2 * g + 1) * LANES:(2 * g + 2) * LANES] = jnp.ones((TM, LANES), BF16)

    store_values(mv_ref, kv[:, MLA_HEADS * LANES:])

    for g in range(2):
        sl = slice(g * LANES, (g + 1) * LANES)
        dq_ref[0, :, sl] = (_rope(grp(G_DQ + g), *diff_tab, DIFF_QK // 2) * diff_scale).astype(BF16)
        dk_ref[0, :, sl] = _rope(grp(G_DK + g), *diff_tab, DIFF_QK // 2).astype(BF16)
    store_values(dv_ref, grp(G_DV, 2))

    for g in range(4):
        sl = slice(g * LANES, (g + 1) * LANES)
        sq_ref[0, :, sl] = (_rope(grp(G_SQ + g), *swa_tab, SWA_DIM // 2) * swa_scale).astype(BF16)
    for g in range(2):
        sl = slice(g * LANES, (g + 1) * LANES)
        sk_ref[0, :, sl] = _rope(grp(G_SK + g), *swa_tab, SWA_DIM // 2).astype(BF16)
    store_values(sv_ref, grp(G_SV, 2))


def _slot_tiles(slots):
    t = slots.shape[0]
    return slots.reshape(t // TM, TM, TOP_K).transpose(0, 2, 1).reshape(t // TM, 1, TOP_K * TM)


def _pre_attn(layer, xs, mods, g_attn, w_in_p, q_norm, w_uq_p, kv_norm, w_ukv_p, tables, n_ctx, moe=None):
    b, nt, d = xs.shape
    tiles = nt // TM
    n_ctx_tiles = n_ctx // TM
    ctx_row = b

    def tok(w):
        return pl.BlockSpec((1, TM, w), lambda j, bi: (bi, j, 0))

    def lay(shape):
        return pl.BlockSpec((1,) + shape, lambda j, bi: (layer,) + (0,) * len(shape))

    def mod_spec(lyr):
        return pl.BlockSpec((1, 1, N_MOD, d), lambda j, bi: (lyr, jnp.where(j < n_ctx_tiles, ctx_row, bi), 0, 0))

    tab = pl.BlockSpec((TM, LANES), lambda j, bi: (j, 0))
    widths = (512, 512, 512, 256, 256, 512, 512, 256, 512)
    in_specs = [tok(d), mod_spec(layer), lay((1, d)), lay(w_in_p.shape[1:]), lay((1, MLA_Q_RANK)),
                lay(w_uq_p.shape[1:]), lay((1, MLA_KV_RANK)), lay(w_ukv_p.shape[1:])] + [tab] * 9
    out_specs = [tok(w) for w in widths]
    out_shape = [jax.ShapeDtypeStruct((b, nt, w), BF16) for w in widths]
    args = [xs, mods, g_attn, w_in_p, q_norm, w_uq_p, kv_norm, w_ukv_p, *tables]
    scratch = []
    if moe is not None:
        y_rows, slots, rw = moe
        nch = d // LANES

        def next_tile(j, bi):
            wrap = bi + 1 == b
            nj = jnp.minimum(jnp.where(wrap, j + 1, j), tiles - 1)
            return (jnp.where(wrap, 0, bi + 1) * tiles + nj, 0, 0)

        slot_block = (1, 1, TOP_K * TM)
        in_specs = [pl.BlockSpec(slot_block, lambda j, bi: (bi * tiles + j, 0, 0), memory_space=pltpu.SMEM),
                    pl.BlockSpec(slot_block, next_tile, memory_space=pltpu.SMEM),
                    pl.BlockSpec(memory_space=pl.ANY),
                    pl.BlockSpec((TM, LANES), lambda j, bi: (bi * tiles + j, 0)),
                    mod_spec(layer - 1)] + in_specs
        slot_tiles = _slot_tiles(slots)
        args = [slot_tiles, slot_tiles, y_rows, rw, mods] + args
        out_specs = out_specs + [tok(d)]
        out_shape = out_shape + [jax.ShapeDtypeStruct((b, nt, d), F32)]
        scratch = [pltpu.VMEM((2, TOP_K * TM * nch, LANES), F32), pltpu.SemaphoreType.DMA((2,))]
    outs = pl.pallas_call(
        functools.partial(_pre_attn_kernel, moe is not None),
        grid=(tiles, b),
        in_specs=in_specs,
        out_specs=out_specs,
        out_shape=out_shape,
        scratch_shapes=scratch,
        compiler_params=_params("arbitrary", "arbitrary"),
        name="pre_attn",
    )(*args)
    return (outs[-1], outs[:-1]) if moe is not None else (xs, outs)


def _softmax_pv(s, v):
    p = jnp.exp2(s - jnp.max(s, axis=-1, keepdims=True))
    ov = jnp.dot(p.astype(BF16), v, preferred_element_type=F32)
    return ov[:, :LANES] / ov[:, LANES:]


def _dense_steps(n_ctx, nt, run):
    j = pl.program_id(1)

    @pl.when(j == 0)
    def _():
        run(0, n_ctx, n_ctx)

    @pl.when(j > 0)
    def _():
        run(pl.multiple_of(n_ctx + (j - 1) * TQ, TM), TQ, nt)


def _mla_attn_kernel(n_ctx, q_ref, k_ref, v_ref, o_ref):
    def run(row0, rows, nk):
        lane = _lane_iota((rows, LANES))
        outs = []
        for hd in range(MLA_HEADS):
            sl = slice(hd * LANES, (hd + 1) * LANES)
            vs = slice((hd // 2) * 2 * LANES, (hd // 2 + 1) * 2 * LANES)
            s = _qk(q_ref[0, pl.ds(row0, rows), sl], k_ref[0, :nk, sl])
            outs.append(_softmax_pv(s, v_ref[0, :nk, vs]))
        for g in range(2):
            o_ref[0, pl.ds(row0, rows), g * LANES:(g + 1) * LANES] = jnp.where(
                lane < MLA_V, outs[2 * g], outs[2 * g + 1]).astype(BF16)

    _dense_steps(n_ctx, k_ref.shape[1], run)


def _diff_attn_kernel(n_ctx, lambda_init, q_ref, k_ref, v_ref, lam_ref, g_ref, seg_ref, o_ref):
    lam = lam_ref[0]
    lam_full = (jnp.exp(jnp.sum(lam[0:1] * lam[1:2], axis=-1, keepdims=True))
                - jnp.exp(jnp.sum(lam[2:3] * lam[3:4], axis=-1, keepdims=True)) + lambda_init)

    def run(row0, rows, nk):
        lane = _lane_iota((rows, LANES))
        heads = []
        for hd in range(DIFF_HEADS):
            sl = slice((hd // 2) * LANES, (hd // 2 + 1) * LANES)
            q = q_ref[0, pl.ds(row0, rows), sl]
            k = k_ref[0, :nk, sl]
            v = v_ref[0, :nk, (hd // 2) * 2 * LANES:(hd // 2 + 1) * 2 * LANES]
            a = []
            for comp in range(2):
                seg = (hd % 2) * 2 + comp
                qm = jnp.where((lane >= seg * DIFF_QK) & (lane < (seg + 1) * DIFF_QK), q, jnp.zeros_like(q))
                a.append(_softmax_pv(_qk(qm, k), v))
            heads.append(a[0] - lam_full * a[1])
        for g in range(2):
            o = jnp.where(lane < DIFF_V, heads[2 * g], heads[2 * g + 1])
            ms = jnp.dot(o * o, seg_ref[...], precision=HIGHEST, preferred_element_type=F32) * (1.0 / DIFF_V)
            o = o * lax.rsqrt(ms + EPS) * g_ref[...] * (1.0 - lambda_init)
            o_ref[0, pl.ds(row0, rows), g * LANES:(g + 1) * LANES] = o.astype(BF16)

    _dense_steps(n_ctx, k_ref.shape[1], run)


def _swa_attn_kernel(n_ctx, q_ref, k_ref, v_ref, sink_ref, o_ref):
    j = pl.program_id(1)
    nt = k_ref.shape[1]
    band = TM + 2 * WINDOW
    lane = _lane_iota((TM, LANES))
    group_heads = SWA_HEADS // SWA_KV_HEADS

    def run(kv, k, v, allowed):
        qs = jnp.concatenate([_masked_rows(q_ref[0, :, g * LANES:(g + 1) * LANES], range(2), SWA_DIM)
                              for g in (2 * kv, 2 * kv + 1)], axis=0)
        s = _qk(qs, k)
        if allowed is not None:
            s = jnp.where(jnp.concatenate([allowed] * group_heads, axis=0), s, NEG_INF)
        sink = jnp.concatenate([jnp.broadcast_to(sink_ref[0, hd:hd + 1, 0:1] * LOG2E, (TM, 1))
                                for hd in range(kv * group_heads, (kv + 1) * group_heads)], axis=0)
        m = jnp.maximum(jnp.max(s, axis=-1, keepdims=True), sink)
        p = jnp.exp2(s - m)
        ov = jnp.dot(p.astype(BF16), v, preferred_element_type=F32)
        o = ov[:, :LANES] / (ov[:, LANES:] + jnp.exp2(sink - m))
        for i in range(2):
            g = 2 * kv + i
            o_ref[0, :, g * LANES:(g + 1) * LANES] = jnp.where(
                lane < SWA_DIM, o[2 * i * TM:(2 * i + 1) * TM], o[(2 * i + 1) * TM:(2 * i + 2) * TM]).astype(BF16)

    @pl.when(j < n_ctx // TM)
    def _():
        for kv in range(SWA_KV_HEADS):
            ks = slice(kv * LANES, (kv + 1) * LANES)
            vs = slice(2 * kv * LANES, (2 * kv + 2) * LANES)
            run(kv, k_ref[0, :n_ctx, ks], v_ref[0, :n_ctx, vs], None)

    @pl.when(j >= n_ctx // TM)
    def _():
        q0 = j * TM
        w0 = pl.multiple_of(jnp.clip(q0 - WINDOW, n_ctx, nt - band), WINDOW)
        q_pos = q0 + lax.broadcasted_iota(I32, (TM, n_ctx + band), 0)
        col = lax.broadcasted_iota(I32, (TM, n_ctx + band), 1)
        k_pos = w0 + col - n_ctx
        allowed = (col < n_ctx) | (jnp.abs(k_pos - q_pos) <= WINDOW)
        for kv in range(SWA_KV_HEADS):
            ks = slice(kv * LANES, (kv + 1) * LANES)
            vs = slice(2 * kv * LANES, (2 * kv + 2) * LANES)
            k = jnp.concatenate([k_ref[0, :n_ctx, ks], k_ref[0, pl.ds(w0, band), ks]], axis=0)
            v = jnp.concatenate([v_ref[0, :n_ctx, vs], v_ref[0, pl.ds(w0, band), vs]], axis=0)
            run(kv, k, v, allowed)


def _attention(kernel, q, k, v, extra, out_width, name, dense_steps=None):
    b, nt, _ = q.shape

    def whole(width):
        return pl.BlockSpec((1, nt, width), lambda bi, j: (bi, 0, 0))

    def tile(width):
        return pl.BlockSpec((1, TM, width), lambda bi, j: (bi, j, 0))

    q_spec, o_spec, steps = (tile, tile, nt // TM) if dense_steps is None else (whole, whole, dense_steps)
    extra_specs = [pl.BlockSpec(e.shape, lambda bi, j, nd=e.ndim: (0,) * nd) for e in extra]
    return pl.pallas_call(
        kernel,
        grid=(b, steps),
        in_specs=[q_spec(q.shape[2]), whole(k.shape[2]), whole(v.shape[2])] + extra_specs,
        out_specs=o_spec(out_width),
        out_shape=jax.ShapeDtypeStruct((b, nt, out_width), BF16),
        compiler_params=_params("arbitrary", "arbitrary"),
        name=name,
    )(q, k, v, *extra)


def _post_attn_kernel(n_exp, x_ref, mod_ref, mla_ref, diff_ref, swa_ref, wout_ref, g_ref, wr_ref, br_ref,
                      x1_ref, f_ref, route_ref, rw_ref, cnt_ref, carry_ref):
    first = (pl.program_id(0) == 0) & (pl.program_id(1) == 0)

    @pl.when(first)
    def _():
        carry_ref[...] = jnp.zeros_like(carry_ref)

    x = x_ref[0]
    mod = mod_ref[0, 0]
    a = jnp.concatenate([mla_ref[0], diff_ref[0], swa_ref[0]], axis=-1)
    x1 = x + mod[2:3] * jnp.dot(a, wout_ref[0], preferred_element_type=F32)
    x1_ref[0] = x1
    f = _modulate(x1, g_ref[0], mod[3:4], mod[4:5])
    nch = f.shape[1] // LANES
    for c in range(nch):
        f_ref[pl.ds(c, TM, stride=nch), :] = f[:, c * LANES:(c + 1) * LANES]

    logits = jnp.dot(f.astype(BF16), wr_ref[0], preferred_element_type=F32) + br_ref[0]
    lane = _lane_iota((TM, LANES))
    vals, hots = [], []
    for _ in range(TOP_K):
        m = jnp.max(logits, axis=-1, keepdims=True)
        idx = jnp.min(jnp.where(logits == m, lane, LANES), axis=-1, keepdims=True)
        hot = lane == idx
        logits = jnp.where(hot, -3e38, logits)
        vals.append(m)
        hots.append((idx, hot))
    es = [jnp.exp(v - vals[0]) for v in vals]
    denom = functools.reduce(jnp.add, es)
    sel = functools.reduce(jnp.add, [jnp.where(hot, 1.0, 0.0) for _, hot in hots])

    r_io = lax.broadcasted_iota(I32, (TM, TM), 0)
    c_io = lax.broadcasted_iota(I32, (TM, TM), 1)
    tril = jnp.where(c_io < r_io, 1.0, 0.0).astype(BF16)
    rank = carry_ref[0:1, :] + jnp.dot(tril, sel.astype(BF16), preferred_element_type=F32)
    carry_ref[0:1, :] = carry_ref[0:1, :] + jnp.sum(sel, axis=0, keepdims=True)
    cnt_ref[...] = jnp.broadcast_to(carry_ref[0:1, :], cnt_ref.shape)

    route = jnp.zeros((TM, LANES), I32)
    rw = jnp.zeros((TM, LANES), F32)
    for k, (idx, hot) in enumerate(hots):
        rk = jnp.sum(jnp.where(hot, rank, 0.0), axis=-1, keepdims=True).astype(I32)
        route = jnp.where(lane == k, idx, route)
        route = jnp.where(lane == TOP_K + k, rk, route)
        rw = jnp.where(lane == k, es[k] / denom, rw)
    route_ref[...] = route
    rw_ref[...] = rw


def _post_attn(layer, xs, mods, mla_o, diff_o, swa_o, w_out_b, g_ffn, w_router_p, b_router_p, n_ctx, n_exp):
    b, nt, d = xs.shape
    t = b * nt
    nch = d // LANES
    n_ctx_tiles = n_ctx // TM
    tiles = nt // TM

    def tok(w):
        return pl.BlockSpec((1, TM, w), lambda bi, j: (bi, j, 0))

    def lay(shape):
        return pl.BlockSpec((1,) + shape, lambda bi, j: (layer,) + (0,) * len(shape))

    def flat(rows, w):
        return pl.BlockSpec((rows, w), lambda bi, j: (bi * tiles + j, 0))

    mod_spec = pl.BlockSpec((1, 1, N_MOD, d), lambda bi, j: (layer, jnp.where(j < n_ctx_tiles, b, bi), 0, 0))
    return pl.pallas_call(
        functools.partial(_post_attn_kernel, n_exp),
        grid=(b, tiles),
        in_specs=[tok(d), mod_spec, tok(256), tok(256), tok(512), lay(w_out_b.shape[1:]), lay((1, d)),
                  lay((d, LANES)), lay((1, LANES))],
        out_specs=[tok(d), flat(TM * nch, LANES), flat(TM, LANES), flat(TM, LANES),
                   pl.BlockSpec((SUBLANES, LANES), lambda bi, j: (0, 0))],
        out_shape=[jax.ShapeDtypeStruct((b, nt, d), F32), jax.ShapeDtypeStruct((t * nch, LANES), F32),
                   jax.ShapeDtypeStruct((t, LANES), I32), jax.ShapeDtypeStruct((t, LANES), F32),
                   jax.ShapeDtypeStruct((SUBLANES, LANES), F32)],
        scratch_shapes=[pltpu.VMEM((SUBLANES, LANES), F32)],
        compiler_params=_params("arbitrary", "arbitrary"),
        name="post_attn",
    )(xs, mods, mla_o, diff_o, swa_o, w_out_b, g_ffn, w_router_p, b_router_p)


def _dispatch_kernel(n_exp, nch, n_tiles, zstart_ref, zflag_ref, nu_ref, slot_ref, f_ref, xs_ref, zbuf, zsem, sem):
    @pl.when(pl.program_id(0) == 0)
    def _():
        zbuf[...] = jnp.zeros_like(zbuf)

        def zero_tile(row_start):
            dst = xs_ref.at[pl.ds(pl.multiple_of(row_start * nch, SUBLANES), TMOE * nch)]
            cp = pltpu.make_async_copy(zbuf, dst, zsem)
            cp.start()
            cp.wait()

        for e in range(n_exp):
            @pl.when(zflag_ref[e] > 0)
            def _():
                zero_tile(zstart_ref[e])

        def tail(tile, carry):
            zero_tile(tile * TMOE)
            return carry

        lax.fori_loop(nu_ref[0], n_tiles, tail, 0)

    def body(r, carry):
        src = f_ref.at[pl.ds(pl.multiple_of(r * nch, nch), nch)]
        for k in range(TOP_K):
            dst = pl.multiple_of(slot_ref[0, 0, k * TD + r] * nch, nch)
            pltpu.make_async_copy(src, xs_ref.at[pl.ds(dst, nch)], sem).start()
        return carry

    lax.fori_loop(0, TD, body, 0)
    for _ in range(TOP_K):
        pltpu.make_async_copy(f_ref, xs_ref.at[pl.ds(0, TD * nch)], sem).wait()


def _dispatch(f_rows, slots, zstart, zflag, n_used, n_slots, n_exp):
    nch_t = f_rows.shape[0]
    t = slots.shape[0]
    nch = nch_t // t
    slots_km = slots.reshape(t // TD, TD, TOP_K).transpose(0, 2, 1).reshape(t // TD, 1, TOP_K * TD)
    return pl.pallas_call(
        functools.partial(_dispatch_kernel, n_exp, nch, n_slots // TMOE),
        grid_spec=pltpu.PrefetchScalarGridSpec(
            num_scalar_prefetch=3,
            grid=(t // TD,),
            in_specs=[pl.BlockSpec((1, 1, TOP_K * TD), lambda i, zs, zf, nu: (i, 0, 0), memory_space=pltpu.SMEM),
                      pl.BlockSpec((TD * nch, LANES), lambda i, zs, zf, nu: (i, 0))],
            out_specs=pl.BlockSpec(memory_space=pl.ANY),
            scratch_shapes=[pltpu.VMEM((TMOE * nch, LANES), F32), pltpu.SemaphoreType.DMA, pltpu.SemaphoreType.DMA]),
        out_shape=jax.ShapeDtypeStruct((n_slots * nch, LANES), F32),
        compiler_params=_params("arbitrary"),
        name="dispatch",
    )(zstart, zflag, n_used, slots_km, f_rows)


def _moe_kernel(nch, te_ref, nu_ref, x_ref, wg_ref, bg_ref, wu_ref, bu_ref, wd_ref, bd_ref, y_ref,
                wg_b, wu_b, wd_b):
    i = pl.program_id(0)

    @pl.when((i == 0) | (te_ref[i] != te_ref[jnp.maximum(i - 1, 0)]))
    def _():
        wg_b[...] = wg_ref[0].astype(BF16)
        wu_b[...] = wu_ref[0].astype(BF16)
        wd_b[...] = wd_ref[0].astype(BF16)

    @pl.when(i < nu_ref[0])
    def _():
        x = jnp.concatenate([x_ref[pl.ds(c, TMOE, stride=nch), :] for c in range(nch)], axis=-1).astype(BF16)
        gate = jnp.dot(x, wg_b[...], preferred_element_type=F32) + bg_ref[0]
        up = jnp.dot(x, wu_b[...], preferred_element_type=F32) + bu_ref[0]
        gate = jnp.minimum(gate, SWIGLU_LIMIT)
        up = jnp.clip(up, -SWIGLU_LIMIT, SWIGLU_LIMIT)
        act = gate * (1.0 / (1.0 + jnp.exp(-SWIGLU_ALPHA * gate))) * (up + 1.0)
        y = jnp.dot(act.astype(BF16), wd_b[...], preferred_element_type=F32) + bd_ref[0]
        for c in range(nch):
            y_ref[pl.ds(c, TMOE, stride=nch), :] = y[:, c * LANES:(c + 1) * LANES]

    @pl.when(pl.program_id(0) >= nu_ref[0])
    def _():
        y_ref[...] = jnp.zeros_like(y_ref)


def _moe(layer, xs_rows, tile_e, n_used, wg, bg, wu, bu, wd, bd, n_exp):
    d, f = wg.shape[1], wg.shape[2]
    nch = d // LANES
    n_tiles = xs_rows.shape[0] // (TMOE * nch)

    def rows(i, te, nu):
        return (jnp.minimum(i, nu[0] - 1), 0)

    def exp(i, te, nu):
        return (layer * n_exp + te[i], 0, 0)

    return pl.pallas_call(
        functools.partial(_moe_kernel, nch),
        grid_spec=pltpu.PrefetchScalarGridSpec(
            num_scalar_prefetch=2,
            grid=(n_tiles,),
            in_specs=[pl.BlockSpec((TMOE * nch, LANES), rows),
                      pl.BlockSpec((1, d, f), exp), pl.BlockSpec((1, 1, f), exp),
                      pl.BlockSpec((1, d, f), exp), pl.BlockSpec((1, 1, f), exp),
                      pl.BlockSpec((1, f, d), exp), pl.BlockSpec((1, 1, d), exp)],
            out_specs=pl.BlockSpec((TMOE * nch, LANES), lambda i, te, nu: (i, 0)),
            scratch_shapes=[pltpu.VMEM((d, f), BF16), pltpu.VMEM((d, f), BF16), pltpu.VMEM((f, d), BF16)]),
        out_shape=jax.ShapeDtypeStruct(xs_rows.shape, F32),
        compiler_params=_params("arbitrary"),
        name="moe",
    )(tile_e, n_used, xs_rows, wg, bg, wu, bu, wd, bd)


def _final_kernel(slot_ref, next_slot_ref, y_ref, rw_ref, mod_ref, x1_ref, g_ref, o_ref, ybuf, sem):
    step = pl.program_id(0) * pl.num_programs(1) + pl.program_id(1)
    n_steps = pl.num_programs(0) * pl.num_programs(1)
    x = x1_ref[0] + mod_ref[0, 0][5:6] * _moe_combine(step, n_steps, slot_ref, next_slot_ref, y_ref, rw_ref, ybuf, sem)
    o_ref[0] = _rms(x) * g_ref[...]


def _final_norm(layer, x1, y_rows, slots, rw, mods, g_final, n_ctx, s_len):
    b, nt, d = x1.shape
    nch = d // LANES
    tiles = nt // TM
    lat_tiles = s_len // TM
    off = n_ctx // TM

    def next_tile(bi, j):
        wrap = j + 1 == lat_tiles
        nb = jnp.minimum(jnp.where(wrap, bi + 1, bi), b - 1)
        return (nb * tiles + off + jnp.where(wrap, 0, j + 1), 0, 0)

    slot_block = (1, 1, TOP_K * TM)
    slot_tiles = _slot_tiles(slots)
    return pl.pallas_call(
        _final_kernel,
        grid=(b, lat_tiles),
        in_specs=[pl.BlockSpec(slot_block, lambda bi, j: (bi * tiles + off + j, 0, 0), memory_space=pltpu.SMEM),
                  pl.BlockSpec(slot_block, next_tile, memory_space=pltpu.SMEM),
                  pl.BlockSpec(memory_space=pl.ANY),
                  pl.BlockSpec((TM, LANES), lambda bi, j: (bi * tiles + off + j, 0)),
                  pl.BlockSpec((1, 1, N_MOD, d), lambda bi, j: (layer, bi, 0, 0)),
                  pl.BlockSpec((1, TM, d), lambda bi, j: (bi, j + off, 0)),
                  pl.BlockSpec((1, d), lambda bi, j: (0, 0))],
        out_specs=pl.BlockSpec((1, TM, d), lambda bi, j: (bi, j, 0)),
        out_shape=jax.ShapeDtypeStruct((b, s_len, d), F32),
        scratch_shapes=[pltpu.VMEM((2, TOP_K * TM * nch, LANES), F32), pltpu.SemaphoreType.DMA((2,))],
        compiler_params=_params("arbitrary", "arbitrary"),
        name="final_norm",
    )(slot_tiles, slot_tiles, y_rows, rw, mods, x1, g_final.reshape(1, d))


def _routing_plan(route, counts, n_exp, n_tiles):
    idx = route[:, 0:TOP_K]
    rank = route[:, TOP_K:2 * TOP_K]
    counts = counts[0, :n_exp].astype(I32)
    padded = ((counts + TMOE - 1) // TMOE) * TMOE
    ends = jnp.cumsum(padded)
    starts = ends - padded
    onehot = idx[..., None] == jnp.arange(n_exp, dtype=I32)
    slots = jnp.sum(jnp.where(onehot, starts, 0), axis=-1) + rank
    n_used = (ends[-1] // TMOE).reshape(1)
    tile_ids = jnp.minimum(jnp.arange(n_tiles, dtype=I32), n_used[0] - 1)
    tile_e = jnp.sum((tile_ids[:, None] >= (ends // TMOE)[None, :]).astype(I32), axis=-1)
    tile_e = jnp.minimum(tile_e, n_exp - 1)
    zstart = jnp.maximum(ends - TMOE, 0)
    zflag = (padded > 0).astype(I32)
    return slots, tile_e, n_used, zstart, zflag


def kernel(x, c, ctx, c_ctx, w_ada, b_ada, g_attn, w_in, mla_q_norm, mla_w_uq, mla_kv_norm, mla_w_ukv,
           diff_lambda, diff_subln, swa_sink, w_out, g_ffn, w_router, b_router, w_gate, b_gate, w_up, b_up,
           w_down, b_down, g_final):
    b, s_len, d = x.shape
    n_ctx = ctx.shape[1]
    n_layers = w_ada.shape[0]
    n_exp = w_router.shape[2]
    nt = n_ctx + s_len
    t = b * nt
    assert d % LANES == 0 and n_ctx % TM == 0 and s_len % TQ == 0 and t % TD == 0 and s_len % GRID_W == 0
    assert n_exp <= LANES and s_len >= TM + 2 * WINDOW

    w_in_p = _gather_columns(w_in, _in_proj_columns()).astype(BF16)
    w_uq_p = _gather_columns(mla_w_uq, _uq_columns()).astype(BF16)
    w_ukv_p = _gather_columns(mla_w_ukv, _ukv_columns()).astype(BF16)
    w_out_b = w_out.astype(BF16)
    w_router_p = jnp.pad(w_router, ((0, 0), (0, 0), (0, LANES - n_exp))).astype(BF16)
    b_router_p = jnp.pad(b_router, ((0, 0), (0, LANES - n_exp)), constant_values=NEG_INF).reshape(n_layers, 1, LANES)
    wg = w_gate.reshape(n_layers * n_exp, d, -1)
    wu = w_up.reshape(n_layers * n_exp, d, -1)
    wd = w_down.reshape(n_layers * n_exp, -1, d)
    bg = b_gate.reshape(n_layers * n_exp, 1, -1)
    bu = b_up.reshape(n_layers * n_exp, 1, -1)
    bd = b_down.reshape(n_layers * n_exp, 1, d)
    lam_p = jnp.pad(diff_lambda, ((0, 0), (0, 0), (0, LANES - DIFF_QK)))
    subln_p = jnp.tile(diff_subln, (1, LANES // DIFF_V)).reshape(n_layers, 1, LANES)
    sink_p = jnp.broadcast_to(swa_sink[:, :, None], (n_layers, SWA_HEADS, LANES))
    seg = jnp.asarray((np.arange(LANES)[:, None] // DIFF_V == np.arange(LANES)[None, :] // DIFF_V), F32)
    tables = _rope_tables(n_ctx, s_len)

    ada_rows = -(-(b + 1) // SUBLANES) * SUBLANES
    cc = jnp.concatenate([c, c_ctx[None, :], jnp.zeros((ada_rows - b - 1, d), F32)], axis=0)
    mods = _ada(cc, w_ada, b_ada).reshape(n_layers, ada_rows, N_MOD, d)

    n_slots = t * TOP_K + n_exp * TMOE
    n_tiles = n_slots // TMOE
    xs = jnp.concatenate([ctx, x], axis=1)
    moe = None
    for layer in range(n_layers):
        lambda_init = 0.8 - 0.6 * math.exp(-0.3 * layer)
        xs, (mq, mk, mv, dq, dk, dv, sq, sk, sv) = _pre_attn(
            layer, xs, mods, g_attn.reshape(n_layers, 1, d), w_in_p, mla_q_norm.reshape(n_layers, 1, -1), w_uq_p,
            mla_kv_norm.reshape(n_layers, 1, -1), w_ukv_p, tables, n_ctx, moe)
        dense_steps = 1 + s_len // TQ
        mla_o = _attention(functools.partial(_mla_attn_kernel, n_ctx), mq, mk, mv, [], 256, "mla_attn", dense_steps)
        diff_o = _attention(functools.partial(_diff_attn_kernel, n_ctx, lambda_init), dq, dk, dv,
                            [lam_p[layer:layer + 1], subln_p[layer], seg], 256, "diff_attn", dense_steps)
        swa_o = _attention(functools.partial(_swa_attn_kernel, n_ctx), sq, sk, sv, [sink_p[layer:layer + 1]],
                           512, "swa_attn")
        xs, f_rows, route, rw, counts = _post_attn(
            layer, xs, mods, mla_o, diff_o, swa_o, w_out_b, g_ffn.reshape(n_layers, 1, d), w_router_p, b_router_p,
            n_ctx, n_exp)
        slots, tile_e, n_used, zstart, zflag = _routing_plan(route, counts, n_exp, n_tiles)
        xs_rows = _dispatch(f_rows, slots, zstart, zflag, n_used, n_slots, n_exp)
        moe = (_moe(layer, xs_rows, tile_e, n_used, wg, bg, wu, bu, wd, bd, n_exp), slots, rw)
    return _final_norm(n_layers - 1, xs, *moe, mods, g_final, n_ctx, s_len)
```

```python
import functools
import math

import jax
import jax.numpy as jnp
import numpy as np
from jax import lax
from jax.experimental import pallas as pl
from jax.experimental.pallas import tpu as pltpu

F32 = jnp.float32
BF16 = jnp.bfloat16
I32 = jnp.int32
HIGHEST = lax.Precision.HIGHEST
LOG2E = math.log2(math.e)

LANES = 128
SUBLANES = 8
VMEM_LIMIT = 56 * 1024 * 1024

GRID_W = 64
ROPE_THETA = 10000.0
EPS = 1e-6
NEG_INF = -1e30
N_MOD = 6

MLA_HEADS, MLA_Q_RANK, MLA_KV_RANK, MLA_NOPE, MLA_ROPE, MLA_V = 4, 256, 128, 64, 32, 64
DIFF_HEADS, DIFF_QK = 4, 32
DIFF_V = 2 * DIFF_QK
SWA_HEADS, SWA_KV_HEADS, SWA_DIM, WINDOW = 8, 2, 64, 128
TOP_K = 4
SWIGLU_LIMIT = 7.0
SWIGLU_ALPHA = 1.702

TM = 256
TQ = 512
TMOE = 512
TD = 1024
GATHER_UNROLL = 8

G_CQ, G_CKV, G_KROPE, G_DQ, G_DK, G_DV, G_SQ, G_SK, G_SV, N_GROUPS = 0, 2, 3, 4, 6, 8, 10, 14, 16, 18


def _params(*sem):
    return pltpu.CompilerParams(dimension_semantics=sem, vmem_limit_bytes=VMEM_LIMIT)


def _in_proj_columns():
    src = -np.ones(N_GROUPS * LANES, np.int64)
    o_ckv = MLA_Q_RANK
    o_kr = o_ckv + MLA_KV_RANK
    o_dq = o_kr + MLA_ROPE
    o_dk = o_dq + DIFF_HEADS * 2 * DIFF_QK
    o_dv = o_dk + DIFF_HEADS * 2 * DIFF_QK
    o_sq = o_dv + DIFF_HEADS * DIFF_V
    o_sk = o_sq + SWA_HEADS * SWA_DIM
    o_sv = o_sk + SWA_KV_HEADS * SWA_DIM
    src[G_CQ * LANES:G_CQ * LANES + MLA_Q_RANK] = np.arange(MLA_Q_RANK)
    src[G_CKV * LANES:G_CKV * LANES + MLA_KV_RANK] = o_ckv + np.arange(MLA_KV_RANK)
    src[G_KROPE * LANES + MLA_NOPE:G_KROPE * LANES + MLA_NOPE + MLA_ROPE] = o_kr + np.arange(MLA_ROPE)
    src[G_DQ * LANES:G_DQ * LANES + 256] = o_dq + np.arange(256)
    src[G_DK * LANES:G_DK * LANES + 256] = o_dk + np.arange(256)
    src[G_DV * LANES:G_DV * LANES + 256] = o_dv + np.arange(256)
    src[G_SQ * LANES:G_SQ * LANES + 512] = o_sq + np.arange(512)
    for kv in range(SWA_KV_HEADS):
        for half in range(2):
            lo = half * SWA_DIM
            src[(G_SK + kv) * LANES + lo:(G_SK + kv) * LANES + lo + SWA_DIM] = o_sk + kv * SWA_DIM + np.arange(SWA_DIM)
            src[(G_SV + kv) * LANES + lo:(G_SV + kv) * LANES + lo + SWA_DIM] = o_sv + kv * SWA_DIM + np.arange(SWA_DIM)
    return src


def _gather_columns(w, src):
    cols = jnp.take(w, jnp.asarray(np.maximum(src, 0), I32), axis=-1)
    return jnp.where(jnp.asarray(src >= 0), cols, 0.0)


def _uq_columns():
    src = -np.ones(MLA_HEADS * LANES, np.int64)
    hd = MLA_NOPE + MLA_ROPE
    for h in range(MLA_HEADS):
        src[h * LANES:h * LANES + hd] = h * hd + np.arange(hd)
    return src


def _ukv_columns():
    src = -np.ones(MLA_HEADS * LANES + MLA_HEADS * MLA_V, np.int64)
    hd = MLA_NOPE + MLA_V
    for h in range(MLA_HEADS):
        src[h * LANES:h * LANES + MLA_NOPE] = h * hd + np.arange(MLA_NOPE)
        src[MLA_HEADS * LANES + h * MLA_V:MLA_HEADS * LANES + (h + 1) * MLA_V] = h * hd + MLA_NOPE + np.arange(MLA_V)
    return src


def _rope_tables(n_ctx, s_len):
    rows = s_len // GRID_W

    def axial(rot_dim):
        n_freq = rot_dim // 4
        inv_freq = ROPE_THETA ** (-jnp.arange(n_freq, dtype=F32) / n_freq)
        row_pos = jnp.repeat(jnp.arange(rows, dtype=F32), GRID_W)
        col_pos = jnp.tile(jnp.arange(GRID_W, dtype=F32), rows)
        ang = jnp.concatenate([row_pos[:, None] * inv_freq, col_pos[:, None] * inv_freq], axis=-1)
        return jnp.cos(ang), jnp.sin(ang)

    def expand(cos, sin, lane_rot):
        half = cos.shape[1]
        lane_rot = np.asarray(lane_rot)
        idx = np.maximum(lane_rot, 0) % half
        is_rot = lane_rot >= 0
        lo = is_rot & (lane_rot < half)
        hi = is_rot & (lane_rot >= half)
        c = jnp.where(jnp.asarray(is_rot), cos[:, idx], 1.0)
        s_lo = jnp.where(jnp.asarray(lo), -sin[:, idx], 0.0)
        s_hi = jnp.where(jnp.asarray(hi), sin[:, idx], 0.0)
        ident = [jnp.ones((n_ctx, LANES), F32), jnp.zeros((n_ctx, LANES), F32), jnp.zeros((n_ctx, LANES), F32)]
        return [jnp.concatenate([i, t], axis=0) for i, t in zip(ident, (c, s_lo, s_hi))]

    cos_r, sin_r = axial(MLA_ROPE)
    cos_w, sin_w = axial(SWA_DIM)
    lanes = np.arange(LANES)
    mla_rot = np.where((lanes >= MLA_NOPE) & (lanes < MLA_NOPE + MLA_ROPE), lanes - MLA_NOPE, -1)
    return (expand(cos_r, sin_r, mla_rot) + expand(cos_r, sin_r, lanes % DIFF_QK)
            + expand(cos_w, sin_w, lanes % SWA_DIM))


def _rms(x):
    return x * lax.rsqrt(jnp.mean(x * x, axis=-1, keepdims=True) + EPS)


def _modulate(x, g, shift, scale):
    return (_rms(x) * g) * (1.0 + scale) + shift


def _rope(v, c, s_lo, s_hi, half):
    return v * c + pltpu.roll(v, half, 1) * s_hi + pltpu.roll(v, LANES - half, 1) * s_lo


def _qk(q, k):
    return lax.dot_general(q, k, (((1,), (1,)), ((), ())), preferred_element_type=F32)


def _lane_iota(shape):
    return lax.broadcasted_iota(I32, shape, len(shape) - 1)


def _masked_rows(q, segs, width):
    lane = _lane_iota(q.shape)
    zero = jnp.zeros_like(q)
    return jnp.concatenate([jnp.where((lane >= sg * width) & (lane < (sg + 1) * width), q, zero) for sg in segs], axis=0)


def _ada_kernel(c_ref, w_ref, b_ref, o_ref):
    c = c_ref[...]
    a = c * (1.0 / (1.0 + jnp.exp(-c)))
    o_ref[0] = jnp.dot(a, w_ref[0], precision=HIGHEST, preferred_element_type=F32) + b_ref[0]


def _ada(cc, w_ada, b_ada):
    n_layers, d, n_out = w_ada.shape
    rows = cc.shape[0]
    tn = d
    return pl.pallas_call(
        _ada_kernel,
        grid=(n_layers, n_out // tn),
        in_specs=[pl.BlockSpec((rows, d), lambda l, n: (0, 0)),
                  pl.BlockSpec((1, d, tn), lambda l, n: (l, 0, n)),
                  pl.BlockSpec((1, 1, tn), lambda l, n: (l, 0, n))],
        out_specs=pl.BlockSpec((1, rows, tn), lambda l, n: (l, 0, n)),
        out_shape=jax.ShapeDtypeStruct((n_layers, rows, n_out), F32),
        compiler_params=_params("arbitrary", "arbitrary"),
        name="ada",
    )(cc, w_ada, b_ada.reshape(n_layers, 1, n_out))


def _moe_combine(step, n_steps, slot_ref, next_slot_ref, y_ref, rw_ref, ybuf, sem):
    nch = ybuf.shape[1] // (TOP_K * TM)
    cur = step % 2

    def start(slots, buf):
        def body(i, carry):
            for u in range(GATHER_UNROLL):
                r = i * GATHER_UNROLL + u
                for k in range(TOP_K):
                    src = pl.multiple_of(slots[0, 0, k * TM + r] * nch, nch)
                    dst = pl.multiple_of((k * TM + r) * nch, nch)
                    pltpu.make_async_copy(y_ref.at[pl.ds(src, nch)], ybuf.at[buf, pl.ds(dst, nch)], sem.at[buf]).start()
            return carry

        lax.fori_loop(0, TM // GATHER_UNROLL, body, 0)

    @pl.when(step == 0)
    def _():
        start(slot_ref, 0)

    @pl.when(step + 1 < n_steps)
    def _():
        start(next_slot_ref, 1 - cur)

    pltpu.make_async_copy(y_ref.at[pl.ds(0, TOP_K * TM * nch)], ybuf.at[cur], sem.at[cur]).wait()
    rw = rw_ref[...]
    acc = None
    for k in range(TOP_K):
        yk = jnp.concatenate([ybuf[cur, pl.ds(k * TM * nch + c, TM, stride=nch), :] for c in range(nch)], axis=-1)
        term = rw[:, k:k + 1] * yk
        acc = term if acc is None else acc + term
    return acc


def _pre_attn_kernel(has_moe, *refs):
    if has_moe:
        slot_ref, next_slot_ref, y_ref, rw_ref, pmod_ref = refs[:5]
        x2_ref, ybuf, sem = refs[-3:]
        refs = refs[5:-3]
    (x_ref, mod_ref, g_ref, win_ref, qn_ref, wuq_ref, kvn_ref, wukv_ref,
     mc_ref, ml_ref, mh_ref, dc_ref, dl_ref, dh_ref, wc_ref, wl_ref, wh_ref,
     mq_ref, mk_ref, mv_ref, dq_ref, dk_ref, dv_ref, sq_ref, sk_ref, sv_ref) = refs
    x = x_ref[0]
    if has_moe:
        step = pl.program_id(0) * pl.num_programs(1) + pl.program_id(1)
        n_steps = pl.num_programs(0) * pl.num_programs(1)
        x = x + pmod_ref[0, 0][5:6] * _moe_combine(step, n_steps, slot_ref, next_slot_ref, y_ref, rw_ref, ybuf, sem)
        x2_ref[0] = x
    mod = mod_ref[0, 0]
    h = _modulate(x, g_ref[0], mod[0:1], mod[1:2])
    p = jnp.dot(h.astype(BF16), win_ref[0], preferred_element_type=F32)

    def grp(g, n=1):
        return p[:, g * LANES:(g + n) * LANES]

    mla_scale = LOG2E * (MLA_NOPE + MLA_ROPE) ** -0.5
    diff_scale = LOG2E * DIFF_QK ** -0.5
    swa_scale = LOG2E * SWA_DIM ** -0.5
    mla_tab = (mc_ref[...], ml_ref[...], mh_ref[...])
    diff_tab = (dc_ref[...], dl_ref[...], dh_ref[...])
    swa_tab = (wc_ref[...], wl_ref[...], wh_ref[...])

    cq = (_rms(grp(G_CQ, 2)) * qn_ref[0]).astype(BF16)
    q = jnp.dot(cq, wuq_ref[0], preferred_element_type=F32)
    ckv = (_rms(grp(G_CKV)) * kvn_ref[0]).astype(BF16)
    kv = jnp.dot(ckv, wukv_ref[0], preferred_element_type=F32)
    k_rope = _rope(grp(G_KROPE), *mla_tab, MLA_ROPE // 2)
    for hd in range(MLA_HEADS):
        sl = slice(hd * LANES, (hd + 1) * LANES)
        mq_ref[0, :, sl] = (_rope(q[:, sl], *mla_tab, MLA_ROPE // 2) * mla_scale).astype(BF16)
        mk_ref[0, :, sl] = (kv[:, sl] + k_rope).astype(BF16)
    def store_values(ref, vals):
        for g in range(2):
            ref[0, :, 2 * g * LANES:(2 * g + 1) * LANES] = vals[:, g * LANES:(g + 1) * LANES].astype(BF16)
            ref[0, :, (2 * g + 1) * LANES:(2 * g + 2) * LANES] = jnp.ones((TM, LANES), BF16)

    store_values(mv_ref, kv[:, MLA_HEADS * LANES:])

    for g in range(2):
        sl = slice(g * LANES, (g + 1) * LANES)
        dq_ref[0, :, sl] = (_rope(grp(G_DQ + g), *diff_tab, DIFF_QK // 2) * diff_scale).astype(BF16)
        dk_ref[0, :, sl] = _rope(grp(G_DK + g), *diff_tab, DIFF_QK // 2).astype(BF16)
    store_values(dv_ref, grp(G_DV, 2))

    for g in range(4):
        sl = slice(g * LANES, (g + 1) * LANES)
        sq_ref[0, :, sl] = (_rope(grp(G_SQ + g), *swa_tab, SWA_DIM // 2) * swa_scale).astype(BF16)
    for g in range(2):
        sl = slice(g * LANES, (g + 1) * LANES)
        sk_ref[0, :, sl] = _rope(grp(G_SK + g), *swa_tab, SWA_DIM // 2).astype(BF16)
    store_values(sv_ref, grp(G_SV, 2))


def _slot_tiles(slots):
    t = slots.shape[0]
    return slots.reshape(t // TM, TM, TOP_K).transpose(0, 2, 1).reshape(t // TM, 1, TOP_K * TM)


def _pre_attn(layer, xs, mods, g_attn, w_in_p, q_norm, w_uq_p, kv_norm, w_ukv_p, tables, n_ctx, moe=None):
    b, nt, d = xs.shape
    tiles = nt // TM
    n_ctx_tiles = n_ctx // TM
    ctx_row = b

    def tok(w):
        return pl.BlockSpec((1, TM, w), lambda j, bi: (bi, j, 0))

    def lay(shape):
        return pl.BlockSpec((1,) + shape, lambda j, bi: (layer,) + (0,) * len(shape))

    def mod_spec(lyr):
        return pl.BlockSpec((1, 1, N_MOD, d), lambda j, bi: (lyr, jnp.where(j < n_ctx_tiles, ctx_row, bi), 0, 0))

    tab = pl.BlockSpec((TM, LANES), lambda j, bi: (j, 0))
    widths = (512, 512, 512, 256, 256, 512, 512, 256, 512)
    in_specs = [tok(d), mod_spec(layer), lay((1, d)), lay(w_in_p.shape[1:]), lay((1, MLA_Q_RANK)),
                lay(w_uq_p.shape[1:]), lay((1, MLA_KV_RANK)), lay(w_ukv_p.shape[1:])] + [tab] * 9
    out_specs = [tok(w) for w in widths]
    out_shape = [jax.ShapeDtypeStruct((b, nt, w), BF16) for w in widths]
    args = [xs, mods, g_attn, w_in_p, q_norm, w_uq_p, kv_norm, w_ukv_p, *tables]
    scratch = []
    if moe is not None:
        y_rows, slots, rw = moe
        nch = d // LANES

        def next_tile(j, bi):
            wrap = bi + 1 == b
            nj = jnp.minimum(jnp.where(wrap, j + 1, j), tiles - 1)
            return (jnp.where(wrap, 0, bi + 1) * tiles + nj, 0, 0)

        slot_block = (1, 1, TOP_K * TM)
        in_specs = [pl.BlockSpec(slot_block, lambda j, bi: (bi * tiles + j, 0, 0), memory_space=pltpu.SMEM),
                    pl.BlockSpec(slot_block, next_tile, memory_space=pltpu.SMEM),
                    pl.BlockSpec(memory_space=pl.ANY),
                    pl.BlockSpec((TM, LANES), lambda j, bi: (bi * tiles + j, 0)),
                    mod_spec(layer - 1)] + in_specs
        slot_tiles = _slot_tiles(slots)
        args = [slot_tiles, slot_tiles, y_rows, rw, mods] + args
        out_specs = out_specs + [tok(d)]
        out_shape = out_shape + [jax.ShapeDtypeStruct((b, nt, d), F32)]
        scratch = [pltpu.VMEM((2, TOP_K * TM * nch, LANES), F32), pltpu.SemaphoreType.DMA((2,))]
    outs = pl.pallas_call(
        functools.partial(_pre_attn_kernel, moe is not None),
        grid=(tiles, b),
        in_specs=in_specs,
        out_specs=out_specs,
        out_shape=out_shape,
        scratch_shapes=scratch,
        compiler_params=_params("arbitrary", "arbitrary"),
        name="pre_attn",
    )(*args)
    return (outs[-1], outs[:-1]) if moe is not None else (xs, outs)


def _softmax_pv(s, v):
    p = jnp.exp2(s - jnp.max(s, axis=-1, keepdims=True))
    ov = jnp.dot(p.astype(BF16), v, preferred_element_type=F32)
    return ov[:, :LANES] / ov[:, LANES:]


def _dense_steps(n_ctx, nt, run):
    j = pl.program_id(1)

    @pl.when(j == 0)
    def _():
        run(0, n_ctx, n_ctx)

    @pl.when(j > 0)
    def _():
        run(pl.multiple_of(n_ctx + (j - 1) * TQ, TM), TQ, nt)


def _mla_attn_kernel(n_ctx, q_ref, k_ref, v_ref, o_ref):
    def run(row0, rows, nk):
        lane = _lane_iota((rows, LANES))
        outs = []
        for hd in range(MLA_HEADS):
            sl = slice(hd * LANES, (hd + 1) * LANES)
            vs = slice((hd // 2) * 2 * LANES, (hd // 2 + 1) * 2 * LANES)
            s = _qk(q_ref[0, pl.ds(row0, rows), sl], k_ref[0, :nk, sl])
            outs.append(_softmax_pv(s, v_ref[0, :nk, vs]))
        for g in range(2):
            o_ref[0, pl.ds(row0, rows), g * LANES:(g + 1) * LANES] = jnp.where(
                lane < MLA_V, outs[2 * g], outs[2 * g + 1]).astype(BF16)

    _dense_steps(n_ctx, k_ref.shape[1], run)


def _diff_attn_kernel(n_ctx, lambda_init, q_ref, k_ref, v_ref, lam_ref, g_ref, seg_ref, o_ref):
    lam = lam_ref[0]
    lam_full = (jnp.exp(jnp.sum(lam[0:1] * lam[1:2], axis=-1, keepdims=True))
                - jnp.exp(jnp.sum(lam[2:3] * lam[3:4], axis=-1, keepdims=True)) + lambda_init)

    def run(row0, rows, nk):
        lane = _lane_iota((rows, LANES))
        heads = []
        for hd in range(DIFF_HEADS):
            sl = slice((hd // 2) * LANES, (hd // 2 + 1) * LANES)
            q = q_ref[0, pl.ds(row0, rows), sl]
            k = k_ref[0, :nk, sl]
            v = v_ref[0, :nk, (hd // 2) * 2 * LANES:(hd // 2 + 1) * 2 * LANES]
            a = []
            for comp in range(2):
                seg = (hd % 2) * 2 + comp
                qm = jnp.where((lane >= seg * DIFF_QK) & (lane < (seg + 1) * DIFF_QK), q, jnp.zeros_like(q))
                a.append(_softmax_pv(_qk(qm, k), v))
            heads.append(a[0] - lam_full * a[1])
        for g in range(2):
            o = jnp.where(lane < DIFF_V, heads[2 * g], heads[2 * g + 1])
            ms = jnp.dot(o * o, seg_ref[...], precision=HIGHEST, preferred_element_type=F32) * (1.0 / DIFF_V)
            o = o * lax.rsqrt(ms + EPS) * g_ref[...] * (1.0 - lambda_init)
            o_ref[0, pl.ds(row0, rows), g * LANES:(g + 1) * LANES] = o.astype(BF16)

    _dense_steps(n_ctx, k_ref.shape[1], run)


def _swa_attn_kernel(n_ctx, q_ref, k_ref, v_ref, sink_ref, o_ref):
    j = pl.program_id(1)
    nt = k_ref.shape[1]
    band = TM + 2 * WINDOW
    lane = _lane_iota((TM, LANES))
    group_heads = SWA_HEADS // SWA_KV_HEADS

    def run(kv, k, v, allowed):
        for g in (2 * kv, 2 * kv + 1):
            q = q_ref[0, :, g * LANES:(g + 1) * LANES]
            outs = []
            for half in range(2):
                qm = jnp.where((lane >= half * SWA_DIM) & (lane < (half + 1) * SWA_DIM), q, jnp.zeros_like(q))
                s = _qk(qm, k)
                if allowed is not None:
                    s = jnp.where(allowed, s, NEG_INF)
                sink = sink_ref[0, 2 * g + half:2 * g + half + 1, 0:1] * LOG2E
                m = jnp.maximum(jnp.max(s, axis=-1, keepdims=True), sink)
                p = jnp.exp2(s - m)
                ov = jnp.dot(p.astype(BF16), v, preferred_element_type=F32)
                outs.append(ov[:, :LANES] / (ov[:, LANES:] + jnp.exp2(sink - m)))
            o_ref[0, :, g * LANES:(g + 1) * LANES] = jnp.where(lane < SWA_DIM, outs[0], outs[1]).astype(BF16)

    @pl.when(j < n_ctx // TM)
    def _():
        for kv in range(SWA_KV_HEADS):
            ks = slice(kv * LANES, (kv + 1) * LANES)
            vs = slice(2 * kv * LANES, (2 * kv + 2) * LANES)
            run(kv, k_ref[0, :n_ctx, ks], v_ref[0, :n_ctx, vs], None)

    @pl.when(j >= n_ctx // TM)
    def _():
        q0 = j * TM
        w0 = pl.multiple_of(jnp.clip(q0 - WINDOW, n_ctx, nt - band), WINDOW)
        q_pos = q0 + lax.broadcasted_iota(I32, (TM, n_ctx + band), 0)
        col = lax.broadcasted_iota(I32, (TM, n_ctx + band), 1)
        k_pos = w0 + col - n_ctx
        allowed = (col < n_ctx) | (jnp.abs(k_pos - q_pos) <= WINDOW)
        for kv in range(SWA_KV_HEADS):
            ks = slice(kv * LANES, (kv + 1) * LANES)
            vs = slice(2 * kv * LANES, (2 * kv + 2) * LANES)
            k = jnp.concatenate([k_ref[0, :n_ctx, ks], k_ref[0, pl.ds(w0, band), ks]], axis=0)
            v = jnp.concatenate([v_ref[0, :n_ctx, vs], v_ref[0, pl.ds(w0, band), vs]], axis=0)
            run(kv, k, v, allowed)


def _attention(kernel, q, k, v, extra, out_width, name, dense_steps=None):
    b, nt, _ = q.shape

    def whole(width):
        return pl.BlockSpec((1, nt, width), lambda bi, j: (bi, 0, 0))

    def tile(width):
        return pl.BlockSpec((1, TM, width), lambda bi, j: (bi, j, 0))

    q_spec, o_spec, steps = (tile, tile, nt // TM) if dense_steps is None else (whole, whole, dense_steps)
    extra_specs = [pl.BlockSpec(e.shape, lambda bi, j, nd=e.ndim: (0,) * nd) for e in extra]
    return pl.pallas_call(
        kernel,
        grid=(b, steps),
        in_specs=[q_spec(q.shape[2]), whole(k.shape[2]), whole(v.shape[2])] + extra_specs,
        out_specs=o_spec(out_width),
        out_shape=jax.ShapeDtypeStruct((b, nt, out_width), BF16),
        compiler_params=_params("arbitrary", "arbitrary"),
        name=name,
    )(q, k, v, *extra)


def _post_attn_kernel(n_exp, x_ref, mod_ref, mla_ref, diff_ref, swa_ref, wout_ref, g_ref, wr_ref, br_ref,
                      x1_ref, f_ref, route_ref, rw_ref, cnt_ref, carry_ref):
    first = (pl.program_id(0) == 0) & (pl.program_id(1) == 0)

    @pl.when(first)
    def _():
        carry_ref[...] = jnp.zeros_like(carry_ref)

    x = x_ref[0]
    mod = mod_ref[0, 0]
    a = jnp.concatenate([mla_ref[0], diff_ref[0], swa_ref[0]], axis=-1)
    x1 = x + mod[2:3] * jnp.dot(a, wout_ref[0], preferred_element_type=F32)
    x1_ref[0] = x1
    f = _modulate(x1, g_ref[0], mod[3:4], mod[4:5])
    nch = f.shape[1] // LANES
    for c in range(nch):
        f_ref[pl.ds(c, TM, stride=nch), :] = f[:, c * LANES:(c + 1) * LANES]

    logits = jnp.dot(f.astype(BF16), wr_ref[0], preferred_element_type=F32) + br_ref[0]
    lane = _lane_iota((TM, LANES))
    vals, hots = [], []
    for _ in range(TOP_K):
        m = jnp.max(logits, axis=-1, keepdims=True)
        idx = jnp.min(jnp.where(logits == m, lane, LANES), axis=-1, keepdims=True)
        hot = lane == idx
        logits = jnp.where(hot, -3e38, logits)
        vals.append(m)
        hots.append((idx, hot))
    es = [jnp.exp(v - vals[0]) for v in vals]
    denom = functools.reduce(jnp.add, es)
    sel = functools.reduce(jnp.add, [jnp.where(hot, 1.0, 0.0) for _, hot in hots])

    r_io = lax.broadcasted_iota(I32, (TM, TM), 0)
    c_io = lax.broadcasted_iota(I32, (TM, TM), 1)
    tril = jnp.where(c_io < r_io, 1.0, 0.0).astype(BF16)
    rank = carry_ref[0:1, :] + jnp.dot(tril, sel.astype(BF16), preferred_element_type=F32)
    carry_ref[0:1, :] = carry_ref[0:1, :] + jnp.sum(sel, axis=0, keepdims=True)
    cnt_ref[...] = jnp.broadcast_to(carry_ref[0:1, :], cnt_ref.shape)

    route = jnp.zeros((TM, LANES), I32)
    rw = jnp.zeros((TM, LANES), F32)
    for k, (idx, hot) in enumerate(hots):
        rk = jnp.sum(jnp.where(hot, rank, 0.0), axis=-1, keepdims=True).astype(I32)
        route = jnp.where(lane == k, idx, route)
        route = jnp.where(lane == TOP_K + k, rk, route)
        rw = jnp.where(lane == k, es[k] / denom, rw)
    route_ref[...] = route
    rw_ref[...] = rw


def _post_attn(layer, xs, mods, mla_o, diff_o, swa_o, w_out_b, g_ffn, w_router_p, b_router_p, n_ctx, n_exp):
    b, nt, d = xs.shape
    t = b * nt
    nch = d // LANES
    n_ctx_tiles = n_ctx // TM
    tiles = nt // TM

    def tok(w):
        return pl.BlockSpec((1, TM, w), lambda bi, j: (bi, j, 0))

    def lay(shape):
        return pl.BlockSpec((1,) + shape, lambda bi, j: (layer,) + (0,) * len(shape))

    def flat(rows, w):
        return pl.BlockSpec((rows, w), lambda bi, j: (bi * tiles + j, 0))

    mod_spec = pl.BlockSpec((1, 1, N_MOD, d), lambda bi, j: (layer, jnp.where(j < n_ctx_tiles, b, bi), 0, 0))
    return pl.pallas_call(
        functools.partial(_post_attn_kernel, n_exp),
        grid=(b, tiles),
        in_specs=[tok(d), mod_spec, tok(256), tok(256), tok(512), lay(w_out_b.shape[1:]), lay((1, d)),
                  lay((d, LANES)), lay((1, LANES))],
        out_specs=[tok(d), flat(TM * nch, LANES), flat(TM, LANES), flat(TM, LANES),
                   pl.BlockSpec((SUBLANES, LANES), lambda bi, j: (0, 0))],
        out_shape=[jax.ShapeDtypeStruct((b, nt, d), F32), jax.ShapeDtypeStruct((t * nch, LANES), F32),
                   jax.ShapeDtypeStruct((t, LANES), I32), jax.ShapeDtypeStruct((t, LANES), F32),
                   jax.ShapeDtypeStruct((SUBLANES, LANES), F32)],
        scratch_shapes=[pltpu.VMEM((SUBLANES, LANES), F32)],
        compiler_params=_params("arbitrary", "arbitrary"),
        name="post_attn",
    )(xs, mods, mla_o, diff_o, swa_o, w_out_b, g_ffn, w_router_p, b_router_p)


def _dispatch_kernel(n_exp, nch, n_tiles, zstart_ref, zflag_ref, nu_ref, slot_ref, f_ref, xs_ref, zbuf, zsem, sem):
    @pl.when(pl.program_id(0) == 0)
    def _():
        zbuf[...] = jnp.zeros_like(zbuf)

        def zero_tile(row_start):
            dst = xs_ref.at[pl.ds(pl.multiple_of(row_start * nch, SUBLANES), TMOE * nch)]
            cp = pltpu.make_async_copy(zbuf, dst, zsem)
            cp.start()
            cp.wait()

        for e in range(n_exp):
            @pl.when(zflag_ref[e] > 0)
            def _():
                zero_tile(zstart_ref[e])

        def tail(tile, carry):
            zero_tile(tile * TMOE)
            return carry

        lax.fori_loop(nu_ref[0], n_tiles, tail, 0)

    def body(r, carry):
        src = f_ref.at[pl.ds(pl.multiple_of(r * nch, nch), nch)]
        for k in range(TOP_K):
            dst = pl.multiple_of(slot_ref[0, 0, k * TD + r] * nch, nch)
            pltpu.make_async_copy(src, xs_ref.at[pl.ds(dst, nch)], sem).start()
        return carry

    lax.fori_loop(0, TD, body, 0)
    for _ in range(TOP_K):
        pltpu.make_async_copy(f_ref, xs_ref.at[pl.ds(0, TD * nch)], sem).wait()


def _dispatch(f_rows, slots, zstart, zflag, n_used, n_slots, n_exp):
    nch_t = f_rows.shape[0]
    t = slots.shape[0]
    nch = nch_t // t
    slots_km = slots.reshape(t // TD, TD, TOP_K).transpose(0, 2, 1).reshape(t // TD, 1, TOP_K * TD)
    return pl.pallas_call(
        functools.partial(_dispatch_kernel, n_exp, nch, n_slots // TMOE),
        grid_spec=pltpu.PrefetchScalarGridSpec(
            num_scalar_prefetch=3,
            grid=(t // TD,),
            in_specs=[pl.BlockSpec((1, 1, TOP_K * TD), lambda i, zs, zf, nu: (i, 0, 0), memory_space=pltpu.SMEM),
                      pl.BlockSpec((TD * nch, LANES), lambda i, zs, zf, nu: (i, 0))],
            out_specs=pl.BlockSpec(memory_space=pl.ANY),
            scratch_shapes=[pltpu.VMEM((TMOE * nch, LANES), F32), pltpu.SemaphoreType.DMA, pltpu.SemaphoreType.DMA]),
        out_shape=jax.ShapeDtypeStruct((n_slots * nch, LANES), F32),
        compiler_params=_params("arbitrary"),
        name="dispatch",
    )(zstart, zflag, n_used, slots_km, f_rows)


def _moe_kernel(nch, te_ref, nu_ref, x_ref, wg_ref, bg_ref, wu_ref, bu_ref, wd_ref, bd_ref, y_ref,
                wg_b, wu_b, wd_b):
    i = pl.program_id(0)

    @pl.when((i == 0) | (te_ref[i] != te_ref[jnp.maximum(i - 1, 0)]))
    def _():
        wg_b[...] = wg_ref[0].astype(BF16)
        wu_b[...] = wu_ref[0].astype(BF16)
        wd_b[...] = wd_ref[0].astype(BF16)

    @pl.when(i < nu_ref[0])
    def _():
        x = jnp.concatenate([x_ref[pl.ds(c, TMOE, stride=nch), :] for c in range(nch)], axis=-1).astype(BF16)
        gate = jnp.dot(x, wg_b[...], preferred_element_type=F32) + bg_ref[0]
        up = jnp.dot(x, wu_b[...], preferred_element_type=F32) + bu_ref[0]
        gate = jnp.minimum(gate, SWIGLU_LIMIT)
        up = jnp.clip(up, -SWIGLU_LIMIT, SWIGLU_LIMIT)
        act = gate * (1.0 / (1.0 + jnp.exp(-SWIGLU_ALPHA * gate))) * (up + 1.0)
        y = jnp.dot(act.astype(BF16), wd_b[...], preferred_element_type=F32) + bd_ref[0]
        for c in range(nch):
            y_ref[pl.ds(c, TMOE, stride=nch), :] = y[:, c * LANES:(c + 1) * LANES]

    @pl.when(pl.program_id(0) >= nu_ref[0])
    def _():
        y_ref[...] = jnp.zeros_like(y_ref)


def _moe(layer, xs_rows, tile_e, n_used, wg, bg, wu, bu, wd, bd, n_exp):
    d, f = wg.shape[1], wg.shape[2]
    nch = d // LANES
    n_tiles = xs_rows.shape[0] // (TMOE * nch)

    def rows(i, te, nu):
        return (jnp.minimum(i, nu[0] - 1), 0)

    def exp(i, te, nu):
        return (layer * n_exp + te[i], 0, 0)

    return pl.pallas_call(
        functools.partial(_moe_kernel, nch),
        grid_spec=pltpu.PrefetchScalarGridSpec(
            num_scalar_prefetch=2,
            grid=(n_tiles,),
            in_specs=[pl.BlockSpec((TMOE * nch, LANES), rows),
                      pl.BlockSpec((1, d, f), exp), pl.BlockSpec((1, 1, f), exp),
                      pl.BlockSpec((1, d, f), exp), pl.BlockSpec((1, 1, f), exp),
                      pl.BlockSpec((1, f, d), exp), pl.BlockSpec((1, 1, d), exp)],
            out_specs=pl.BlockSpec((TMOE * nch, LANES), lambda i, te, nu: (i, 0)),
            scratch_shapes=[pltpu.VMEM((d, f), BF16), pltpu.VMEM((d, f), BF16), pltpu.VMEM((f, d), BF16)]),
        out_shape=jax.ShapeDtypeStruct(xs_rows.shape, F32),
        compiler_params=_params("arbitrary"),
        name="moe",
    )(tile_e, n_used, xs_rows, wg, bg, wu, bu, wd, bd)


def _final_kernel(slot_ref, next_slot_ref, y_ref, rw_ref, mod_ref, x1_ref, g_ref, o_ref, ybuf, sem):
    step = pl.program_id(0) * pl.num_programs(1) + pl.program_id(1)
    n_steps = pl.num_programs(0) * pl.num_programs(1)
    x = x1_ref[0] + mod_ref[0, 0][5:6] * _moe_combine(step, n_steps, slot_ref, next_slot_ref, y_ref, rw_ref, ybuf, sem)
    o_ref[0] = _rms(x) * g_ref[...]


def _final_norm(layer, x1, y_rows, slots, rw, mods, g_final, n_ctx, s_len):
    b, nt, d = x1.shape
    nch = d // LANES
    tiles = nt // TM
    lat_tiles = s_len // TM
    off = n_ctx // TM

    def next_tile(bi, j):
        wrap = j + 1 == lat_tiles
        nb = jnp.minimum(jnp.where(wrap, bi + 1, bi), b - 1)
        return (nb * tiles + off + jnp.where(wrap, 0, j + 1), 0, 0)

    slot_block = (1, 1, TOP_K * TM)
    slot_tiles = _slot_tiles(slots)
    return pl.pallas_call(
        _final_kernel,
        grid=(b, lat_tiles),
        in_specs=[pl.BlockSpec(slot_block, lambda bi, j: (bi * tiles + off + j, 0, 0), memory_space=pltpu.SMEM),
                  pl.BlockSpec(slot_block, next_tile, memory_space=pltpu.SMEM),
                  pl.BlockSpec(memory_space=pl.ANY),
                  pl.BlockSpec((TM, LANES), lambda bi, j: (bi * tiles + off + j, 0)),
                  pl.BlockSpec((1, 1, N_MOD, d), lambda bi, j: (layer, bi, 0, 0)),
                  pl.BlockSpec((1, TM, d), lambda bi, j: (bi, j + off, 0)),
                  pl.BlockSpec((1, d), lambda bi, j: (0, 0))],
        out_specs=pl.BlockSpec((1, TM, d), lambda bi, j: (bi, j, 0)),
        out_shape=jax.ShapeDtypeStruct((b, s_len, d), F32),
        scratch_shapes=[pltpu.VMEM((2, TOP_K * TM * nch, LANES), F32), pltpu.SemaphoreType.DMA((2,))],
        compiler_params=_params("arbitrary", "arbitrary"),
        name="final_norm",
    )(slot_tiles, slot_tiles, y_rows, rw, mods, x1, g_final.reshape(1, d))


def _routing_plan(route, counts, n_exp, n_tiles):
    idx = route[:, 0:TOP_K]
    rank = route[:, TOP_K:2 * TOP_K]
    counts = counts[0, :n_exp].astype(I32)
    padded = ((counts + TMOE - 1) // TMOE) * TMOE
    ends = jnp.cumsum(padded)
    starts = ends - padded
    onehot = idx[..., None] == jnp.arange(n_exp, dtype=I32)
    slots = jnp.sum(jnp.where(onehot, starts, 0), axis=-1) + rank
    n_used = (ends[-1] // TMOE).reshape(1)
    tile_ids = jnp.minimum(jnp.arange(n_tiles, dtype=I32), n_used[0] - 1)
    tile_e = jnp.sum((tile_ids[:, None] >= (ends // TMOE)[None, :]).astype(I32), axis=-1)
    tile_e = jnp.minimum(tile_e, n_exp - 1)
    zstart = jnp.maximum(ends - TMOE, 0)
    zflag = (padded > 0).astype(I32)
    return slots, tile_e, n_used, zstart, zflag


def kernel(x, c, ctx, c_ctx, w_ada, b_ada, g_attn, w_in, mla_q_norm, mla_w_uq, mla_kv_norm, mla_w_ukv,
           diff_lambda, diff_subln, swa_sink, w_out, g_ffn, w_router, b_router, w_gate, b_gate, w_up, b_up,
           w_down, b_down, g_final):
    b, s_len, d = x.shape
    n_ctx = ctx.shape[1]
    n_layers = w_ada.shape[0]
    n_exp = w_router.shape[2]
    nt = n_ctx + s_len
    t = b * nt
    assert d % LANES == 0 and n_ctx % TM == 0 and s_len % TQ == 0 and t % TD == 0 and s_len % GRID_W == 0
    assert n_exp <= LANES and s_len >= TM + 2 * WINDOW

    w_in_p = _gather_columns(w_in, _in_proj_columns()).astype(BF16)
    w_uq_p = _gather_columns(mla_w_uq, _uq_columns()).astype(BF16)
    w_ukv_p = _gather_columns(mla_w_ukv, _ukv_columns()).astype(BF16)
    w_out_b = w_out.astype(BF16)
    w_router_p = jnp.pad(w_router, ((0, 0), (0, 0), (0, LANES - n_exp))).astype(BF16)
    b_router_p = jnp.pad(b_router, ((0, 0), (0, LANES - n_exp)), constant_values=NEG_INF).reshape(n_layers, 1, LANES)
    wg = w_gate.reshape(n_layers * n_exp, d, -1)
    wu = w_up.reshape(n_layers * n_exp, d, -1)
    wd = w_down.reshape(n_layers * n_exp, -1, d)
    bg = b_gate.reshape(n_layers * n_exp, 1, -1)
    bu = b_up.reshape(n_layers * n_exp, 1, -1)
    bd = b_down.reshape(n_layers * n_exp, 1, d)
    lam_p = jnp.pad(diff_lambda, ((0, 0), (0, 0), (0, LANES - DIFF_QK)))
    subln_p = jnp.tile(diff_subln, (1, LANES // DIFF_V)).reshape(n_layers, 1, LANES)
    sink_p = jnp.broadcast_to(swa_sink[:, :, None], (n_layers, SWA_HEADS, LANES))
    seg = jnp.asarray((np.arange(LANES)[:, None] // DIFF_V == np.arange(LANES)[None, :] // DIFF_V), F32)
    tables = _rope_tables(n_ctx, s_len)

    ada_rows = -(-(b + 1) // SUBLANES) * SUBLANES
    cc = jnp.concatenate([c, c_ctx[None, :], jnp.zeros((ada_rows - b - 1, d), F32)], axis=0)
    mods = _ada(cc, w_ada, b_ada).reshape(n_layers, ada_rows, N_MOD, d)

    n_slots = t * TOP_K + n_exp * TMOE
    n_tiles = n_slots // TMOE
    xs = jnp.concatenate([ctx, x], axis=1)
    moe = None
    for layer in range(n_layers):
        lambda_init = 0.8 - 0.6 * math.exp(-0.3 * layer)
        xs, (mq, mk, mv, dq, dk, dv, sq, sk, sv) = _pre_attn(
            layer, xs, mods, g_attn.reshape(n_layers, 1, d), w_in_p, mla_q_norm.reshape(n_layers, 1, -1), w_uq_p,
            mla_kv_norm.reshape(n_layers, 1, -1), w_ukv_p, tables, n_ctx, moe)
        dense_steps = 1 + s_len // TQ
        mla_o = _attention(functools.partial(_mla_attn_kernel, n_ctx), mq, mk, mv, [], 256, "mla_attn", dense_steps)
        diff_o = _attention(functools.partial(_diff_attn_kernel, n_ctx, lambda_init), dq, dk, dv,
                            [lam_p[layer:layer + 1], subln_p[layer], seg], 256, "diff_attn", dense_steps)
        swa_o = _attention(functools.partial(_swa_attn_kernel, n_ctx), sq, sk, sv, [sink_p[layer:layer + 1]],
                           512, "swa_attn")
        xs, f_rows, route, rw, counts = _post_attn(
            layer, xs, mods, mla_o, diff_o, swa_o, w_out_b, g_ffn.reshape(n_layers, 1, d), w_router_p, b_router_p,
            n_ctx, n_exp)
        slots, tile_e, n_used, zstart, zflag = _routing_plan(route, counts, n_exp, n_tiles)
        xs_rows = _dispatch(f_rows, slots, zstart, zflag, n_used, n_slots, n_exp)
        moe = (_moe(layer, xs_rows, tile_e, n_used, wg, bg, wu, bu, wd, bd, n_exp), slots, rw)
    return _final_norm(n_layers - 1, xs, *moe, mods, g_final, n_ctx, s_len)
```

```python
import functools
import math

import jax
import jax.numpy as jnp
import numpy as np
from jax import lax
from jax.experimental import pallas as pl
from jax.experimental.pallas import tpu as pltpu

F32 = jnp.float32
BF16 = jnp.bfloat16
I32 = jnp.int32
HIGHEST = lax.Precision.HIGHEST
LOG2E = math.log2(math.e)

LANES = 128
SUBLANES = 8
VMEM_LIMIT = 56 * 1024 * 1024

GRID_W = 64
ROPE_THETA = 10000.0
EPS = 1e-6
NEG_INF = -1e30
N_MOD = 6

MLA_HEADS, MLA_Q_RANK, MLA_KV_RANK, MLA_NOPE, MLA_ROPE, MLA_V = 4, 256, 128, 64, 32, 64
DIFF_HEADS, DIFF_QK = 4, 32
DIFF_V = 2 * DIFF_QK
SWA_HEADS, SWA_KV_HEADS, SWA_DIM, WINDOW = 8, 2, 64, 128
TOP_K = 4
SWIGLU_LIMIT = 7.0
SWIGLU_ALPHA = 1.702

TM = 256
TQ = 512
TMOE = 512
GATHER_UNROLL = 8

G_CQ, G_CKV, G_KROPE, G_DQ, G_DK, G_DV, G_SQ, G_SK, G_SV, N_GROUPS = 0, 2, 3, 4, 6, 8, 10, 14, 16, 18


def _params(*sem):
    return pltpu.CompilerParams(dimension_semantics=sem, vmem_limit_bytes=VMEM_LIMIT)


def _in_proj_columns():
    src = -np.ones(N_GROUPS * LANES, np.int64)
    o_ckv = MLA_Q_RANK
    o_kr = o_ckv + MLA_KV_RANK
    o_dq = o_kr + MLA_ROPE
    o_dk = o_dq + DIFF_HEADS * 2 * DIFF_QK
    o_dv = o_dk + DIFF_HEADS * 2 * DIFF_QK
    o_sq = o_dv + DIFF_HEADS * DIFF_V
    o_sk = o_sq + SWA_HEADS * SWA_DIM
    o_sv = o_sk + SWA_KV_HEADS * SWA_DIM
    src[G_CQ * LANES:G_CQ * LANES + MLA_Q_RANK] = np.arange(MLA_Q_RANK)
    src[G_CKV * LANES:G_CKV * LANES + MLA_KV_RANK] = o_ckv + np.arange(MLA_KV_RANK)
    src[G_KROPE * LANES + MLA_NOPE:G_KROPE * LANES + MLA_NOPE + MLA_ROPE] = o_kr + np.arange(MLA_ROPE)
    src[G_DQ * LANES:G_DQ * LANES + 256] = o_dq + np.arange(256)
    src[G_DK * LANES:G_DK * LANES + 256] = o_dk + np.arange(256)
    src[G_DV * LANES:G_DV * LANES + 256] = o_dv + np.arange(256)
    src[G_SQ * LANES:G_SQ * LANES + 512] = o_sq + np.arange(512)
    for kv in range(SWA_KV_HEADS):
        for half in range(2):
            lo = half * SWA_DIM
            src[(G_SK + kv) * LANES + lo:(G_SK + kv) * LANES + lo + SWA_DIM] = o_sk + kv * SWA_DIM + np.arange(SWA_DIM)
            src[(G_SV + kv) * LANES + lo:(G_SV + kv) * LANES + lo + SWA_DIM] = o_sv + kv * SWA_DIM + np.arange(SWA_DIM)
    return src


def _gather_columns(w, src):
    cols = jnp.take(w, jnp.asarray(np.maximum(src, 0), I32), axis=-1)
    return jnp.where(jnp.asarray(src >= 0), cols, 0.0)


def _uq_columns():
    src = -np.ones(MLA_HEADS * LANES, np.int64)
    hd = MLA_NOPE + MLA_ROPE
    for h in range(MLA_HEADS):
        src[h * LANES:h * LANES + hd] = h * hd + np.arange(hd)
    return src


def _ukv_columns():
    src = -np.ones(MLA_HEADS * LANES + MLA_HEADS * MLA_V, np.int64)
    hd = MLA_NOPE + MLA_V
    for h in range(MLA_HEADS):
        src[h * LANES:h * LANES + MLA_NOPE] = h * hd + np.arange(MLA_NOPE)
        src[MLA_HEADS * LANES + h * MLA_V:MLA_HEADS * LANES + (h + 1) * MLA_V] = h * hd + MLA_NOPE + np.arange(MLA_V)
    return src


def _rope_tables(n_ctx, s_len):
    rows = s_len // GRID_W

    def axial(rot_dim):
        n_freq = rot_dim // 4
        inv_freq = ROPE_THETA ** (-jnp.arange(n_freq, dtype=F32) / n_freq)
        row_pos = jnp.repeat(jnp.arange(rows, dtype=F32), GRID_W)
        col_pos = jnp.tile(jnp.arange(GRID_W, dtype=F32), rows)
        ang = jnp.concatenate([row_pos[:, None] * inv_freq, col_pos[:, None] * inv_freq], axis=-1)
        return jnp.cos(ang), jnp.sin(ang)

    def expand(cos, sin, lane_rot):
        half = cos.shape[1]
        lane_rot = np.asarray(lane_rot)
        idx = np.maximum(lane_rot, 0) % half
        is_rot = lane_rot >= 0
        lo = is_rot & (lane_rot < half)
        hi = is_rot & (lane_rot >= half)
        c = jnp.where(jnp.asarray(is_rot), cos[:, idx], 1.0)
        s_lo = jnp.where(jnp.asarray(lo), -sin[:, idx], 0.0)
        s_hi = jnp.where(jnp.asarray(hi), sin[:, idx], 0.0)
        ident = [jnp.ones((n_ctx, LANES), F32), jnp.zeros((n_ctx, LANES), F32), jnp.zeros((n_ctx, LANES), F32)]
        return [jnp.concatenate([i, t], axis=0) for i, t in zip(ident, (c, s_lo, s_hi))]

    cos_r, sin_r = axial(MLA_ROPE)
    cos_w, sin_w = axial(SWA_DIM)
    lanes = np.arange(LANES)
    mla_rot = np.where((lanes >= MLA_NOPE) & (lanes < MLA_NOPE + MLA_ROPE), lanes - MLA_NOPE, -1)
    return (expand(cos_r, sin_r, mla_rot) + expand(cos_r, sin_r, lanes % DIFF_QK)
            + expand(cos_w, sin_w, lanes % SWA_DIM))


def _rms(x):
    return x * lax.rsqrt(jnp.mean(x * x, axis=-1, keepdims=True) + EPS)


def _modulate(x, g, shift, scale):
    return (_rms(x) * g) * (1.0 + scale) + shift


def _rope(v, c, s_lo, s_hi, half):
    return v * c + pltpu.roll(v, half, 1) * s_hi + pltpu.roll(v, LANES - half, 1) * s_lo


def _qk(q, k):
    return lax.dot_general(q, k, (((1,), (1,)), ((), ())), preferred_element_type=F32)


def _lane_iota(shape):
    return lax.broadcasted_iota(I32, shape, len(shape) - 1)


def _masked_rows(q, segs, width):
    lane = _lane_iota(q.shape)
    zero = jnp.zeros_like(q)
    return jnp.concatenate([jnp.where((lane >= sg * width) & (lane < (sg + 1) * width), q, zero) for sg in segs], axis=0)


def _ada_kernel(c_ref, w_ref, b_ref, o_ref):
    c = c_ref[...]
    a = c * (1.0 / (1.0 + jnp.exp(-c)))
    o_ref[0] = jnp.dot(a, w_ref[0], precision=HIGHEST, preferred_element_type=F32) + b_ref[0]


def _ada(cc, w_ada, b_ada):
    n_layers, d, n_out = w_ada.shape
    rows = cc.shape[0]
    tn = d
    return pl.pallas_call(
        _ada_kernel,
        grid=(n_layers, n_out // tn),
        in_specs=[pl.BlockSpec((rows, d), lambda l, n: (0, 0)),
                  pl.BlockSpec((1, d, tn), lambda l, n: (l, 0, n)),
                  pl.BlockSpec((1, 1, tn), lambda l, n: (l, 0, n))],
        out_specs=pl.BlockSpec((1, rows, tn), lambda l, n: (l, 0, n)),
        out_shape=jax.ShapeDtypeStruct((n_layers, rows, n_out), F32),
        compiler_params=_params("arbitrary", "arbitrary"),
        name="ada",
    )(cc, w_ada, b_ada.reshape(n_layers, 1, n_out))


def _moe_combine(step, n_steps, slot_ref, next_slot_ref, y_ref, rw_ref, ybuf, sem):
    nch = ybuf.shape[1] // (TOP_K * TM)
    cur = step % 2

    def start(slots, buf):
        def body(i, carry):
            for u in range(GATHER_UNROLL):
                r = i * GATHER_UNROLL + u
                for k in range(TOP_K):
                    src = pl.multiple_of(slots[0, 0, k * TM + r] * nch, nch)
                    dst = pl.multiple_of((k * TM + r) * nch, nch)
                    pltpu.make_async_copy(y_ref.at[pl.ds(src, nch)], ybuf.at[buf, pl.ds(dst, nch)], sem.at[buf]).start()
            return carry

        lax.fori_loop(0, TM // GATHER_UNROLL, body, 0)

    @pl.when(step == 0)
    def _():
        start(slot_ref, 0)

    @pl.when(step + 1 < n_steps)
    def _():
        start(next_slot_ref, 1 - cur)

    pltpu.make_async_copy(y_ref.at[pl.ds(0, TOP_K * TM * nch)], ybuf.at[cur], sem.at[cur]).wait()
    rw = rw_ref[...]
    acc = None
    for k in range(TOP_K):
        yk = jnp.concatenate([ybuf[cur, pl.ds(k * TM * nch + c, TM, stride=nch), :] for c in range(nch)], axis=-1)
        term = rw[:, k:k + 1] * yk
        acc = term if acc is None else acc + term
    return acc


def _pre_attn_kernel(has_moe, *refs):
    if has_moe:
        slot_ref, next_slot_ref, y_ref, rw_ref, pmod_ref = refs[:5]
        x2_ref, ybuf, sem = refs[-3:]
        refs = refs[5:-3]
    (x_ref, mod_ref, g_ref, win_ref, qn_ref, wuq_ref, kvn_ref, wukv_ref,
     mc_ref, ml_ref, mh_ref, dc_ref, dl_ref, dh_ref, wc_ref, wl_ref, wh_ref,
     mq_ref, mk_ref, mv_ref, dq_ref, dk_ref, dv_ref, sq_ref, sk_ref, sv_ref) = refs
    x = x_ref[0]
    if has_moe:
        step = pl.program_id(0) * pl.num_programs(1) + pl.program_id(1)
        n_steps = pl.num_programs(0) * pl.num_programs(1)
        x = x + pmod_ref[0, 0][5:6] * _moe_combine(step, n_steps, slot_ref, next_slot_ref, y_ref, rw_ref, ybuf, sem)
        x2_ref[0] = x
    mod = mod_ref[0, 0]
    h = _modulate(x, g_ref[0], mod[0:1], mod[1:2])
    p = jnp.dot(h.astype(BF16), win_ref[0], preferred_element_type=F32)

    def grp(g, n=1):
        return p[:, g * LANES:(g + n) * LANES]

    mla_scale = LOG2E * (MLA_NOPE + MLA_ROPE) ** -0.5
    diff_scale = LOG2E * DIFF_QK ** -0.5
    swa_scale = LOG2E * SWA_DIM ** -0.5
    mla_tab = (mc_ref[...], ml_ref[...], mh_ref[...])
    diff_tab = (dc_ref[...], dl_ref[...], dh_ref[...])
    swa_tab = (wc_ref[...], wl_ref[...], wh_ref[...])

    cq = (_rms(grp(G_CQ, 2)) * qn_ref[0]).astype(BF16)
    q = jnp.dot(cq, wuq_ref[0], preferred_element_type=F32)
    ckv = (_rms(grp(G_CKV)) * kvn_ref[0]).astype(BF16)
    kv = jnp.dot(ckv, wukv_ref[0], preferred_element_type=F32)
    k_rope = _rope(grp(G_KROPE), *mla_tab, MLA_ROPE // 2)
    for hd in range(MLA_HEADS):
        sl = slice(hd * LANES, (hd + 1) * LANES)
        mq_ref[0, :, sl] = (_rope(q[:, sl], *mla_tab, MLA_ROPE // 2) * mla_scale).astype(BF16)
        mk_ref[0, :, sl] = (kv[:, sl] + k_rope).astype(BF16)
    def store_values(ref, vals):
        for g in range(2):
            ref[0, :, 2 * g * LANES:(2 * g + 1) * LANES] = vals[:, g * LANES:(g + 1) * LANES].astype(BF16)
            ref[0, :, (2 * g + 1) * LANES:(2 * g + 2) * LANES] = jnp.ones((TM, LANES), BF16)

    store_values(mv_ref, kv[:, MLA_HEADS * LANES:])

    for g in range(2):
        sl = slice(g * LANES, (g + 1) * LANES)
        dq_ref[0, :, sl] = (_rope(grp(G_DQ + g), *diff_tab, DIFF_QK // 2) * diff_scale).astype(BF16)
        dk_ref[0, :, sl] = _rope(grp(G_DK + g), *diff_tab, DIFF_QK // 2).astype(BF16)
    store_values(dv_ref, grp(G_DV, 2))

    for g in range(4):
        sl = slice(g * LANES, (g + 1) * LANES)
        sq_ref[0, :, sl] = (_rope(grp(G_SQ + g), *swa_tab, SWA_DIM // 2) * swa_scale).astype(BF16)
    for g in range(2):
        sl = slice(g * LANES, (g + 1) * LANES)
        sk_ref[0, :, sl] = _rope(grp(G_SK + g), *swa_tab, SWA_DIM // 2).astype(BF16)
    store_values(sv_ref, grp(G_SV, 2))


def _slot_tiles(slots):
    t = slots.shape[0]
    return slots.reshape(t // TM, TM, TOP_K).transpose(0, 2, 1).reshape(t // TM, 1, TOP_K * TM)


def _pre_attn(layer, xs, mods, g_attn, w_in_p, q_norm, w_uq_p, kv_norm, w_ukv_p, tables, n_ctx, moe=None):
    b, nt, d = xs.shape
    tiles = nt // TM
    n_ctx_tiles = n_ctx // TM
    ctx_row = b

    def tok(w):
        return pl.BlockSpec((1, TM, w), lambda j, bi: (bi, j, 0))

    def lay(shape):
        return pl.BlockSpec((1,) + shape, lambda j, bi: (layer,) + (0,) * len(shape))

    def mod_spec(lyr):
        return pl.BlockSpec((1, 1, N_MOD, d), lambda j, bi: (lyr, jnp.where(j < n_ctx_tiles, ctx_row, bi), 0, 0))

    tab = pl.BlockSpec((TM, LANES), lambda j, bi: (j, 0))
    widths = (512, 512, 512, 256, 256, 512, 512, 256, 512)
    in_specs = [tok(d), mod_spec(layer), lay((1, d)), lay(w_in_p.shape[1:]), lay((1, MLA_Q_RANK)),
                lay(w_uq_p.shape[1:]), lay((1, MLA_KV_RANK)), lay(w_ukv_p.shape[1:])] + [tab] * 9
    out_specs = [tok(w) for w in widths]
    out_shape = [jax.ShapeDtypeStruct((b, nt, w), BF16) for w in widths]
    args = [xs, mods, g_attn, w_in_p, q_norm, w_uq_p, kv_norm, w_ukv_p, *tables]
    scratch = []
    if moe is not None:
        y_rows, slots, rw = moe
        nch = d // LANES

        def next_tile(j, bi):
            wrap = bi + 1 == b
            nj = jnp.minimum(jnp.where(wrap, j + 1, j), tiles - 1)
            return (jnp.where(wrap, 0, bi + 1) * tiles + nj, 0, 0)

        slot_block = (1, 1, TOP_K * TM)
        in_specs = [pl.BlockSpec(slot_block, lambda j, bi: (bi * tiles + j, 0, 0), memory_space=pltpu.SMEM),
                    pl.BlockSpec(slot_block, next_tile, memory_space=pltpu.SMEM),
                    pl.BlockSpec(memory_space=pl.ANY),
                    pl.BlockSpec((TM, LANES), lambda j, bi: (bi * tiles + j, 0)),
                    mod_spec(layer - 1)] + in_specs
        slot_tiles = _slot_tiles(slots)
        args = [slot_tiles, slot_tiles, y_rows, rw, mods] + args
        out_specs = out_specs + [tok(d)]
        out_shape = out_shape + [jax.ShapeDtypeStruct((b, nt, d), F32)]
        scratch = [pltpu.VMEM((2, TOP_K * TM * nch, LANES), F32), pltpu.SemaphoreType.DMA((2,))]
    outs = pl.pallas_call(
        functools.partial(_pre_attn_kernel, moe is not None),
        grid=(tiles, b),
        in_specs=in_specs,
        out_specs=out_specs,
        out_shape=out_shape,
        scratch_shapes=scratch,
        compiler_params=_params("arbitrary", "arbitrary"),
        name="pre_attn",
    )(*args)
    return (outs[-1], outs[:-1]) if moe is not None else (xs, outs)


def _softmax_pv(s, v):
    p = jnp.exp2(s - jnp.max(s, axis=-1, keepdims=True))
    ov = jnp.dot(p.astype(BF16), v, preferred_element_type=F32)
    return ov[:, :LANES] / ov[:, LANES:]


def _dense_steps(n_ctx, nt, run):
    j = pl.program_id(1)

    @pl.when(j == 0)
    def _():
        run(0, n_ctx, n_ctx)

    @pl.when(j > 0)
    def _():
        run(pl.multiple_of(n_ctx + (j - 1) * TQ, TM), TQ, nt)


def _mla_attn_kernel(n_ctx, q_ref, k_ref, v_ref, o_ref):
    def run(row0, rows, nk):
        lane = _lane_iota((rows, LANES))
        outs = []
        for hd in range(MLA_HEADS):
            sl = slice(hd * LANES, (hd + 1) * LANES)
            vs = slice((hd // 2) * 2 * LANES, (hd // 2 + 1) * 2 * LANES)
            s = _qk(q_ref[0, pl.ds(row0, rows), sl], k_ref[0, :nk, sl])
            outs.append(_softmax_pv(s, v_ref[0, :nk, vs]))
        for g in range(2):
            o_ref[0, pl.ds(row0, rows), g * LANES:(g + 1) * LANES] = jnp.where(
                lane < MLA_V, outs[2 * g], outs[2 * g + 1]).astype(BF16)

    _dense_steps(n_ctx, k_ref.shape[1], run)


def _diff_attn_kernel(n_ctx, lambda_init, q_ref, k_ref, v_ref, lam_ref, g_ref, seg_ref, o_ref):
    lam = lam_ref[0]
    lam_full = (jnp.exp(jnp.sum(lam[0:1] * lam[1:2], axis=-1, keepdims=True))
                - jnp.exp(jnp.sum(lam[2:3] * lam[3:4], axis=-1, keepdims=True)) + lambda_init)

    def run(row0, rows, nk):
        lane = _lane_iota((rows, LANES))
        heads = []
        for hd in range(DIFF_HEADS):
            sl = slice((hd // 2) * LANES, (hd // 2 + 1) * LANES)
            q = q_ref[0, pl.ds(row0, rows), sl]
            k = k_ref[0, :nk, sl]
            v = v_ref[0, :nk, (hd // 2) * 2 * LANES:(hd // 2 + 1) * 2 * LANES]
            a = []
            for comp in range(2):
                seg = (hd % 2) * 2 + comp
                qm = jnp.where((lane >= seg * DIFF_QK) & (lane < (seg + 1) * DIFF_QK), q, jnp.zeros_like(q))
                a.append(_softmax_pv(_qk(qm, k), v))
            heads.append(a[0] - lam_full * a[1])
        for g in range(2):
            o = jnp.where(lane < DIFF_V, heads[2 * g], heads[2 * g + 1])
            ms = jnp.dot(o * o, seg_ref[...], precision=HIGHEST, preferred_element_type=F32) * (1.0 / DIFF_V)
            o = o * lax.rsqrt(ms + EPS) * g_ref[...] * (1.0 - lambda_init)
            o_ref[0, pl.ds(row0, rows), g * LANES:(g + 1) * LANES] = o.astype(BF16)

    _dense_steps(n_ctx, k_ref.shape[1], run)


def _swa_attn_kernel(n_ctx, q_ref, k_ref, v_ref, sink_ref, o_ref):
    j = pl.program_id(1)
    nt = k_ref.shape[1]
    band = TM + 2 * WINDOW
    lane = _lane_iota((TM, LANES))
    group_heads = SWA_HEADS // SWA_KV_HEADS

    def run(kv, k, v, allowed):
        for g in (2 * kv, 2 * kv + 1):
            q = q_ref[0, :, g * LANES:(g + 1) * LANES]
            outs = []
            for half in range(2):
                qm = jnp.where((lane >= half * SWA_DIM) & (lane < (half + 1) * SWA_DIM), q, jnp.zeros_like(q))
                s = _qk(qm, k)
                if allowed is not None:
                    s = jnp.where(allowed, s, NEG_INF)
                sink = sink_ref[0, 2 * g + half:2 * g + half + 1, 0:1] * LOG2E
                m = jnp.maximum(jnp.max(s, axis=-1, keepdims=True), sink)
                p = jnp.exp2(s - m)
                ov = jnp.dot(p.astype(BF16), v, preferred_element_type=F32)
                outs.append(ov[:, :LANES] / (ov[:, LANES:] + jnp.exp2(sink - m)))
            o_ref[0, :, g * LANES:(g + 1) * LANES] = jnp.where(lane < SWA_DIM, outs[0], outs[1]).astype(BF16)

    @pl.when(j < n_ctx // TM)
    def _():
        for kv in range(SWA_KV_HEADS):
            ks = slice(kv * LANES, (kv + 1) * LANES)
            vs = slice(2 * kv * LANES, (2 * kv + 2) * LANES)
            run(kv, k_ref[0, :n_ctx, ks], v_ref[0, :n_ctx, vs], None)

    @pl.when(j >= n_ctx // TM)
    def _():
        q0 = j * TM
        w0 = pl.multiple_of(jnp.clip(q0 - WINDOW, n_ctx, nt - band), WINDOW)
        q_pos = q0 + lax.broadcasted_iota(I32, (TM, n_ctx + band), 0)
        col = lax.broadcasted_iota(I32, (TM, n_ctx + band), 1)
        k_pos = w0 + col - n_ctx
        allowed = (col < n_ctx) | (jnp.abs(k_pos - q_pos) <= WINDOW)
        for kv in range(SWA_KV_HEADS):
            ks = slice(kv * LANES, (kv + 1) * LANES)
            vs = slice(2 * kv * LANES, (2 * kv + 2) * LANES)
            k = jnp.concatenate([k_ref[0, :n_ctx, ks], k_ref[0, pl.ds(w0, band), ks]], axis=0)
            v = jnp.concatenate([v_ref[0, :n_ctx, vs], v_ref[0, pl.ds(w0, band), vs]], axis=0)
            run(kv, k, v, allowed)


def _attention(kernel, q, k, v, extra, out_width, name, dense_steps=None):
    b, nt, _ = q.shape

    def whole(width):
        return pl.BlockSpec((1, nt, width), lambda bi, j: (bi, 0, 0))

    def tile(width):
        return pl.BlockSpec((1, TM, width), lambda bi, j: (bi, j, 0))

    q_spec, o_spec, steps = (tile, tile, nt // TM) if dense_steps is None else (whole, whole, dense_steps)
    extra_specs = [pl.BlockSpec(e.shape, lambda bi, j, nd=e.ndim: (0,) * nd) for e in extra]
    return pl.pallas_call(
        kernel,
        grid=(b, steps),
        in_specs=[q_spec(q.shape[2]), whole(k.shape[2]), whole(v.shape[2])] + extra_specs,
        out_specs=o_spec(out_width),
        out_shape=jax.ShapeDtypeStruct((b, nt, out_width), BF16),
        compiler_params=_params("arbitrary", "arbitrary"),
        name=name,
    )(q, k, v, *extra)


def _post_attn_kernel(cap, x_ref, mod_ref, mla_ref, diff_ref, swa_ref, wout_ref, g_ref, wr_ref, br_ref,
                      x1_ref, route_ref, rw_ref, cnt_ref, xs_ref, carry_ref, fbuf, slot_v, slot_s, dsem, ssem):
    step = pl.program_id(0) * pl.num_programs(1) + pl.program_id(1)
    n_steps = pl.num_programs(0) * pl.num_programs(1)
    cur = step % 2
    nch = fbuf.shape[1] // TM

    def scatter(buf):
        pltpu.make_async_copy(slot_v, slot_s.at[buf], ssem.at[buf]).wait()

        def body(i, carry):
            for u in range(GATHER_UNROLL):
                r = i * GATHER_UNROLL + u
                src = fbuf.at[buf, pl.ds(pl.multiple_of(r * nch, nch), nch)]
                for k in range(TOP_K):
                    dst = pl.multiple_of(slot_s[buf, r, k] * nch, nch)
                    pltpu.make_async_copy(src, xs_ref.at[pl.ds(dst, nch)], dsem.at[buf]).start()
            return carry

        lax.fori_loop(0, TM // GATHER_UNROLL, body, 0)

    def drain(buf):
        for _ in range(TOP_K):
            pltpu.make_async_copy(fbuf.at[buf], xs_ref.at[pl.ds(0, TM * nch)], dsem.at[buf]).wait()

    @pl.when(step == 0)
    def _():
        carry_ref[...] = jnp.zeros_like(carry_ref)

    @pl.when(step >= 2)
    def _():
        drain(cur)

    @pl.when(step >= 1)
    def _():
        scatter(1 - cur)

    x = x_ref[0]
    mod = mod_ref[0, 0]
    a = jnp.concatenate([mla_ref[0], diff_ref[0], swa_ref[0]], axis=-1)
    x1 = x + mod[2:3] * jnp.dot(a, wout_ref[0], preferred_element_type=F32)
    x1_ref[0] = x1
    f = _modulate(x1, g_ref[0], mod[3:4], mod[4:5])
    for c in range(nch):
        fbuf[cur, pl.ds(c, TM, stride=nch), :] = f[:, c * LANES:(c + 1) * LANES]

    logits = jnp.dot(f.astype(BF16), wr_ref[0], preferred_element_type=F32) + br_ref[0]
    lane = _lane_iota((TM, LANES))
    vals, hots = [], []
    for _ in range(TOP_K):
        m = jnp.max(logits, axis=-1, keepdims=True)
        idx = jnp.min(jnp.where(logits == m, lane, LANES), axis=-1, keepdims=True)
        hot = lane == idx
        logits = jnp.where(hot, -3e38, logits)
        vals.append(m)
        hots.append((idx, hot))
    es = [jnp.exp(v - vals[0]) for v in vals]
    denom = functools.reduce(jnp.add, es)
    sel = functools.reduce(jnp.add, [jnp.where(hot, 1.0, 0.0) for _, hot in hots])

    r_io = lax.broadcasted_iota(I32, (TM, TM), 0)
    c_io = lax.broadcasted_iota(I32, (TM, TM), 1)
    tril = jnp.where(c_io < r_io, 1.0, 0.0).astype(BF16)
    rank = carry_ref[0:1, :] + jnp.dot(tril, sel.astype(BF16), preferred_element_type=F32)
    carry_ref[0:1, :] = carry_ref[0:1, :] + jnp.sum(sel, axis=0, keepdims=True)
    cnt_ref[...] = jnp.broadcast_to(carry_ref[0:1, :], cnt_ref.shape)

    route = jnp.zeros((TM, LANES), I32)
    slots = jnp.zeros((TM, LANES), I32)
    rw = jnp.zeros((TM, LANES), F32)
    for k, (idx, hot) in enumerate(hots):
        rk = jnp.sum(jnp.where(hot, rank, 0.0), axis=-1, keepdims=True).astype(I32)
        route = jnp.where(lane == k, idx, route)
        route = jnp.where(lane == TOP_K + k, rk, route)
        slots = jnp.where(lane == k, idx * cap + rk, slots)
        rw = jnp.where(lane == k, es[k] / denom, rw)
    route_ref[...] = route
    rw_ref[...] = rw
    slot_v[...] = slots
    pltpu.make_async_copy(slot_v, slot_s.at[cur], ssem.at[cur]).start()

    @pl.when(step == n_steps - 1)
    def _():
        scatter(cur)
        drain(cur)

        @pl.when(n_steps >= 2)
        def _():
            drain(1 - cur)


def _post_attn(layer, xs, mods, mla_o, diff_o, swa_o, w_out_b, g_ffn, w_router_p, b_router_p, n_ctx, n_exp, cap):
    b, nt, d = xs.shape
    t = b * nt
    nch = d // LANES
    n_ctx_tiles = n_ctx // TM
    tiles = nt // TM

    def tok(w):
        return pl.BlockSpec((1, TM, w), lambda bi, j: (bi, j, 0))

    def lay(shape):
        return pl.BlockSpec((1,) + shape, lambda bi, j: (layer,) + (0,) * len(shape))

    def flat(rows, w):
        return pl.BlockSpec((rows, w), lambda bi, j: (bi * tiles + j, 0))

    mod_spec = pl.BlockSpec((1, 1, N_MOD, d), lambda bi, j: (layer, jnp.where(j < n_ctx_tiles, b, bi), 0, 0))
    return pl.pallas_call(
        functools.partial(_post_attn_kernel, cap),
        grid=(b, tiles),
        in_specs=[tok(d), mod_spec, tok(256), tok(256), tok(512), lay(w_out_b.shape[1:]), lay((1, d)),
                  lay((d, LANES)), lay((1, LANES))],
        out_specs=[tok(d), flat(TM, LANES), flat(TM, LANES),
                   pl.BlockSpec((SUBLANES, LANES), lambda bi, j: (0, 0)), pl.BlockSpec(memory_space=pl.ANY)],
        out_shape=[jax.ShapeDtypeStruct((b, nt, d), F32),
                   jax.ShapeDtypeStruct((t, LANES), I32), jax.ShapeDtypeStruct((t, LANES), F32),
                   jax.ShapeDtypeStruct((SUBLANES, LANES), F32),
                   jax.ShapeDtypeStruct((n_exp * cap * nch, LANES), F32)],
        scratch_shapes=[pltpu.VMEM((SUBLANES, LANES), F32), pltpu.VMEM((2, TM * nch, LANES), F32),
                        pltpu.VMEM((TM, LANES), I32), pltpu.SMEM((2, TM, LANES), I32),
                        pltpu.SemaphoreType.DMA((2,)), pltpu.SemaphoreType.DMA((2,))],
        compiler_params=_params("arbitrary", "arbitrary"),
        name="post_attn",
    )(xs, mods, mla_o, diff_o, swa_o, w_out_b, g_ffn, w_router_p, b_router_p)


def _pad_zero_kernel(n_exp, nch, start_ref, len_ref, xs_in_ref, xs_ref, zbuf, sem):
    del xs_in_ref
    zbuf[...] = jnp.zeros_like(zbuf)
    bits = (TMOE - 1).bit_length()
    for e in range(n_exp):
        for bit in range(bits):
            size = 1 << bit

            @pl.when(((len_ref[e] >> bit) & 1) == 1)
            def _():
                below = len_ref[e] & (size - 1)
                dst = pl.multiple_of((start_ref[e] + below) * nch, nch)
                cp = pltpu.make_async_copy(zbuf.at[pl.ds(0, size * nch)], xs_ref.at[pl.ds(dst, size * nch)], sem)
                cp.start()
                cp.wait()


def _pad_zero(xs_rows, pad_start, pad_len, n_exp, nch):
    return pl.pallas_call(
        functools.partial(_pad_zero_kernel, n_exp, nch),
        grid_spec=pltpu.PrefetchScalarGridSpec(
            num_scalar_prefetch=2,
            grid=(1,),
            in_specs=[pl.BlockSpec(memory_space=pl.ANY)],
            out_specs=pl.BlockSpec(memory_space=pl.ANY),
            scratch_shapes=[pltpu.VMEM((TMOE // 2 * nch, LANES), F32), pltpu.SemaphoreType.DMA]),
        out_shape=jax.ShapeDtypeStruct(xs_rows.shape, F32),
        input_output_aliases={2: 0},
        compiler_params=_params("arbitrary"),
        name="pad_zero",
    )(pad_start, pad_len, xs_rows)


def _moe_kernel(nch, te_ref, tb_ref, nu_ref, x_ref, wg_ref, bg_ref, wu_ref, bu_ref, wd_ref, bd_ref, y_ref,
                wg_b, wu_b, wd_b):
    del tb_ref
    i = pl.program_id(0)

    @pl.when((i == 0) | (te_ref[i] != te_ref[jnp.maximum(i - 1, 0)]))
    def _():
        wg_b[...] = wg_ref[0].astype(BF16)
        wu_b[...] = wu_ref[0].astype(BF16)
        wd_b[...] = wd_ref[0].astype(BF16)

    @pl.when(i < nu_ref[0])
    def _():
        x = jnp.concatenate([x_ref[pl.ds(c, TMOE, stride=nch), :] for c in range(nch)], axis=-1).astype(BF16)
        gate = jnp.dot(x, wg_b[...], preferred_element_type=F32) + bg_ref[0]
        up = jnp.dot(x, wu_b[...], preferred_element_type=F32) + bu_ref[0]
        gate = jnp.minimum(gate, SWIGLU_LIMIT)
        up = jnp.clip(up, -SWIGLU_LIMIT, SWIGLU_LIMIT)
        act = gate * (1.0 / (1.0 + jnp.exp(-SWIGLU_ALPHA * gate))) * (up + 1.0)
        y = jnp.dot(act.astype(BF16), wd_b[...], preferred_element_type=F32) + bd_ref[0]
        for c in range(nch):
            y_ref[pl.ds(c, TMOE, stride=nch), :] = y[:, c * LANES:(c + 1) * LANES]


def _moe(layer, xs_rows, tile_e, tile_blk, n_used, n_tiles, wg, bg, wu, bu, wd, bd, n_exp):
    d, f = wg.shape[1], wg.shape[2]
    nch = d // LANES

    def rows(i, te, tb, nu):
        return (tb[i], 0)

    def exp(i, te, tb, nu):
        return (layer * n_exp + te[i], 0, 0)

    return pl.pallas_call(
        functools.partial(_moe_kernel, nch),
        grid_spec=pltpu.PrefetchScalarGridSpec(
            num_scalar_prefetch=3,
            grid=(n_tiles,),
            in_specs=[pl.BlockSpec((TMOE * nch, LANES), rows),
                      pl.BlockSpec((1, d, f), exp), pl.BlockSpec((1, 1, f), exp),
                      pl.BlockSpec((1, d, f), exp), pl.BlockSpec((1, 1, f), exp),
                      pl.BlockSpec((1, f, d), exp), pl.BlockSpec((1, 1, d), exp)],
            out_specs=pl.BlockSpec((TMOE * nch, LANES), rows),
            scratch_shapes=[pltpu.VMEM((d, f), BF16), pltpu.VMEM((d, f), BF16), pltpu.VMEM((f, d), BF16)]),
        out_shape=jax.ShapeDtypeStruct(xs_rows.shape, F32),
        compiler_params=_params("arbitrary"),
        name="moe",
    )(tile_e, tile_blk, n_used, xs_rows, wg, bg, wu, bu, wd, bd)


def _final_kernel(slot_ref, next_slot_ref, y_ref, rw_ref, mod_ref, x1_ref, g_ref, o_ref, ybuf, sem):
    step = pl.program_id(0) * pl.num_programs(1) + pl.program_id(1)
    n_steps = pl.num_programs(0) * pl.num_programs(1)
    x = x1_ref[0] + mod_ref[0, 0][5:6] * _moe_combine(step, n_steps, slot_ref, next_slot_ref, y_ref, rw_ref, ybuf, sem)
    o_ref[0] = _rms(x) * g_ref[...]


def _final_norm(layer, x1, y_rows, slots, rw, mods, g_final, n_ctx, s_len):
    b, nt, d = x1.shape
    nch = d // LANES
    tiles = nt // TM
    lat_tiles = s_len // TM
    off = n_ctx // TM

    def next_tile(bi, j):
        wrap = j + 1 == lat_tiles
        nb = jnp.minimum(jnp.where(wrap, bi + 1, bi), b - 1)
        return (nb * tiles + off + jnp.where(wrap, 0, j + 1), 0, 0)

    slot_block = (1, 1, TOP_K * TM)
    slot_tiles = _slot_tiles(slots)
    return pl.pallas_call(
        _final_kernel,
        grid=(b, lat_tiles),
        in_specs=[pl.BlockSpec(slot_block, lambda bi, j: (bi * tiles + off + j, 0, 0), memory_space=pltpu.SMEM),
                  pl.BlockSpec(slot_block, next_tile, memory_space=pltpu.SMEM),
                  pl.BlockSpec(memory_space=pl.ANY),
                  pl.BlockSpec((TM, LANES), lambda bi, j: (bi * tiles + off + j, 0)),
                  pl.BlockSpec((1, 1, N_MOD, d), lambda bi, j: (layer, bi, 0, 0)),
                  pl.BlockSpec((1, TM, d), lambda bi, j: (bi, j + off, 0)),
                  pl.BlockSpec((1, d), lambda bi, j: (0, 0))],
        out_specs=pl.BlockSpec((1, TM, d), lambda bi, j: (bi, j, 0)),
        out_shape=jax.ShapeDtypeStruct((b, s_len, d), F32),
        scratch_shapes=[pltpu.VMEM((2, TOP_K * TM * nch, LANES), F32), pltpu.SemaphoreType.DMA((2,))],
        compiler_params=_params("arbitrary", "arbitrary"),
        name="final_norm",
    )(slot_tiles, slot_tiles, y_rows, rw, mods, x1, g_final.reshape(1, d))


def _routing_plan(route, counts, n_exp, n_tiles, cap):
    slots = route[:, 0:TOP_K] * cap + route[:, TOP_K:2 * TOP_K]
    counts = counts[0, :n_exp].astype(I32)
    e_tiles = (counts + TMOE - 1) // TMOE
    ends = jnp.cumsum(e_tiles)
    n_used = ends[-1].reshape(1)
    tile_ids = jnp.minimum(jnp.arange(n_tiles, dtype=I32), n_used[0] - 1)
    tile_e = jnp.minimum(jnp.sum((tile_ids[:, None] >= ends[None, :]).astype(I32), axis=-1), n_exp - 1)
    first = jnp.sum(jnp.where(tile_e[:, None] == jnp.arange(n_exp, dtype=I32), (ends - e_tiles)[None, :], 0), axis=-1)
    tile_blk = tile_e * (cap // TMOE) + tile_ids - first
    pad_start = jnp.arange(n_exp, dtype=I32) * cap + counts
    pad_len = e_tiles * TMOE - counts
    return slots, tile_e, tile_blk, n_used, pad_start, pad_len


def kernel(x, c, ctx, c_ctx, w_ada, b_ada, g_attn, w_in, mla_q_norm, mla_w_uq, mla_kv_norm, mla_w_ukv,
           diff_lambda, diff_subln, swa_sink, w_out, g_ffn, w_router, b_router, w_gate, b_gate, w_up, b_up,
           w_down, b_down, g_final):
    b, s_len, d = x.shape
    n_ctx = ctx.shape[1]
    n_layers = w_ada.shape[0]
    n_exp = w_router.shape[2]
    nt = n_ctx + s_len
    t = b * nt
    assert d % LANES == 0 and n_ctx % TM == 0 and s_len % TQ == 0 and t % TMOE == 0 and s_len % GRID_W == 0
    assert n_exp <= LANES and s_len >= TM + 2 * WINDOW

    w_in_p = _gather_columns(w_in, _in_proj_columns()).astype(BF16)
    w_uq_p = _gather_columns(mla_w_uq, _uq_columns()).astype(BF16)
    w_ukv_p = _gather_columns(mla_w_ukv, _ukv_columns()).astype(BF16)
    w_out_b = w_out.astype(BF16)
    w_router_p = jnp.pad(w_router, ((0, 0), (0, 0), (0, LANES - n_exp))).astype(BF16)
    b_router_p = jnp.pad(b_router, ((0, 0), (0, LANES - n_exp)), constant_values=NEG_INF).reshape(n_layers, 1, LANES)
    wg = w_gate.reshape(n_layers * n_exp, d, -1)
    wu = w_up.reshape(n_layers * n_exp, d, -1)
    wd = w_down.reshape(n_layers * n_exp, -1, d)
    bg = b_gate.reshape(n_layers * n_exp, 1, -1)
    bu = b_up.reshape(n_layers * n_exp, 1, -1)
    bd = b_down.reshape(n_layers * n_exp, 1, d)
    lam_p = jnp.pad(diff_lambda, ((0, 0), (0, 0), (0, LANES - DIFF_QK)))
    subln_p = jnp.tile(diff_subln, (1, LANES // DIFF_V)).reshape(n_layers, 1, LANES)
    sink_p = jnp.broadcast_to(swa_sink[:, :, None], (n_layers, SWA_HEADS, LANES))
    seg = jnp.asarray((np.arange(LANES)[:, None] // DIFF_V == np.arange(LANES)[None, :] // DIFF_V), F32)
    tables = _rope_tables(n_ctx, s_len)

    ada_rows = -(-(b + 1) // SUBLANES) * SUBLANES
    cc = jnp.concatenate([c, c_ctx[None, :], jnp.zeros((ada_rows - b - 1, d), F32)], axis=0)
    mods = _ada(cc, w_ada, b_ada).reshape(n_layers, ada_rows, N_MOD, d)

    cap = t
    n_tiles = (t * TOP_K) // TMOE + n_exp
    xs = jnp.concatenate([ctx, x], axis=1)
    moe = None
    for layer in range(n_layers):
        lambda_init = 0.8 - 0.6 * math.exp(-0.3 * layer)
        xs, (mq, mk, mv, dq, dk, dv, sq, sk, sv) = _pre_attn(
            layer, xs, mods, g_attn.reshape(n_layers, 1, d), w_in_p, mla_q_norm.reshape(n_layers, 1, -1), w_uq_p,
            mla_kv_norm.reshape(n_layers, 1, -1), w_ukv_p, tables, n_ctx, moe)
        dense_steps = 1 + s_len // TQ
        mla_o = _attention(functools.partial(_mla_attn_kernel, n_ctx), mq, mk, mv, [], 256, "mla_attn", dense_steps)
        diff_o = _attention(functools.partial(_diff_attn_kernel, n_ctx, lambda_init), dq, dk, dv,
                            [lam_p[layer:layer + 1], subln_p[layer], seg], 256, "diff_attn", dense_steps)
        swa_o = _attention(functools.partial(_swa_attn_kernel, n_ctx), sq, sk, sv, [sink_p[layer:layer + 1]],
                           512, "swa_attn")
        xs, route, rw, counts, xs_rows = _post_attn(
            layer, xs, mods, mla_o, diff_o, swa_o, w_out_b, g_ffn.reshape(n_layers, 1, d), w_router_p, b_router_p,
            n_ctx, n_exp, cap)
        slots, tile_e, tile_blk, n_used, pad_start, pad_len = _routing_plan(route, counts, n_exp, n_tiles, cap)
        xs_rows = _pad_zero(xs_rows, pad_start, pad_len, n_exp, d // LANES)
        moe = (_moe(layer, xs_rows, tile_e, tile_blk, n_used, n_tiles, wg, bg, wu, bu, wd, bd, n_exp), slots, rw)
    return _final_norm(n_layers - 1, xs, *moe, mods, g_final, n_ctx, s_len)
```

```python
import functools
import math

import jax
import jax.numpy as jnp
import numpy as np
from jax import lax
from jax.experimental import pallas as pl
from jax.experimental.pallas import tpu as pltpu

F32 = jnp.float32
BF16 = jnp.bfloat16
I32 = jnp.int32
HIGHEST = lax.Precision.HIGHEST
LOG2E = math.log2(math.e)

LANES = 128
SUBLANES = 8
VMEM_LIMIT = 56 * 1024 * 1024

GRID_W = 64
ROPE_THETA = 10000.0
EPS = 1e-6
NEG_INF = -1e30
N_MOD = 6

MLA_HEADS, MLA_Q_RANK, MLA_KV_RANK, MLA_NOPE, MLA_ROPE, MLA_V = 4, 256, 128, 64, 32, 64
DIFF_HEADS, DIFF_QK = 4, 32
DIFF_V = 2 * DIFF_QK
SWA_HEADS, SWA_KV_HEADS, SWA_DIM, WINDOW = 8, 2, 64, 128
TOP_K = 4
SWIGLU_LIMIT = 7.0
SWIGLU_ALPHA = 1.702

TM = 256
TQ = 512
TMOE = 512
GATHER_UNROLL = 8

G_CQ, G_CKV, G_KROPE, G_DQ, G_DK, G_DV, G_SQ, G_SK, G_SV, N_GROUPS = 0, 2, 3, 4, 6, 8, 10, 14, 16, 18


def _params(*sem):
    return pltpu.CompilerParams(dimension_semantics=sem, vmem_limit_bytes=VMEM_LIMIT)


def _in_proj_columns():
    src = -np.ones(N_GROUPS * LANES, np.int64)
    o_ckv = MLA_Q_RANK
    o_kr = o_ckv + MLA_KV_RANK
    o_dq = o_kr + MLA_ROPE
    o_dk = o_dq + DIFF_HEADS * 2 * DIFF_QK
    o_dv = o_dk + DIFF_HEADS * 2 * DIFF_QK
    o_sq = o_dv + DIFF_HEADS * DIFF_V
    o_sk = o_sq + SWA_HEADS * SWA_DIM
    o_sv = o_sk + SWA_KV_HEADS * SWA_DIM
    src[G_CQ * LANES:G_CQ * LANES + MLA_Q_RANK] = np.arange(MLA_Q_RANK)
    src[G_CKV * LANES:G_CKV * LANES + MLA_KV_RANK] = o_ckv + np.arange(MLA_KV_RANK)
    src[G_KROPE * LANES + MLA_NOPE:G_KROPE * LANES + MLA_NOPE + MLA_ROPE] = o_kr + np.arange(MLA_ROPE)
    src[G_DQ * LANES:G_DQ * LANES + 256] = o_dq + np.arange(256)
    src[G_DK * LANES:G_DK * LANES + 256] = o_dk + np.arange(256)
    src[G_DV * LANES:G_DV * LANES + 256] = o_dv + np.arange(256)
    src[G_SQ * LANES:G_SQ * LANES + 512] = o_sq + np.arange(512)
    for kv in range(SWA_KV_HEADS):
        for half in range(2):
            lo = half * SWA_DIM
            src[(G_SK + kv) * LANES + lo:(G_SK + kv) * LANES + lo + SWA_DIM] = o_sk + kv * SWA_DIM + np.arange(SWA_DIM)
            src[(G_SV + kv) * LANES + lo:(G_SV + kv) * LANES + lo + SWA_DIM] = o_sv + kv * SWA_DIM + np.arange(SWA_DIM)
    return src


def _gather_columns(w, src):
    cols = jnp.take(w, jnp.asarray(np.maximum(src, 0), I32), axis=-1)
    return jnp.where(jnp.asarray(src >= 0), cols, 0.0)


def _uq_columns():
    src = -np.ones(MLA_HEADS * LANES, np.int64)
    hd = MLA_NOPE + MLA_ROPE
    for h in range(MLA_HEADS):
        src[h * LANES:h * LANES + hd] = h * hd + np.arange(hd)
    return src


def _ukv_columns():
    src = -np.ones(MLA_HEADS * LANES + MLA_HEADS * MLA_V, np.int64)
    hd = MLA_NOPE + MLA_V
    for h in range(MLA_HEADS):
        src[h * LANES:h * LANES + MLA_NOPE] = h * hd + np.arange(MLA_NOPE)
        src[MLA_HEADS * LANES + h * MLA_V:MLA_HEADS * LANES + (h + 1) * MLA_V] = h * hd + MLA_NOPE + np.arange(MLA_V)
    return src


def _rope_tables(n_ctx, s_len):
    rows = s_len // GRID_W

    def axial(rot_dim):
        n_freq = rot_dim // 4
        inv_freq = ROPE_THETA ** (-jnp.arange(n_freq, dtype=F32) / n_freq)
        row_pos = jnp.repeat(jnp.arange(rows, dtype=F32), GRID_W)
        col_pos = jnp.tile(jnp.arange(GRID_W, dtype=F32), rows)
        ang = jnp.concatenate([row_pos[:, None] * inv_freq, col_pos[:, None] * inv_freq], axis=-1)
        return jnp.cos(ang), jnp.sin(ang)

    def expand(cos, sin, lane_rot):
        half = cos.shape[1]
        lane_rot = np.asarray(lane_rot)
        idx = np.maximum(lane_rot, 0) % half
        is_rot = lane_rot >= 0
        lo = is_rot & (lane_rot < half)
        hi = is_rot & (lane_rot >= half)
        c = jnp.where(jnp.asarray(is_rot), cos[:, idx], 1.0)
        s_lo = jnp.where(jnp.asarray(lo), -sin[:, idx], 0.0)
        s_hi = jnp.where(jnp.asarray(hi), sin[:, idx], 0.0)
        ident = [jnp.ones((n_ctx, LANES), F32), jnp.zeros((n_ctx, LANES), F32), jnp.zeros((n_ctx, LANES), F32)]
        return [jnp.concatenate([i, t], axis=0) for i, t in zip(ident, (c, s_lo, s_hi))]

    cos_r, sin_r = axial(MLA_ROPE)
    cos_w, sin_w = axial(SWA_DIM)
    lanes = np.arange(LANES)
    mla_rot = np.where((lanes >= MLA_NOPE) & (lanes < MLA_NOPE + MLA_ROPE), lanes - MLA_NOPE, -1)
    return (expand(cos_r, sin_r, mla_rot) + expand(cos_r, sin_r, lanes % DIFF_QK)
            + expand(cos_w, sin_w, lanes % SWA_DIM))


def _rms(x):
    return x * lax.rsqrt(jnp.mean(x * x, axis=-1, keepdims=True) + EPS)


def _modulate(x, g, shift, scale):
    return (_rms(x) * g) * (1.0 + scale) + shift


def _rope(v, c, s_lo, s_hi, half):
    return v * c + pltpu.roll(v, half, 1) * s_hi + pltpu.roll(v, LANES - half, 1) * s_lo


def _qk(q, k):
    return lax.dot_general(q, k, (((1,), (1,)), ((), ())), preferred_element_type=F32)


def _lane_iota(shape):
    return lax.broadcasted_iota(I32, shape, len(shape) - 1)


def _masked_rows(q, segs, width):
    lane = _lane_iota(q.shape)
    zero = jnp.zeros_like(q)
    return jnp.concatenate([jnp.where((lane >= sg * width) & (lane < (sg + 1) * width), q, zero) for sg in segs], axis=0)


def _ada_kernel(c_ref, w_ref, b_ref, o_ref):
    c = c_ref[...]
    a = c * (1.0 / (1.0 + jnp.exp(-c)))
    o_ref[0] = jnp.dot(a, w_ref[0], precision=HIGHEST, preferred_element_type=F32) + b_ref[0]


def _ada(cc, w_ada, b_ada):
    n_layers, d, n_out = w_ada.shape
    rows = cc.shape[0]
    tn = d
    return pl.pallas_call(
        _ada_kernel,
        grid=(n_layers, n_out // tn),
        in_specs=[pl.BlockSpec((rows, d), lambda l, n: (0, 0)),
                  pl.BlockSpec((1, d, tn), lambda l, n: (l, 0, n)),
                  pl.BlockSpec((1, 1, tn), lambda l, n: (l, 0, n))],
        out_specs=pl.BlockSpec((1, rows, tn), lambda l, n: (l, 0, n)),
        out_shape=jax.ShapeDtypeStruct((n_layers, rows, n_out), F32),
        compiler_params=_params("arbitrary", "arbitrary"),
        name="ada",
    )(cc, w_ada, b_ada.reshape(n_layers, 1, n_out))


def _moe_combine(step, n_steps, slot_ref, next_slot_ref, y_ref, rw_ref, ybuf, sem):
    nch = ybuf.shape[1] // (TOP_K * TM)
    cur = step % 2

    def start(slots, buf):
        def body(i, carry):
            for u in range(GATHER_UNROLL):
                r = i * GATHER_UNROLL + u
                for k in range(TOP_K):
                    src = pl.multiple_of(slots[0, 0, k * TM + r] * nch, nch)
                    dst = pl.multiple_of((k * TM + r) * nch, nch)
                    pltpu.make_async_copy(y_ref.at[pl.ds(src, nch)], ybuf.at[buf, pl.ds(dst, nch)], sem.at[buf]).start()
            return carry

        lax.fori_loop(0, TM // GATHER_UNROLL, body, 0)

    @pl.when(step == 0)
    def _():
        start(slot_ref, 0)

    @pl.when(step + 1 < n_steps)
    def _():
        start(next_slot_ref, 1 - cur)

    pltpu.make_async_copy(y_ref.at[pl.ds(0, TOP_K * TM * nch)], ybuf.at[cur], sem.at[cur]).wait()
    rw = rw_ref[...]
    acc = None
    for k in range(TOP_K):
        yk = jnp.concatenate([ybuf[cur, pl.ds(k * TM * nch + c, TM, stride=nch), :] for c in range(nch)], axis=-1)
        term = rw[:, k:k + 1] * yk
        acc = term if acc is None else acc + term
    return acc


def _pre_attn_kernel(has_moe, *refs):
    if has_moe:
        slot_ref, next_slot_ref, y_ref, rw_ref, pmod_ref = refs[:5]
        x2_ref, ybuf, sem = refs[-3:]
        refs = refs[5:-3]
    (x_ref, mod_ref, g_ref, win_ref, qn_ref, wuq_ref, kvn_ref, wukv_ref,
     mc_ref, ml_ref, mh_ref, dc_ref, dl_ref, dh_ref, wc_ref, wl_ref, wh_ref,
     mq_ref, mk_ref, mv_ref, dq_ref, dk_ref, dv_ref, sq_ref, sk_ref, sv_ref) = refs
    x = x_ref[0]
    if has_moe:
        step = pl.program_id(0) * pl.num_programs(1) + pl.program_id(1)
        n_steps = pl.num_programs(0) * pl.num_programs(1)
        x = x + pmod_ref[0, 0][5:6] * _moe_combine(step, n_steps, slot_ref, next_slot_ref, y_ref, rw_ref, ybuf, sem)
        x2_ref[0] = x
    mod = mod_ref[0, 0]
    h = _modulate(x, g_ref[0], mod[0:1], mod[1:2])
    p = jnp.dot(h.astype(BF16), win_ref[0], preferred_element_type=F32)

    def grp(g, n=1):
        return p[:, g * LANES:(g + n) * LANES]

    mla_scale = LOG2E * (MLA_NOPE + MLA_ROPE) ** -0.5
    diff_scale = LOG2E * DIFF_QK ** -0.5
    swa_scale = LOG2E * SWA_DIM ** -0.5
    mla_tab = (mc_ref[...], ml_ref[...], mh_ref[...])
    diff_tab = (dc_ref[...], dl_ref[...], dh_ref[...])
    swa_tab = (wc_ref[...], wl_ref[...], wh_ref[...])

    cq = (_rms(grp(G_CQ, 2)) * qn_ref[0]).astype(BF16)
    q = jnp.dot(cq, wuq_ref[0], preferred_element_type=F32)
    ckv = (_rms(grp(G_CKV)) * kvn_ref[0]).astype(BF16)
    kv = jnp.dot(ckv, wukv_ref[0], preferred_element_type=F32)
    k_rope = _rope(grp(G_KROPE), *mla_tab, MLA_ROPE // 2)
    for hd in range(MLA_HEADS):
        sl = slice(hd * LANES, (hd + 1) * LANES)
        mq_ref[0, :, sl] = (_rope(q[:, sl], *mla_tab, MLA_ROPE // 2) * mla_scale).astype(BF16)
        mk_ref[0, :, sl] = (kv[:, sl] + k_rope).astype(BF16)
    def store_values(ref, vals):
        for g in range(2):
            ref[0, :, 2 * g * LANES:(2 * g + 1) * LANES] = vals[:, g * LANES:(g + 1) * LANES].astype(BF16)
            ref[0, :, (2 * g + 1) * LANES:(2 * g + 2) * LANES] = jnp.ones((TM, LANES), BF16)

    store_values(mv_ref, kv[:, MLA_HEADS * LANES:])

    for g in range(2):
        sl = slice(g * LANES, (g + 1) * LANES)
        dq_ref[0, :, sl] = (_rope(grp(G_DQ + g), *diff_tab, DIFF_QK // 2) * diff_scale).astype(BF16)
        dk_ref[0, :, sl] = _rope(grp(G_DK + g), *diff_tab, DIFF_QK // 2).astype(BF16)
    store_values(dv_ref, grp(G_DV, 2))

    for g in range(4):
        sl = slice(g * LANES, (g + 1) * LANES)
        sq_ref[0, :, sl] = (_rope(grp(G_SQ + g), *swa_tab, SWA_DIM // 2) * swa_scale).astype(BF16)
    for g in range(2):
        sl = slice(g * LANES, (g + 1) * LANES)
        sk_ref[0, :, sl] = _rope(grp(G_SK + g), *swa_tab, SWA_DIM // 2).astype(BF16)
    store_values(sv_ref, grp(G_SV, 2))


def _slot_tiles(slots):
    t = slots.shape[0]
    return slots.reshape(t // TM, TM, TOP_K).transpose(0, 2, 1).reshape(t // TM, 1, TOP_K * TM)


def _pre_attn(layer, xs, mods, g_attn, w_in_p, q_norm, w_uq_p, kv_norm, w_ukv_p, tables, n_ctx, moe=None):
    b, nt, d = xs.shape
    tiles = nt // TM
    n_ctx_tiles = n_ctx // TM
    ctx_row = b

    def tok(w):
        return pl.BlockSpec((1, TM, w), lambda j, bi: (bi, j, 0))

    def lay(shape):
        return pl.BlockSpec((1,) + shape, lambda j, bi: (layer,) + (0,) * len(shape))

    def mod_spec(lyr):
        return pl.BlockSpec((1, 1, N_MOD, d), lambda j, bi: (lyr, jnp.where(j < n_ctx_tiles, ctx_row, bi), 0, 0))

    tab = pl.BlockSpec((TM, LANES), lambda j, bi: (j, 0))
    widths = (512, 512, 512, 256, 256, 512, 512, 256, 512)
    in_specs = [tok(d), mod_spec(layer), lay((1, d)), lay(w_in_p.shape[1:]), lay((1, MLA_Q_RANK)),
                lay(w_uq_p.shape[1:]), lay((1, MLA_KV_RANK)), lay(w_ukv_p.shape[1:])] + [tab] * 9
    out_specs = [tok(w) for w in widths]
    out_shape = [jax.ShapeDtypeStruct((b, nt, w), BF16) for w in widths]
    args = [xs, mods, g_attn, w_in_p, q_norm, w_uq_p, kv_norm, w_ukv_p, *tables]
    scratch = []
    if moe is not None:
        y_rows, slots, rw = moe
        nch = d // LANES

        def next_tile(j, bi):
            wrap = bi + 1 == b
            nj = jnp.minimum(jnp.where(wrap, j + 1, j), tiles - 1)
            return (jnp.where(wrap, 0, bi + 1) * tiles + nj, 0, 0)

        slot_block = (1, 1, TOP_K * TM)
        in_specs = [pl.BlockSpec(slot_block, lambda j, bi: (bi * tiles + j, 0, 0), memory_space=pltpu.SMEM),
                    pl.BlockSpec(slot_block, next_tile, memory_space=pltpu.SMEM),
                    pl.BlockSpec(memory_space=pl.ANY),
                    pl.BlockSpec((TM, LANES), lambda j, bi: (bi * tiles + j, 0)),
                    mod_spec(layer - 1)] + in_specs
        slot_tiles = _slot_tiles(slots)
        args = [slot_tiles, slot_tiles, y_rows, rw, mods] + args
        out_specs = out_specs + [tok(d)]
        out_shape = out_shape + [jax.ShapeDtypeStruct((b, nt, d), F32)]
        scratch = [pltpu.VMEM((2, TOP_K * TM * nch, LANES), F32), pltpu.SemaphoreType.DMA((2,))]
    outs = pl.pallas_call(
        functools.partial(_pre_attn_kernel, moe is not None),
        grid=(tiles, b),
        in_specs=in_specs,
        out_specs=out_specs,
        out_shape=out_shape,
        scratch_shapes=scratch,
        compiler_params=_params("arbitrary", "arbitrary"),
        name="pre_attn",
    )(*args)
    return (outs[-1], outs[:-1]) if moe is not None else (xs, outs)


def _softmax_pv(s, v):
    p = jnp.exp2(s - jnp.max(s, axis=-1, keepdims=True))
    ov = jnp.dot(p.astype(BF16), v, preferred_element_type=F32)
    return ov[:, :LANES] / ov[:, LANES:]


def _dense_steps(n_ctx, nt, run):
    j = pl.program_id(1)

    @pl.when(j == 0)
    def _():
        run(0, n_ctx, n_ctx)

    @pl.when(j > 0)
    def _():
        run(pl.multiple_of(n_ctx + (j - 1) * TQ, TM), TQ, nt)


def _mla_attn_kernel(n_ctx, q_ref, k_ref, v_ref, o_ref):
    def run(row0, rows, nk):
        lane = _lane_iota((rows, LANES))
        outs = []
        for hd in range(MLA_HEADS):
            sl = slice(hd * LANES, (hd + 1) * LANES)
            vs = slice((hd // 2) * 2 * LANES, (hd // 2 + 1) * 2 * LANES)
            s = _qk(q_ref[0, pl.ds(row0, rows), sl], k_ref[0, :nk, sl])
            outs.append(_softmax_pv(s, v_ref[0, :nk, vs]))
        for g in range(2):
            o_ref[0, pl.ds(row0, rows), g * LANES:(g + 1) * LANES] = jnp.where(
                lane < MLA_V, outs[2 * g], outs[2 * g + 1]).astype(BF16)

    _dense_steps(n_ctx, k_ref.shape[1], run)


def _diff_attn_kernel(n_ctx, lambda_init, q_ref, k_ref, v_ref, lam_ref, g_ref, seg_ref, o_ref):
    lam = lam_ref[0]
    lam_full = (jnp.exp(jnp.sum(lam[0:1] * lam[1:2], axis=-1, keepdims=True))
                - jnp.exp(jnp.sum(lam[2:3] * lam[3:4], axis=-1, keepdims=True)) + lambda_init)

    def run(row0, rows, nk):
        lane = _lane_iota((rows, LANES))
        heads = []
        for hd in range(DIFF_HEADS):
            sl = slice((hd // 2) * LANES, (hd // 2 + 1) * LANES)
            q = q_ref[0, pl.ds(row0, rows), sl]
            k = k_ref[0, :nk, sl]
            v = v_ref[0, :nk, (hd // 2) * 2 * LANES:(hd // 2 + 1) * 2 * LANES]
            a = []
            for comp in range(2):
                seg = (hd % 2) * 2 + comp
                qm = jnp.where((lane >= seg * DIFF_QK) & (lane < (seg + 1) * DIFF_QK), q, jnp.zeros_like(q))
                a.append(_softmax_pv(_qk(qm, k), v))
            heads.append(a[0] - lam_full * a[1])
        for g in range(2):
            o = jnp.where(lane < DIFF_V, heads[2 * g], heads[2 * g + 1])
            ms = jnp.dot(o * o, seg_ref[...], precision=HIGHEST, preferred_element_type=F32) * (1.0 / DIFF_V)
            o = o * lax.rsqrt(ms + EPS) * g_ref[...] * (1.0 - lambda_init)
            o_ref[0, pl.ds(row0, rows), g * LANES:(g + 1) * LANES] = o.astype(BF16)

    _dense_steps(n_ctx, k_ref.shape[1], run)


def _swa_attn_kernel(n_ctx, q_ref, k_ref, v_ref, sink_ref, o_ref):
    j = pl.program_id(1)
    nt = k_ref.shape[1]
    band = TM + 2 * WINDOW
    lane = _lane_iota((TM, LANES))
    group_heads = SWA_HEADS // SWA_KV_HEADS

    def run(kv, k, v, allowed):
        for g in (2 * kv, 2 * kv + 1):
            q = q_ref[0, :, g * LANES:(g + 1) * LANES]
            outs = []
            for half in range(2):
                qm = jnp.where((lane >= half * SWA_DIM) & (lane < (half + 1) * SWA_DIM), q, jnp.zeros_like(q))
                s = _qk(qm, k)
                if allowed is not None:
                    s = jnp.where(allowed, s, NEG_INF)
                sink = sink_ref[0, 2 * g + half:2 * g + half + 1, 0:1] * LOG2E
                m = jnp.maximum(jnp.max(s, axis=-1, keepdims=True), sink)
                p = jnp.exp2(s - m)
                ov = jnp.dot(p.astype(BF16), v, preferred_element_type=F32)
                outs.append(ov[:, :LANES] / (ov[:, LANES:] + jnp.exp2(sink - m)))
            o_ref[0, :, g * LANES:(g + 1) * LANES] = jnp.where(lane < SWA_DIM, outs[0], outs[1]).astype(BF16)

    @pl.when(j < n_ctx // TM)
    def _():
        for kv in range(SWA_KV_HEADS):
            ks = slice(kv * LANES, (kv + 1) * LANES)
            vs = slice(2 * kv * LANES, (2 * kv + 2) * LANES)
            run(kv, k_ref[0, :n_ctx, ks], v_ref[0, :n_ctx, vs], None)

    @pl.when(j >= n_ctx // TM)
    def _():
        q0 = j * TM
        w0 = pl.multiple_of(jnp.clip(q0 - WINDOW, n_ctx, nt - band), WINDOW)
        q_pos = q0 + lax.broadcasted_iota(I32, (TM, n_ctx + band), 0)
        col = lax.broadcasted_iota(I32, (TM, n_ctx + band), 1)
        k_pos = w0 + col - n_ctx
        allowed = (col < n_ctx) | (jnp.abs(k_pos - q_pos) <= WINDOW)
        for kv in range(SWA_KV_HEADS):
            ks = slice(kv * LANES, (kv + 1) * LANES)
            vs = slice(2 * kv * LANES, (2 * kv + 2) * LANES)
            k = jnp.concatenate([k_ref[0, :n_ctx, ks], k_ref[0, pl.ds(w0, band), ks]], axis=0)
            v = jnp.concatenate([v_ref[0, :n_ctx, vs], v_ref[0, pl.ds(w0, band), vs]], axis=0)
            run(kv, k, v, allowed)


def _attention(kernel, q, k, v, extra, out_width, name, dense_steps=None):
    b, nt, _ = q.shape

    def whole(width):
        return pl.BlockSpec((1, nt, width), lambda bi, j: (bi, 0, 0))

    def tile(width):
        return pl.BlockSpec((1, TM, width), lambda bi, j: (bi, j, 0))

    q_spec, o_spec, steps = (tile, tile, nt // TM) if dense_steps is None else (whole, whole, dense_steps)
    extra_specs = [pl.BlockSpec(e.shape, lambda bi, j, nd=e.ndim: (0,) * nd) for e in extra]
    return pl.pallas_call(
        kernel,
        grid=(b, steps),
        in_specs=[q_spec(q.shape[2]), whole(k.shape[2]), whole(v.shape[2])] + extra_specs,
        out_specs=o_spec(out_width),
        out_shape=jax.ShapeDtypeStruct((b, nt, out_width), BF16),
        compiler_params=_params("arbitrary", "arbitrary"),
        name=name,
    )(q, k, v, *extra)


def _post_attn_kernel(cap, x_ref, mod_ref, mla_ref, diff_ref, swa_ref, wout_ref, g_ref, wr_ref, br_ref,
                      x1_ref, route_ref, rw_ref, cnt_ref, xs_ref, carry_ref, fbuf, slot_v, slot_s, dsem, ssem):
    step = pl.program_id(0) * pl.num_programs(1) + pl.program_id(1)
    n_steps = pl.num_programs(0) * pl.num_programs(1)
    cur = step % 2
    nch = fbuf.shape[1] // TM

    def scatter(buf):
        pltpu.make_async_copy(slot_v, slot_s.at[buf], ssem.at[buf]).wait()

        def body(i, carry):
            for u in range(GATHER_UNROLL):
                r = i * GATHER_UNROLL + u
                src = fbuf.at[buf, pl.ds(pl.multiple_of(r * nch, nch), nch)]
                for k in range(TOP_K):
                    dst = pl.multiple_of(slot_s[buf, k, r] * nch, nch)
                    pltpu.make_async_copy(src, xs_ref.at[pl.ds(dst, nch)], dsem.at[buf]).start()
            return carry

        lax.fori_loop(0, TM // GATHER_UNROLL, body, 0)

    def drain(buf):
        for _ in range(TOP_K):
            pltpu.make_async_copy(fbuf.at[buf], xs_ref.at[pl.ds(0, TM * nch)], dsem.at[buf]).wait()

    @pl.when(step == 0)
    def _():
        carry_ref[...] = jnp.zeros_like(carry_ref)

    @pl.when(step >= 2)
    def _():
        drain(cur)

    @pl.when(step >= 1)
    def _():
        scatter(1 - cur)

    x = x_ref[0]
    mod = mod_ref[0, 0]
    a = jnp.concatenate([mla_ref[0], diff_ref[0], swa_ref[0]], axis=-1)
    x1 = x + mod[2:3] * jnp.dot(a, wout_ref[0], preferred_element_type=F32)
    x1_ref[0] = x1
    f = _modulate(x1, g_ref[0], mod[3:4], mod[4:5])
    for c in range(nch):
        fbuf[cur, pl.ds(c, TM, stride=nch), :] = f[:, c * LANES:(c + 1) * LANES]

    logits = jnp.dot(f.astype(BF16), wr_ref[0], preferred_element_type=F32) + br_ref[0]
    lane = _lane_iota((TM, LANES))
    vals, hots = [], []
    for _ in range(TOP_K):
        m = jnp.max(logits, axis=-1, keepdims=True)
        idx = jnp.min(jnp.where(logits == m, lane, LANES), axis=-1, keepdims=True)
        hot = lane == idx
        logits = jnp.where(hot, -3e38, logits)
        vals.append(m)
        hots.append((idx, hot))
    es = [jnp.exp(v - vals[0]) for v in vals]
    denom = functools.reduce(jnp.add, es)
    sel = functools.reduce(jnp.add, [jnp.where(hot, 1.0, 0.0) for _, hot in hots])

    r_io = lax.broadcasted_iota(I32, (TM, TM), 0)
    c_io = lax.broadcasted_iota(I32, (TM, TM), 1)
    tril = jnp.where(c_io < r_io, 1.0, 0.0).astype(BF16)
    rank = carry_ref[0:1, :] + jnp.dot(tril, sel.astype(BF16), preferred_element_type=F32)
    carry_ref[0:1, :] = carry_ref[0:1, :] + jnp.sum(sel, axis=0, keepdims=True)
    cnt_ref[...] = jnp.broadcast_to(carry_ref[0:1, :], cnt_ref.shape)

    route = jnp.zeros((TM, LANES), I32)
    slots = jnp.zeros((TM, LANES), I32)
    rw = jnp.zeros((TM, LANES), F32)
    for k, (idx, hot) in enumerate(hots):
        rk = jnp.sum(jnp.where(hot, rank, 0.0), axis=-1, keepdims=True).astype(I32)
        route = jnp.where(lane == k, idx, route)
        route = jnp.where(lane == TOP_K + k, rk, route)
        slots = jnp.where(lane == k, idx * cap + rk, slots)
        rw = jnp.where(lane == k, es[k] / denom, rw)
    route_ref[...] = route
    rw_ref[...] = rw
    slot_v[...] = slots.T[0:SUBLANES, :]
    pltpu.make_async_copy(slot_v, slot_s.at[cur], ssem.at[cur]).start()

    @pl.when(step == n_steps - 1)
    def _():
        scatter(cur)
        drain(cur)

        @pl.when(n_steps >= 2)
        def _():
            drain(1 - cur)


def _post_attn(layer, xs, mods, mla_o, diff_o, swa_o, w_out_b, g_ffn, w_router_p, b_router_p, n_ctx, n_exp, cap):
    b, nt, d = xs.shape
    t = b * nt
    nch = d // LANES
    n_ctx_tiles = n_ctx // TM
    tiles = nt // TM

    def tok(w):
        return pl.BlockSpec((1, TM, w), lambda bi, j: (bi, j, 0))

    def lay(shape):
        return pl.BlockSpec((1,) + shape, lambda bi, j: (layer,) + (0,) * len(shape))

    def flat(rows, w):
        return pl.BlockSpec((rows, w), lambda bi, j: (bi * tiles + j, 0))

    mod_spec = pl.BlockSpec((1, 1, N_MOD, d), lambda bi, j: (layer, jnp.where(j < n_ctx_tiles, b, bi), 0, 0))
    return pl.pallas_call(
        functools.partial(_post_attn_kernel, cap),
        grid=(b, tiles),
        in_specs=[tok(d), mod_spec, tok(256), tok(256), tok(512), lay(w_out_b.shape[1:]), lay((1, d)),
                  lay((d, LANES)), lay((1, LANES))],
        out_specs=[tok(d), flat(TM, LANES), flat(TM, LANES),
                   pl.BlockSpec((SUBLANES, LANES), lambda bi, j: (0, 0)), pl.BlockSpec(memory_space=pl.ANY)],
        out_shape=[jax.ShapeDtypeStruct((b, nt, d), F32),
                   jax.ShapeDtypeStruct((t, LANES), I32), jax.ShapeDtypeStruct((t, LANES), F32),
                   jax.ShapeDtypeStruct((SUBLANES, LANES), F32),
                   jax.ShapeDtypeStruct((n_exp * cap * nch, LANES), F32)],
        scratch_shapes=[pltpu.VMEM((SUBLANES, LANES), F32), pltpu.VMEM((2, TM * nch, LANES), F32),
                        pltpu.VMEM((SUBLANES, TM), I32), pltpu.SMEM((2, SUBLANES, TM), I32),
                        pltpu.SemaphoreType.DMA((2,)), pltpu.SemaphoreType.DMA((2,))],
        compiler_params=_params("arbitrary", "arbitrary"),
        name="post_attn",
    )(xs, mods, mla_o, diff_o, swa_o, w_out_b, g_ffn, w_router_p, b_router_p)


def _pad_zero_kernel(n_exp, nch, start_ref, len_ref, xs_in_ref, xs_ref, zbuf, sem):
    del xs_in_ref
    zbuf[...] = jnp.zeros_like(zbuf)
    bits = (TMOE - 1).bit_length()
    for wait in (False, True):
        for e in range(n_exp):
            for bit in range(bits):
                size = 1 << bit

                @pl.when(((len_ref[e] >> bit) & 1) == 1)
                def _():
                    below = len_ref[e] & (size - 1)
                    dst = pl.multiple_of((start_ref[e] + below) * nch, nch)
                    cp = pltpu.make_async_copy(zbuf.at[pl.ds(0, size * nch)], xs_ref.at[pl.ds(dst, size * nch)], sem)
                    cp.wait() if wait else cp.start()


def _pad_zero(xs_rows, pad_start, pad_len, n_exp, nch):
    return pl.pallas_call(
        functools.partial(_pad_zero_kernel, n_exp, nch),
        grid_spec=pltpu.PrefetchScalarGridSpec(
            num_scalar_prefetch=2,
            grid=(1,),
            in_specs=[pl.BlockSpec(memory_space=pl.ANY)],
            out_specs=pl.BlockSpec(memory_space=pl.ANY),
            scratch_shapes=[pltpu.VMEM((TMOE // 2 * nch, LANES), F32), pltpu.SemaphoreType.DMA]),
        out_shape=jax.ShapeDtypeStruct(xs_rows.shape, F32),
        input_output_aliases={2: 0},
        compiler_params=_params("arbitrary"),
        name="pad_zero",
    )(pad_start, pad_len, xs_rows)


def _moe_kernel(nch, te_ref, tb_ref, nu_ref, x_ref, wg_ref, bg_ref, wu_ref, bu_ref, wd_ref, bd_ref, y_ref,
                wg_b, wu_b, wd_b):
    del tb_ref
    i = pl.program_id(0)

    @pl.when((i == 0) | (te_ref[i] != te_ref[jnp.maximum(i - 1, 0)]))
    def _():
        wg_b[...] = wg_ref[0].astype(BF16)
        wu_b[...] = wu_ref[0].astype(BF16)
        wd_b[...] = wd_ref[0].astype(BF16)

    @pl.when(i < nu_ref[0])
    def _():
        x = jnp.concatenate([x_ref[pl.ds(c, TMOE, stride=nch), :] for c in range(nch)], axis=-1).astype(BF16)
        gate = jnp.dot(x, wg_b[...], preferred_element_type=F32) + bg_ref[0]
        up = jnp.dot(x, wu_b[...], preferred_element_type=F32) + bu_ref[0]
        gate = jnp.minimum(gate, SWIGLU_LIMIT)
        up = jnp.clip(up, -SWIGLU_LIMIT, SWIGLU_LIMIT)
        act = gate * (1.0 / (1.0 + jnp.exp(-SWIGLU_ALPHA * gate))) * (up + 1.0)
        y = jnp.dot(act.astype(BF16), wd_b[...], preferred_element_type=F32) + bd_ref[0]
        for c in range(nch):
            y_ref[pl.ds(c, TMOE, stride=nch), :] = y[:, c * LANES:(c + 1) * LANES]


def _moe(layer, xs_rows, tile_e, tile_blk, n_used, n_tiles, wg, bg, wu, bu, wd, bd, n_exp):
    d, f = wg.shape[1], wg.shape[2]
    nch = d // LANES

    def rows(i, te, tb, nu):
        return (tb[i], 0)

    def exp(i, te, tb, nu):
        return (layer * n_exp + te[i], 0, 0)

    return pl.pallas_call(
        functools.partial(_moe_kernel, nch),
        grid_spec=pltpu.PrefetchScalarGridSpec(
            num_scalar_prefetch=3,
            grid=(n_tiles,),
            in_specs=[pl.BlockSpec((TMOE * nch, LANES), rows),
                      pl.BlockSpec((1, d, f), exp), pl.BlockSpec((1, 1, f), exp),
                      pl.BlockSpec((1, d, f), exp), pl.BlockSpec((1, 1, f), exp),
                      pl.BlockSpec((1, f, d), exp), pl.BlockSpec((1, 1, d), exp)],
            out_specs=pl.BlockSpec((TMOE * nch, LANES), rows),
            scratch_shapes=[pltpu.VMEM((d, f), BF16), pltpu.VMEM((d, f), BF16), pltpu.VMEM((f, d), BF16)]),
        out_shape=jax.ShapeDtypeStruct(xs_rows.shape, F32),
        compiler_params=_params("arbitrary"),
        name="moe",
    )(tile_e, tile_blk, n_used, xs_rows, wg, bg, wu, bu, wd, bd)


def _final_kernel(slot_ref, next_slot_ref, y_ref, rw_ref, mod_ref, x1_ref, g_ref, o_ref, ybuf, sem):
    step = pl.program_id(0) * pl.num_programs(1) + pl.program_id(1)
    n_steps = pl.num_programs(0) * pl.num_programs(1)
    x = x1_ref[0] + mod_ref[0, 0][5:6] * _moe_combine(step, n_steps, slot_ref, next_slot_ref, y_ref, rw_ref, ybuf, sem)
    o_ref[0] = _rms(x) * g_ref[...]


def _final_norm(layer, x1, y_rows, slots, rw, mods, g_final, n_ctx, s_len):
    b, nt, d = x1.shape
    nch = d // LANES
    tiles = nt // TM
    lat_tiles = s_len // TM
    off = n_ctx // TM

    def next_tile(bi, j):
        wrap = j + 1 == lat_tiles
        nb = jnp.minimum(jnp.where(wrap, bi + 1, bi), b - 1)
        return (nb * tiles + off + jnp.where(wrap, 0, j + 1), 0, 0)

    slot_block = (1, 1, TOP_K * TM)
    slot_tiles = _slot_tiles(slots)
    return pl.pallas_call(
        _final_kernel,
        grid=(b, lat_tiles),
        in_specs=[pl.BlockSpec(slot_block, lambda bi, j: (bi * tiles + off + j, 0, 0), memory_space=pltpu.SMEM),
                  pl.BlockSpec(slot_block, next_tile, memory_space=pltpu.SMEM),
                  pl.BlockSpec(memory_space=pl.ANY),
                  pl.BlockSpec((TM, LANES), lambda bi, j: (bi * tiles + off + j, 0)),
                  pl.BlockSpec((1, 1, N_MOD, d), lambda bi, j: (layer, bi, 0, 0)),
                  pl.BlockSpec((1, TM, d), lambda bi, j: (bi, j + off, 0)),
                  pl.BlockSpec((1, d), lambda bi, j: (0, 0))],
        out_specs=pl.BlockSpec((1, TM, d), lambda bi, j: (bi, j, 0)),
        out_shape=jax.ShapeDtypeStruct((b, s_len, d), F32),
        scratch_shapes=[pltpu.VMEM((2, TOP_K * TM * nch, LANES), F32), pltpu.SemaphoreType.DMA((2,))],
        compiler_params=_params("arbitrary", "arbitrary"),
        name="final_norm",
    )(slot_tiles, slot_tiles, y_rows, rw, mods, x1, g_final.reshape(1, d))


def _routing_plan(route, counts, n_exp, n_tiles, cap):
    slots = route[:, 0:TOP_K] * cap + route[:, TOP_K:2 * TOP_K]
    counts = counts[0, :n_exp].astype(I32)
    e_tiles = (counts + TMOE - 1) // TMOE
    ends = jnp.cumsum(e_tiles)
    n_used = ends[-1].reshape(1)
    tile_ids = jnp.minimum(jnp.arange(n_tiles, dtype=I32), n_used[0] - 1)
    tile_e = jnp.minimum(jnp.sum((tile_ids[:, None] >= ends[None, :]).astype(I32), axis=-1), n_exp - 1)
    first = jnp.sum(jnp.where(tile_e[:, None] == jnp.arange(n_exp, dtype=I32), (ends - e_tiles)[None, :], 0), axis=-1)
    tile_blk = tile_e * (cap // TMOE) + tile_ids - first
    pad_start = jnp.arange(n_exp, dtype=I32) * cap + counts
    pad_len = e_tiles * TMOE - counts
    return slots, tile_e, tile_blk, n_used, pad_start, pad_len


def kernel(x, c, ctx, c_ctx, w_ada, b_ada, g_attn, w_in, mla_q_norm, mla_w_uq, mla_kv_norm, mla_w_ukv,
           diff_lambda, diff_subln, swa_sink, w_out, g_ffn, w_router, b_router, w_gate, b_gate, w_up, b_up,
           w_down, b_down, g_final):
    b, s_len, d = x.shape
    n_ctx = ctx.shape[1]
    n_layers = w_ada.shape[0]
    n_exp = w_router.shape[2]
    nt = n_ctx + s_len
    t = b * nt
    assert d % LANES == 0 and n_ctx % TM == 0 and s_len % TQ == 0 and t % TMOE == 0 and s_len % GRID_W == 0
    assert n_exp <= LANES and s_len >= TM + 2 * WINDOW

    w_in_p = _gather_columns(w_in, _in_proj_columns()).astype(BF16)
    w_uq_p = _gather_columns(mla_w_uq, _uq_columns()).astype(BF16)
    w_ukv_p = _gather_columns(mla_w_ukv, _ukv_columns()).astype(BF16)
    w_out_b = w_out.astype(BF16)
    w_router_p = jnp.pad(w_router, ((0, 0), (0, 0), (0, LANES - n_exp))).astype(BF16)
    b_router_p = jnp.pad(b_router, ((0, 0), (0, LANES - n_exp)), constant_values=NEG_INF).reshape(n_layers, 1, LANES)
    wg = w_gate.reshape(n_layers * n_exp, d, -1)
    wu = w_up.reshape(n_layers * n_exp, d, -1)
    wd = w_down.reshape(n_layers * n_exp, -1, d)
    bg = b_gate.reshape(n_layers * n_exp, 1, -1)
    bu = b_up.reshape(n_layers * n_exp, 1, -1)
    bd = b_down.reshape(n_layers * n_exp, 1, d)
    lam_p = jnp.pad(diff_lambda, ((0, 0), (0, 0), (0, LANES - DIFF_QK)))
    subln_p = jnp.tile(diff_subln, (1, LANES // DIFF_V)).reshape(n_layers, 1, LANES)
    sink_p = jnp.broadcast_to(swa_sink[:, :, None], (n_layers, SWA_HEADS, LANES))
    seg = jnp.asarray((np.arange(LANES)[:, None] // DIFF_V == np.arange(LANES)[None, :] // DIFF_V), F32)
    tables = _rope_tables(n_ctx, s_len)

    ada_rows = -(-(b + 1) // SUBLANES) * SUBLANES
    cc = jnp.concatenate([c, c_ctx[None, :], jnp.zeros((ada_rows - b - 1, d), F32)], axis=0)
    mods = _ada(cc, w_ada, b_ada).reshape(n_layers, ada_rows, N_MOD, d)

    cap = t
    n_tiles = (t * TOP_K) // TMOE + n_exp
    xs = jnp.concatenate([ctx, x], axis=1)
    moe = None
    for layer in range(n_layers):
        lambda_init = 0.8 - 0.6 * math.exp(-0.3 * layer)
        xs, (mq, mk, mv, dq, dk, dv, sq, sk, sv) = _pre_attn(
            layer, xs, mods, g_attn.reshape(n_layers, 1, d), w_in_p, mla_q_norm.reshape(n_layers, 1, -1), w_uq_p,
            mla_kv_norm.reshape(n_layers, 1, -1), w_ukv_p, tables, n_ctx, moe)
        dense_steps = 1 + s_len // TQ
        mla_o = _attention(functools.partial(_mla_attn_kernel, n_ctx), mq, mk, mv, [], 256, "mla_attn", dense_steps)
        diff_o = _attention(functools.partial(_diff_attn_kernel, n_ctx, lambda_init), dq, dk, dv,
                            [lam_p[layer:layer + 1], subln_p[layer], seg], 256, "diff_attn", dense_steps)
        swa_o = _attention(functools.partial(_swa_attn_kernel, n_ctx), sq, sk, sv, [sink_p[layer:layer + 1]],
                           512, "swa_attn")
        xs, route, rw, counts, xs_rows = _post_attn(
            layer, xs, mods, mla_o, diff_o, swa_o, w_out_b, g_ffn.reshape(n_layers, 1, d), w_router_p, b_router_p,
            n_ctx, n_exp, cap)
        slots, tile_e, tile_blk, n_used, pad_start, pad_len = _routing_plan(route, counts, n_exp, n_tiles, cap)
        xs_rows = _pad_zero(xs_rows, pad_start, pad_len, n_exp, d // LANES)
        moe = (_moe(layer, xs_rows, tile_e, tile_blk, n_used, n_tiles, wg, bg, wu, bu, wd, bd, n_exp), slots, rw)
    return _final_norm(n_layers - 1, xs, *moe, mods, g_final, n_ctx, s_len)
```

```python
import functools
import math

import jax
import jax.numpy as jnp
import numpy as np
from jax import lax
from jax.experimental import pallas as pl
from jax.experimental.pallas import tpu as pltpu

F32 = jnp.float32
BF16 = jnp.bfloat16
I32 = jnp.int32
HIGHEST = lax.Precision.HIGHEST
LOG2E = math.log2(math.e)

LANES = 128
SUBLANES = 8
VMEM_LIMIT = 56 * 1024 * 1024

GRID_W = 64
ROPE_THETA = 10000.0
EPS = 1e-6
NEG_INF = -1e30
N_MOD = 6

MLA_HEADS, MLA_Q_RANK, MLA_KV_RANK, MLA_NOPE, MLA_ROPE, MLA_V = 4, 256, 128, 64, 32, 64
DIFF_HEADS, DIFF_QK = 4, 32
DIFF_V = 2 * DIFF_QK
SWA_HEADS, SWA_KV_HEADS, SWA_DIM, WINDOW = 8, 2, 64, 128
TOP_K = 4
SWIGLU_LIMIT = 7.0
SWIGLU_ALPHA = 1.702

TM = 256
TQ = 512
TMOE = 512
GATHER_UNROLL = 8

G_CQ, G_CKV, G_KROPE, G_DQ, G_DK, G_DV, G_SQ, G_SK, G_SV, N_GROUPS = 0, 2, 3, 4, 6, 8, 10, 14, 16, 18


def _params(*sem):
    return pltpu.CompilerParams(dimension_semantics=sem, vmem_limit_bytes=VMEM_LIMIT)


def _in_proj_columns():
    src = -np.ones(N_GROUPS * LANES, np.int64)
    o_ckv = MLA_Q_RANK
    o_kr = o_ckv + MLA_KV_RANK
    o_dq = o_kr + MLA_ROPE
    o_dk = o_dq + DIFF_HEADS * 2 * DIFF_QK
    o_dv = o_dk + DIFF_HEADS * 2 * DIFF_QK
    o_sq = o_dv + DIFF_HEADS * DIFF_V
    o_sk = o_sq + SWA_HEADS * SWA_DIM
    o_sv = o_sk + SWA_KV_HEADS * SWA_DIM
    src[G_CQ * LANES:G_CQ * LANES + MLA_Q_RANK] = np.arange(MLA_Q_RANK)
    src[G_CKV * LANES:G_CKV * LANES + MLA_KV_RANK] = o_ckv + np.arange(MLA_KV_RANK)
    src[G_KROPE * LANES + MLA_NOPE:G_KROPE * LANES + MLA_NOPE + MLA_ROPE] = o_kr + np.arange(MLA_ROPE)
    src[G_DQ * LANES:G_DQ * LANES + 256] = o_dq + np.arange(256)
    src[G_DK * LANES:G_DK * LANES + 256] = o_dk + np.arange(256)
    src[G_DV * LANES:G_DV * LANES + 256] = o_dv + np.arange(256)
    src[G_SQ * LANES:G_SQ * LANES + 512] = o_sq + np.arange(512)
    for kv in range(SWA_KV_HEADS):
        for half in range(2):
            lo = half * SWA_DIM
            src[(G_SK + kv) * LANES + lo:(G_SK + kv) * LANES + lo + SWA_DIM] = o_sk + kv * SWA_DIM + np.arange(SWA_DIM)
            src[(G_SV + kv) * LANES + lo:(G_SV + kv) * LANES + lo + SWA_DIM] = o_sv + kv * SWA_DIM + np.arange(SWA_DIM)
    return src


def _gather_columns(w, src):
    cols = jnp.take(w, jnp.asarray(np.maximum(src, 0), I32), axis=-1)
    return jnp.where(jnp.asarray(src >= 0), cols, 0.0)


def _uq_columns():
    src = -np.ones(MLA_HEADS * LANES, np.int64)
    hd = MLA_NOPE + MLA_ROPE
    for h in range(MLA_HEADS):
        src[h * LANES:h * LANES + hd] = h * hd + np.arange(hd)
    return src


def _ukv_columns():
    src = -np.ones(MLA_HEADS * LANES + MLA_HEADS * MLA_V, np.int64)
    hd = MLA_NOPE + MLA_V
    for h in range(MLA_HEADS):
        src[h * LANES:h * LANES + MLA_NOPE] = h * hd + np.arange(MLA_NOPE)
        src[MLA_HEADS * LANES + h * MLA_V:MLA_HEADS * LANES + (h + 1) * MLA_V] = h * hd + MLA_NOPE + np.arange(MLA_V)
    return src


def _rope_tables(n_ctx, s_len):
    rows = s_len // GRID_W

    def axial(rot_dim):
        n_freq = rot_dim // 4
        inv_freq = ROPE_THETA ** (-jnp.arange(n_freq, dtype=F32) / n_freq)
        row_pos = jnp.repeat(jnp.arange(rows, dtype=F32), GRID_W)
        col_pos = jnp.tile(jnp.arange(GRID_W, dtype=F32), rows)
        ang = jnp.concatenate([row_pos[:, None] * inv_freq, col_pos[:, None] * inv_freq], axis=-1)
        return jnp.cos(ang), jnp.sin(ang)

    def expand(cos, sin, lane_rot):
        half = cos.shape[1]
        lane_rot = np.asarray(lane_rot)
        idx = np.maximum(lane_rot, 0) % half
        is_rot = lane_rot >= 0
        lo = is_rot & (lane_rot < half)
        hi = is_rot & (lane_rot >= half)
        c = jnp.where(jnp.asarray(is_rot), cos[:, idx], 1.0)
        s_lo = jnp.where(jnp.asarray(lo), -sin[:, idx], 0.0)
        s_hi = jnp.where(jnp.asarray(hi), sin[:, idx], 0.0)
        ident = [jnp.ones((n_ctx, LANES), F32), jnp.zeros((n_ctx, LANES), F32), jnp.zeros((n_ctx, LANES), F32)]
        return [jnp.concatenate([i, t], axis=0) for i, t in zip(ident, (c, s_lo, s_hi))]

    cos_r, sin_r = axial(MLA_ROPE)
    cos_w, sin_w = axial(SWA_DIM)
    lanes = np.arange(LANES)
    mla_rot = np.where((lanes >= MLA_NOPE) & (lanes < MLA_NOPE + MLA_ROPE), lanes - MLA_NOPE, -1)
    return (expand(cos_r, sin_r, mla_rot) + expand(cos_r, sin_r, lanes % DIFF_QK)
            + expand(cos_w, sin_w, lanes % SWA_DIM))


def _rms(x):
    return x * lax.rsqrt(jnp.mean(x * x, axis=-1, keepdims=True) + EPS)


def _modulate(x, g, shift, scale):
    return (_rms(x) * g) * (1.0 + scale) + shift


def _rope(v, c, s_lo, s_hi, half):
    return v * c + pltpu.roll(v, half, 1) * s_hi + pltpu.roll(v, LANES - half, 1) * s_lo


def _qk(q, k):
    return lax.dot_general(q, k, (((1,), (1,)), ((), ())), preferred_element_type=F32)


def _lane_iota(shape):
    return lax.broadcasted_iota(I32, shape, len(shape) - 1)


def _masked_rows(q, segs, width):
    lane = _lane_iota(q.shape)
    zero = jnp.zeros_like(q)
    return jnp.concatenate([jnp.where((lane >= sg * width) & (lane < (sg + 1) * width), q, zero) for sg in segs], axis=0)


def _ada_kernel(c_ref, w_ref, b_ref, o_ref):
    c = c_ref[...]
    a = c * (1.0 / (1.0 + jnp.exp(-c)))
    o_ref[0] = jnp.dot(a, w_ref[0], precision=HIGHEST, preferred_element_type=F32) + b_ref[0]


def _ada(cc, w_ada, b_ada):
    n_layers, d, n_out = w_ada.shape
    rows = cc.shape[0]
    tn = d
    return pl.pallas_call(
        _ada_kernel,
        grid=(n_layers, n_out // tn),
        in_specs=[pl.BlockSpec((rows, d), lambda l, n: (0, 0)),
                  pl.BlockSpec((1, d, tn), lambda l, n: (l, 0, n)),
                  pl.BlockSpec((1, 1, tn), lambda l, n: (l, 0, n))],
        out_specs=pl.BlockSpec((1, rows, tn), lambda l, n: (l, 0, n)),
        out_shape=jax.ShapeDtypeStruct((n_layers, rows, n_out), F32),
        compiler_params=_params("arbitrary", "arbitrary"),
        name="ada",
    )(cc, w_ada, b_ada.reshape(n_layers, 1, n_out))


def _moe_combine(step, n_steps, slot_ref, next_slot_ref, y_ref, rw_ref, ybuf, sem):
    nch = ybuf.shape[1] // (TOP_K * TM)
    cur = step % 2

    def start(slots, buf):
        def body(i, carry):
            for u in range(GATHER_UNROLL):
                r = i * GATHER_UNROLL + u
                for k in range(TOP_K):
                    src = pl.multiple_of(slots[0, 0, k * TM + r] * nch, nch)
                    dst = pl.multiple_of((k * TM + r) * nch, nch)
                    pltpu.make_async_copy(y_ref.at[pl.ds(src, nch)], ybuf.at[buf, pl.ds(dst, nch)], sem.at[buf]).start()
            return carry

        lax.fori_loop(0, TM // GATHER_UNROLL, body, 0)

    @pl.when(step == 0)
    def _():
        start(slot_ref, 0)

    @pl.when(step + 1 < n_steps)
    def _():
        start(next_slot_ref, 1 - cur)

    pltpu.make_async_copy(y_ref.at[pl.ds(0, TOP_K * TM * nch)], ybuf.at[cur], sem.at[cur]).wait()
    rw = rw_ref[...]
    acc = None
    for k in range(TOP_K):
        yk = jnp.concatenate([ybuf[cur, pl.ds(k * TM * nch + c, TM, stride=nch), :] for c in range(nch)], axis=-1)
        term = rw[:, k:k + 1] * yk
        acc = term if acc is None else acc + term
    return acc


def _pre_attn_kernel(has_moe, *refs):
    if has_moe:
        slot_ref, next_slot_ref, y_ref, rw_ref, pmod_ref = refs[:5]
        x2_ref, ybuf, sem = refs[-3:]
        refs = refs[5:-3]
    (x_ref, mod_ref, g_ref, win_ref, qn_ref, wuq_ref, kvn_ref, wukv_ref,
     mc_ref, ml_ref, mh_ref, dc_ref, dl_ref, dh_ref, wc_ref, wl_ref, wh_ref,
     mq_ref, mk_ref, mv_ref, dq_ref, dk_ref, dv_ref, sq_ref, sk_ref, sv_ref) = refs
    x = x_ref[0]
    if has_moe:
        step = pl.program_id(0) * pl.num_programs(1) + pl.program_id(1)
        n_steps = pl.num_programs(0) * pl.num_programs(1)
        x = x + pmod_ref[0, 0][5:6] * _moe_combine(step, n_steps, slot_ref, next_slot_ref, y_ref, rw_ref, ybuf, sem)
        x2_ref[0] = x
    mod = mod_ref[0, 0]
    h = _modulate(x, g_ref[0], mod[0:1], mod[1:2])
    p = jnp.dot(h.astype(BF16), win_ref[0], preferred_element_type=F32)

    def grp(g, n=1):
        return p[:, g * LANES:(g + n) * LANES]

    mla_scale = LOG2E * (MLA_NOPE + MLA_ROPE) ** -0.5
    diff_scale = LOG2E * DIFF_QK ** -0.5
    swa_scale = LOG2E * SWA_DIM ** -0.5
    mla_tab = (mc_ref[...], ml_ref[...], mh_ref[...])
    diff_tab = (dc_ref[...], dl_ref[...], dh_ref[...])
    swa_tab = (wc_ref[...], wl_ref[...], wh_ref[...])

    cq = (_rms(grp(G_CQ, 2)) * qn_ref[0]).astype(BF16)
    q = jnp.dot(cq, wuq_ref[0], preferred_element_type=F32)
    ckv = (_rms(grp(G_CKV)) * kvn_ref[0]).astype(BF16)
    kv = jnp.dot(ckv, wukv_ref[0], preferred_element_type=F32)
    k_rope = _rope(grp(G_KROPE), *mla_tab, MLA_ROPE // 2)
    for hd in range(MLA_HEADS):
        sl = slice(hd * LANES, (hd + 1) * LANES)
        mq_ref[0, :, sl] = (_rope(q[:, sl], *mla_tab, MLA_ROPE // 2) * mla_scale).astype(BF16)
        mk_ref[0, :, sl] = (kv[:, sl] + k_rope).astype(BF16)
    def store_values(ref, vals):
        for g in range(2):
            ref[0, :, 2 * g * LANES:(2 * g + 1) * LANES] = vals[:, g * LANES:(g + 1) * LANES].astype(BF16)
            ref[0, :, (2 * g + 1) * LANES:(2 * g + 2) * LANES] = jnp.ones((TM, LANES), BF16)

    store_values(mv_ref, kv[:, MLA_HEADS * LANES:])

    for g in range(2):
        sl = slice(g * LANES, (g + 1) * LANES)
        dq_ref[0, :, sl] = (_rope(grp(G_DQ + g), *diff_tab, DIFF_QK // 2) * diff_scale).astype(BF16)
        dk_ref[0, :, sl] = _rope(grp(G_DK + g), *diff_tab, DIFF_QK // 2).astype(BF16)
    store_values(dv_ref, grp(G_DV, 2))

    for g in range(4):
        sl = slice(g * LANES, (g + 1) * LANES)
        sq_ref[0, :, sl] = (_rope(grp(G_SQ + g), *swa_tab, SWA_DIM // 2) * swa_scale).astype(BF16)
    for g in range(2):
        sl = slice(g * LANES, (g + 1) * LANES)
        sk_ref[0, :, sl] = _rope(grp(G_SK + g), *swa_tab, SWA_DIM // 2).astype(BF16)
    store_values(sv_ref, grp(G_SV, 2))


def _slot_tiles(slots):
    t = slots.shape[0]
    return slots.reshape(t // TM, TM, TOP_K).transpose(0, 2, 1).reshape(t // TM, 1, TOP_K * TM)


def _pre_attn(layer, xs, mods, g_attn, w_in_p, q_norm, w_uq_p, kv_norm, w_ukv_p, tables, n_ctx, moe=None):
    b, nt, d = xs.shape
    tiles = nt // TM
    n_ctx_tiles = n_ctx // TM
    ctx_row = b

    def tok(w):
        return pl.BlockSpec((1, TM, w), lambda j, bi: (bi, j, 0))

    def lay(shape):
        return pl.BlockSpec((1,) + shape, lambda j, bi: (layer,) + (0,) * len(shape))

    def mod_spec(lyr):
        return pl.BlockSpec((1, 1, N_MOD, d), lambda j, bi: (lyr, jnp.where(j < n_ctx_tiles, ctx_row, bi), 0, 0))

    tab = pl.BlockSpec((TM, LANES), lambda j, bi: (j, 0))
    widths = (512, 512, 512, 256, 256, 512, 512, 256, 512)
    in_specs = [tok(d), mod_spec(layer), lay((1, d)), lay(w_in_p.shape[1:]), lay((1, MLA_Q_RANK)),
                lay(w_uq_p.shape[1:]), lay((1, MLA_KV_RANK)), lay(w_ukv_p.shape[1:])] + [tab] * 9
    out_specs = [tok(w) for w in widths]
    out_shape = [jax.ShapeDtypeStruct((b, nt, w), BF16) for w in widths]
    args = [xs, mods, g_attn, w_in_p, q_norm, w_uq_p, kv_norm, w_ukv_p, *tables]
    scratch = []
    if moe is not None:
        y_rows, slots, rw = moe
        nch = d // LANES

        def next_tile(j, bi):
            wrap = bi + 1 == b
            nj = jnp.minimum(jnp.where(wrap, j + 1, j), tiles - 1)
            return (jnp.where(wrap, 0, bi + 1) * tiles + nj, 0, 0)

        slot_block = (1, 1, TOP_K * TM)
        in_specs = [pl.BlockSpec(slot_block, lambda j, bi: (bi * tiles + j, 0, 0), memory_space=pltpu.SMEM),
                    pl.BlockSpec(slot_block, next_tile, memory_space=pltpu.SMEM),
                    pl.BlockSpec(memory_space=pl.ANY),
                    pl.BlockSpec((TM, LANES), lambda j, bi: (bi * tiles + j, 0)),
                    mod_spec(layer - 1)] + in_specs
        slot_tiles = _slot_tiles(slots)
        args = [slot_tiles, slot_tiles, y_rows, rw, mods] + args
        out_specs = out_specs + [tok(d)]
        out_shape = out_shape + [jax.ShapeDtypeStruct((b, nt, d), F32)]
        scratch = [pltpu.VMEM((2, TOP_K * TM * nch, LANES), F32), pltpu.SemaphoreType.DMA((2,))]
    outs = pl.pallas_call(
        functools.partial(_pre_attn_kernel, moe is not None),
        grid=(tiles, b),
        in_specs=in_specs,
        out_specs=out_specs,
        out_shape=out_shape,
        scratch_shapes=scratch,
        compiler_params=_params("arbitrary", "arbitrary"),
        name="pre_attn",
    )(*args)
    return (outs[-1], outs[:-1]) if moe is not None else (xs, outs)


def _softmax_pv(s, v):
    p = jnp.exp2(s - jnp.max(s, axis=-1, keepdims=True))
    ov = jnp.dot(p.astype(BF16), v, preferred_element_type=F32)
    return ov[:, :LANES] / ov[:, LANES:]


def _dense_steps(n_ctx, nt, run):
    j = pl.program_id(1)

    @pl.when(j == 0)
    def _():
        run(0, n_ctx, n_ctx)

    @pl.when(j > 0)
    def _():
        run(pl.multiple_of(n_ctx + (j - 1) * TQ, TM), TQ, nt)


def _mla_attn_kernel(n_ctx, q_ref, k_ref, v_ref, o_ref):
    def run(row0, rows, nk):
        lane = _lane_iota((rows, LANES))
        outs = []
        for hd in range(MLA_HEADS):
            sl = slice(hd * LANES, (hd + 1) * LANES)
            vs = slice((hd // 2) * 2 * LANES, (hd // 2 + 1) * 2 * LANES)
            s = _qk(q_ref[0, pl.ds(row0, rows), sl], k_ref[0, :nk, sl])
            outs.append(_softmax_pv(s, v_ref[0, :nk, vs]))
        for g in range(2):
            o_ref[0, pl.ds(row0, rows), g * LANES:(g + 1) * LANES] = jnp.where(
                lane < MLA_V, outs[2 * g], outs[2 * g + 1]).astype(BF16)

    _dense_steps(n_ctx, k_ref.shape[1], run)


def _diff_attn_kernel(n_ctx, lambda_init, q_ref, k_ref, v_ref, lam_ref, g_ref, seg_ref, o_ref):
    lam = lam_ref[0]
    lam_full = (jnp.exp(jnp.sum(lam[0:1] * lam[1:2], axis=-1, keepdims=True))
                - jnp.exp(jnp.sum(lam[2:3] * lam[3:4], axis=-1, keepdims=True)) + lambda_init)

    def run(row0, rows, nk):
        lane = _lane_iota((rows, LANES))
        heads = []
        for hd in range(DIFF_HEADS):
            sl = slice((hd // 2) * LANES, (hd // 2 + 1) * LANES)
            q = q_ref[0, pl.ds(row0, rows), sl]
            k = k_ref[0, :nk, sl]
            v = v_ref[0, :nk, (hd // 2) * 2 * LANES:(hd // 2 + 1) * 2 * LANES]
            a = []
            for comp in range(2):
                seg = (hd % 2) * 2 + comp
                qm = jnp.where((lane >= seg * DIFF_QK) & (lane < (seg + 1) * DIFF_QK), q, jnp.zeros_like(q))
                a.append(_softmax_pv(_qk(qm, k), v))
            heads.append(a[0] - lam_full * a[1])
        for g in range(2):
            o = jnp.where(lane < DIFF_V, heads[2 * g], heads[2 * g + 1])
            ms = jnp.dot(o * o, seg_ref[...], precision=HIGHEST, preferred_element_type=F32) * (1.0 / DIFF_V)
            o = o * lax.rsqrt(ms + EPS) * g_ref[...] * (1.0 - lambda_init)
            o_ref[0, pl.ds(row0, rows), g * LANES:(g + 1) * LANES] = o.astype(BF16)

    _dense_steps(n_ctx, k_ref.shape[1], run)


def _swa_attn_kernel(n_ctx, q_ref, k_ref, v_ref, sink_ref, o_ref):
    j = pl.program_id(1)
    nt = k_ref.shape[1]
    band = TM + 2 * WINDOW
    lane = _lane_iota((TM, LANES))
    group_heads = SWA_HEADS // SWA_KV_HEADS

    def run(kv, k, v, allowed):
        for g in (2 * kv, 2 * kv + 1):
            q = q_ref[0, :, g * LANES:(g + 1) * LANES]
            outs = []
            for half in range(2):
                qm = jnp.where((lane >= half * SWA_DIM) & (lane < (half + 1) * SWA_DIM), q, jnp.zeros_like(q))
                s = _qk(qm, k)
                if allowed is not None:
                    s = jnp.where(allowed, s, NEG_INF)
                sink = sink_ref[0, 2 * g + half:2 * g + half + 1, 0:1] * LOG2E
                m = jnp.maximum(jnp.max(s, axis=-1, keepdims=True), sink)
                p = jnp.exp2(s - m)
                ov = jnp.dot(p.astype(BF16), v, preferred_element_type=F32)
                outs.append(ov[:, :LANES] / (ov[:, LANES:] + jnp.exp2(sink - m)))
            o_ref[0, :, g * LANES:(g + 1) * LANES] = jnp.where(lane < SWA_DIM, outs[0], outs[1]).astype(BF16)

    @pl.when(j < n_ctx // TM)
    def _():
        for kv in range(SWA_KV_HEADS):
            ks = slice(kv * LANES, (kv + 1) * LANES)
            vs = slice(2 * kv * LANES, (2 * kv + 2) * LANES)
            run(kv, k_ref[0, :n_ctx, ks], v_ref[0, :n_ctx, vs], None)

    @pl.when(j >= n_ctx // TM)
    def _():
        q0 = j * TM
        w0 = pl.multiple_of(jnp.clip(q0 - WINDOW, n_ctx, nt - band), WINDOW)
        q_pos = q0 + lax.broadcasted_iota(I32, (TM, n_ctx + band), 0)
        col = lax.broadcasted_iota(I32, (TM, n_ctx + band), 1)
        k_pos = w0 + col - n_ctx
        allowed = (col < n_ctx) | (jnp.abs(k_pos - q_pos) <= WINDOW)
        for kv in range(SWA_KV_HEADS):
            ks = slice(kv * LANES, (kv + 1) * LANES)
            vs = slice(2 * kv * LANES, (2 * kv + 2) * LANES)
            k = jnp.concatenate([k_ref[0, :n_ctx, ks], k_ref[0, pl.ds(w0, band), ks]], axis=0)
            v = jnp.concatenate([v_ref[0, :n_ctx, vs], v_ref[0, pl.ds(w0, band), vs]], axis=0)
            run(kv, k, v, allowed)


def _attention(kernel, q, k, v, extra, out_width, name, dense_steps=None):
    b, nt, _ = q.shape

    def whole(width):
        return pl.BlockSpec((1, nt, width), lambda bi, j: (bi, 0, 0))

    def tile(width):
        return pl.BlockSpec((1, TM, width), lambda bi, j: (bi, j, 0))

    q_spec, o_spec, steps = (tile, tile, nt // TM) if dense_steps is None else (whole, whole, dense_steps)
    extra_specs = [pl.BlockSpec(e.shape, lambda bi, j, nd=e.ndim: (0,) * nd) for e in extra]
    return pl.pallas_call(
        kernel,
        grid=(b, steps),
        in_specs=[q_spec(q.shape[2]), whole(k.shape[2]), whole(v.shape[2])] + extra_specs,
        out_specs=o_spec(out_width),
        out_shape=jax.ShapeDtypeStruct((b, nt, out_width), BF16),
        compiler_params=_params("arbitrary", "arbitrary"),
        name=name,
    )(q, k, v, *extra)


def _post_attn_kernel(cap, x_ref, mod_ref, mla_ref, diff_ref, swa_ref, wout_ref, g_ref, wr_ref, br_ref,
                      x1_ref, route_ref, rw_ref, cnt_ref, xs_ref, carry_ref, fbuf, slot_v, slot_s, dsem, ssem):
    step = pl.program_id(0) * pl.num_programs(1) + pl.program_id(1)
    n_steps = pl.num_programs(0) * pl.num_programs(1)
    cur, prev, prev2 = step % 3, (step + 2) % 3, (step + 1) % 3
    par = step % 2
    nch = fbuf.shape[1] // TM

    def slots_landed(sbuf):
        pltpu.make_async_copy(slot_v, slot_s.at[sbuf], ssem.at[sbuf]).wait()

    def scatter(buf, sbuf):
        def body(i, carry):
            for u in range(GATHER_UNROLL):
                r = i * GATHER_UNROLL + u
                src = fbuf.at[buf, pl.ds(pl.multiple_of(r * nch, nch), nch)]
                for k in range(TOP_K):
                    dst = pl.multiple_of(slot_s[sbuf, k, r] * nch, nch)
                    pltpu.make_async_copy(src, xs_ref.at[pl.ds(dst, nch)], dsem.at[buf]).start()
            return carry

        lax.fori_loop(0, TM // GATHER_UNROLL, body, 0)

    def drain(buf):
        for _ in range(TOP_K):
            pltpu.make_async_copy(fbuf.at[buf], xs_ref.at[pl.ds(0, TM * nch)], dsem.at[buf]).wait()

    @pl.when(step == 0)
    def _():
        carry_ref[...] = jnp.zeros_like(carry_ref)

    @pl.when(step >= 3)
    def _():
        drain(cur)

    x = x_ref[0]
    mod = mod_ref[0, 0]
    a = jnp.concatenate([mla_ref[0], diff_ref[0], swa_ref[0]], axis=-1)
    x1 = x + mod[2:3] * jnp.dot(a, wout_ref[0], preferred_element_type=F32)
    x1_ref[0] = x1
    f = _modulate(x1, g_ref[0], mod[3:4], mod[4:5])
    for c in range(nch):
        fbuf[cur, pl.ds(c, TM, stride=nch), :] = f[:, c * LANES:(c + 1) * LANES]

    logits = jnp.dot(f.astype(BF16), wr_ref[0], preferred_element_type=F32) + br_ref[0]
    lane = _lane_iota((TM, LANES))
    vals, hots = [], []
    for _ in range(TOP_K):
        m = jnp.max(logits, axis=-1, keepdims=True)
        idx = jnp.min(jnp.where(logits == m, lane, LANES), axis=-1, keepdims=True)
        hot = lane == idx
        logits = jnp.where(hot, -3e38, logits)
        vals.append(m)
        hots.append((idx, hot))
    es = [jnp.exp(v - vals[0]) for v in vals]
    denom = functools.reduce(jnp.add, es)
    sel = functools.reduce(jnp.add, [jnp.where(hot, 1.0, 0.0) for _, hot in hots])

    r_io = lax.broadcasted_iota(I32, (TM, TM), 0)
    c_io = lax.broadcasted_iota(I32, (TM, TM), 1)
    tril = jnp.where(c_io < r_io, 1.0, 0.0).astype(BF16)
    rank = carry_ref[0:1, :] + jnp.dot(tril, sel.astype(BF16), preferred_element_type=F32)
    carry_ref[0:1, :] = carry_ref[0:1, :] + jnp.sum(sel, axis=0, keepdims=True)
    cnt_ref[...] = jnp.broadcast_to(carry_ref[0:1, :], cnt_ref.shape)

    route = jnp.zeros((TM, LANES), I32)
    slots = jnp.zeros((TM, LANES), I32)
    rw = jnp.zeros((TM, LANES), F32)
    for k, (idx, hot) in enumerate(hots):
        rk = jnp.sum(jnp.where(hot, rank, 0.0), axis=-1, keepdims=True).astype(I32)
        route = jnp.where(lane == k, idx, route)
        route = jnp.where(lane == TOP_K + k, rk, route)
        slots = jnp.where(lane == k, idx * cap + rk, slots)
        rw = jnp.where(lane == k, es[k] / denom, rw)
    route_ref[...] = route
    rw_ref[...] = rw
    @pl.when(step >= 1)
    def _():
        slots_landed(1 - par)

    slot_v[...] = slots.T[0:SUBLANES, :]
    pltpu.make_async_copy(slot_v, slot_s.at[par], ssem.at[par]).start()

    @pl.when(step >= 1)
    def _():
        scatter(prev, 1 - par)

    @pl.when(step == n_steps - 1)
    def _():
        slots_landed(par)
        scatter(cur, par)
        drain(prev2)
        drain(prev)
        drain(cur)


def _post_attn(layer, xs, mods, mla_o, diff_o, swa_o, w_out_b, g_ffn, w_router_p, b_router_p, n_ctx, n_exp, cap):
    b, nt, d = xs.shape
    t = b * nt
    nch = d // LANES
    n_ctx_tiles = n_ctx // TM
    tiles = nt // TM

    def tok(w):
        return pl.BlockSpec((1, TM, w), lambda bi, j: (bi, j, 0))

    def lay(shape):
        return pl.BlockSpec((1,) + shape, lambda bi, j: (layer,) + (0,) * len(shape))

    def flat(rows, w):
        return pl.BlockSpec((rows, w), lambda bi, j: (bi * tiles + j, 0))

    mod_spec = pl.BlockSpec((1, 1, N_MOD, d), lambda bi, j: (layer, jnp.where(j < n_ctx_tiles, b, bi), 0, 0))
    return pl.pallas_call(
        functools.partial(_post_attn_kernel, cap),
        grid=(b, tiles),
        in_specs=[tok(d), mod_spec, tok(256), tok(256), tok(512), lay(w_out_b.shape[1:]), lay((1, d)),
                  lay((d, LANES)), lay((1, LANES))],
        out_specs=[tok(d), flat(TM, LANES), flat(TM, LANES),
                   pl.BlockSpec((SUBLANES, LANES), lambda bi, j: (0, 0)), pl.BlockSpec(memory_space=pl.ANY)],
        out_shape=[jax.ShapeDtypeStruct((b, nt, d), F32),
                   jax.ShapeDtypeStruct((t, LANES), I32), jax.ShapeDtypeStruct((t, LANES), F32),
                   jax.ShapeDtypeStruct((SUBLANES, LANES), F32),
                   jax.ShapeDtypeStruct((n_exp * cap * nch, LANES), F32)],
        scratch_shapes=[pltpu.VMEM((SUBLANES, LANES), F32), pltpu.VMEM((3, TM * nch, LANES), F32),
                        pltpu.VMEM((SUBLANES, TM), I32), pltpu.SMEM((2, SUBLANES, TM), I32),
                        pltpu.SemaphoreType.DMA((3,)), pltpu.SemaphoreType.DMA((2,))],
        compiler_params=_params("arbitrary", "arbitrary"),
        name="post_attn",
    )(xs, mods, mla_o, diff_o, swa_o, w_out_b, g_ffn, w_router_p, b_router_p)


def _pad_zero_kernel(n_exp, nch, start_ref, len_ref, xs_in_ref, xs_ref, zbuf, sem):
    del xs_in_ref
    zbuf[...] = jnp.zeros_like(zbuf)
    bits = (TMOE - 1).bit_length()
    for wait in (False, True):
        for e in range(n_exp):
            for bit in range(bits):
                size = 1 << bit

                @pl.when(((len_ref[e] >> bit) & 1) == 1)
                def _():
                    below = len_ref[e] & (size - 1)
                    dst = pl.multiple_of((start_ref[e] + below) * nch, nch)
                    cp = pltpu.make_async_copy(zbuf.at[pl.ds(0, size * nch)], xs_ref.at[pl.ds(dst, size * nch)], sem)
                    cp.wait() if wait else cp.start()


def _pad_zero(xs_rows, pad_start, pad_len, n_exp, nch):
    return pl.pallas_call(
        functools.partial(_pad_zero_kernel, n_exp, nch),
        grid_spec=pltpu.PrefetchScalarGridSpec(
            num_scalar_prefetch=2,
            grid=(1,),
            in_specs=[pl.BlockSpec(memory_space=pl.ANY)],
            out_specs=pl.BlockSpec(memory_space=pl.ANY),
            scratch_shapes=[pltpu.VMEM((TMOE // 2 * nch, LANES), F32), pltpu.SemaphoreType.DMA]),
        out_shape=jax.ShapeDtypeStruct(xs_rows.shape, F32),
        input_output_aliases={2: 0},
        compiler_params=_params("arbitrary"),
        name="pad_zero",
    )(pad_start, pad_len, xs_rows)


def _moe_kernel(nch, te_ref, tb_ref, nu_ref, x_ref, wg_ref, bg_ref, wu_ref, bu_ref, wd_ref, bd_ref, y_ref,
                wg_b, wu_b, wd_b):
    del tb_ref
    i = pl.program_id(0)

    @pl.when((i == 0) | (te_ref[i] != te_ref[jnp.maximum(i - 1, 0)]))
    def _():
        wg_b[...] = wg_ref[0].astype(BF16)
        wu_b[...] = wu_ref[0].astype(BF16)
        wd_b[...] = wd_ref[0].astype(BF16)

    @pl.when(i < nu_ref[0])
    def _():
        x = jnp.concatenate([x_ref[pl.ds(c, TMOE, stride=nch), :] for c in range(nch)], axis=-1).astype(BF16)
        gate = jnp.dot(x, wg_b[...], preferred_element_type=F32) + bg_ref[0]
        up = jnp.dot(x, wu_b[...], preferred_element_type=F32) + bu_ref[0]
        gate = jnp.minimum(gate, SWIGLU_LIMIT)
        up = jnp.clip(up, -SWIGLU_LIMIT, SWIGLU_LIMIT)
        act = gate * (1.0 / (1.0 + jnp.exp(-SWIGLU_ALPHA * gate))) * (up + 1.0)
        y = jnp.dot(act.astype(BF16), wd_b[...], preferred_element_type=F32) + bd_ref[0]
        for c in range(nch):
            y_ref[pl.ds(c, TMOE, stride=nch), :] = y[:, c * LANES:(c + 1) * LANES]


def _moe(layer, xs_rows, tile_e, tile_blk, n_used, n_tiles, wg, bg, wu, bu, wd, bd, n_exp):
    d, f = wg.shape[1], wg.shape[2]
    nch = d // LANES

    def rows(i, te, tb, nu):
        return (tb[i], 0)

    def exp(i, te, tb, nu):
        return (layer * n_exp + te[i], 0, 0)

    return pl.pallas_call(
        functools.partial(_moe_kernel, nch),
        grid_spec=pltpu.PrefetchScalarGridSpec(
            num_scalar_prefetch=3,
            grid=(n_tiles,),
            in_specs=[pl.BlockSpec((TMOE * nch, LANES), rows),
                      pl.BlockSpec((1, d, f), exp), pl.BlockSpec((1, 1, f), exp),
                      pl.BlockSpec((1, d, f), exp), pl.BlockSpec((1, 1, f), exp),
                      pl.BlockSpec((1, f, d), exp), pl.BlockSpec((1, 1, d), exp)],
            out_specs=pl.BlockSpec((TMOE * nch, LANES), rows),
            scratch_shapes=[pltpu.VMEM((d, f), BF16), pltpu.VMEM((d, f), BF16), pltpu.VMEM((f, d), BF16)]),
        out_shape=jax.ShapeDtypeStruct(xs_rows.shape, F32),
        compiler_params=_params("arbitrary"),
        name="moe",
    )(tile_e, tile_blk, n_used, xs_rows, wg, bg, wu, bu, wd, bd)


def _final_kernel(slot_ref, next_slot_ref, y_ref, rw_ref, mod_ref, x1_ref, g_ref, o_ref, ybuf, sem):
    step = pl.program_id(0) * pl.num_programs(1) + pl.program_id(1)
    n_steps = pl.num_programs(0) * pl.num_programs(1)
    x = x1_ref[0] + mod_ref[0, 0][5:6] * _moe_combine(step, n_steps, slot_ref, next_slot_ref, y_ref, rw_ref, ybuf, sem)
    o_ref[0] = _rms(x) * g_ref[...]


def _final_norm(layer, x1, y_rows, slots, rw, mods, g_final, n_ctx, s_len):
    b, nt, d = x1.shape
    nch = d // LANES
    tiles = nt // TM
    lat_tiles = s_len // TM
    off = n_ctx // TM

    def next_tile(bi, j):
        wrap = j + 1 == lat_tiles
        nb = jnp.minimum(jnp.where(wrap, bi + 1, bi), b - 1)
        return (nb * tiles + off + jnp.where(wrap, 0, j + 1), 0, 0)

    slot_block = (1, 1, TOP_K * TM)
    slot_tiles = _slot_tiles(slots)
    return pl.pallas_call(
        _final_kernel,
        grid=(b, lat_tiles),
        in_specs=[pl.BlockSpec(slot_block, lambda bi, j: (bi * tiles + off + j, 0, 0), memory_space=pltpu.SMEM),
                  pl.BlockSpec(slot_block, next_tile, memory_space=pltpu.SMEM),
                  pl.BlockSpec(memory_space=pl.ANY),
                  pl.BlockSpec((TM, LANES), lambda bi, j: (bi * tiles + off + j, 0)),
                  pl.BlockSpec((1, 1, N_MOD, d), lambda bi, j: (layer, bi, 0, 0)),
                  pl.BlockSpec((1, TM, d), lambda bi, j: (bi, j + off, 0)),
                  pl.BlockSpec((1, d), lambda bi, j: (0, 0))],
        out_specs=pl.BlockSpec((1, TM, d), lambda bi, j: (bi, j, 0)),
        out_shape=jax.ShapeDtypeStruct((b, s_len, d), F32),
        scratch_shapes=[pltpu.VMEM((2, TOP_K * TM * nch, LANES), F32), pltpu.SemaphoreType.DMA((2,))],
        compiler_params=_params("arbitrary", "arbitrary"),
        name="final_norm",
    )(slot_tiles, slot_tiles, y_rows, rw, mods, x1, g_final.reshape(1, d))


def _routing_plan(route, counts, n_exp, n_tiles, cap):
    slots = route[:, 0:TOP_K] * cap + route[:, TOP_K:2 * TOP_K]
    counts = counts[0, :n_exp].astype(I32)
    e_tiles = (counts + TMOE - 1) // TMOE
    ends = jnp.cumsum(e_tiles)
    n_used = ends[-1].reshape(1)
    tile_ids = jnp.minimum(jnp.arange(n_tiles, dtype=I32), n_used[0] - 1)
    tile_e = jnp.minimum(jnp.sum((tile_ids[:, None] >= ends[None, :]).astype(I32), axis=-1), n_exp - 1)
    first = jnp.sum(jnp.where(tile_e[:, None] == jnp.arange(n_exp, dtype=I32), (ends - e_tiles)[None, :], 0), axis=-1)
    tile_blk = tile_e * (cap // TMOE) + tile_ids - first
    pad_start = jnp.arange(n_exp, dtype=I32) * cap + counts
    pad_len = e_tiles * TMOE - counts
    return slots, tile_e, tile_blk, n_used, pad_start, pad_len


def kernel(x, c, ctx, c_ctx, w_ada, b_ada, g_attn, w_in, mla_q_norm, mla_w_uq, mla_kv_norm, mla_w_ukv,
           diff_lambda, diff_subln, swa_sink, w_out, g_ffn, w_router, b_router, w_gate, b_gate, w_up, b_up,
           w_down, b_down, g_final):
    b, s_len, d = x.shape
    n_ctx = ctx.shape[1]
    n_layers = w_ada.shape[0]
    n_exp = w_router.shape[2]
    nt = n_ctx + s_len
    t = b * nt
    assert d % LANES == 0 and n_ctx % TM == 0 and s_len % TQ == 0 and t % TMOE == 0 and s_len % GRID_W == 0
    assert n_exp <= LANES and s_len >= TM + 2 * WINDOW and t // TM >= 3

    w_in_p = _gather_columns(w_in, _in_proj_columns()).astype(BF16)
    w_uq_p = _gather_columns(mla_w_uq, _uq_columns()).astype(BF16)
    w_ukv_p = _gather_columns(mla_w_ukv, _ukv_columns()).astype(BF16)
    w_out_b = w_out.astype(BF16)
    w_router_p = jnp.pad(w_router, ((0, 0), (0, 0), (0, LANES - n_exp))).astype(BF16)
    b_router_p = jnp.pad(b_router, ((0, 0), (0, LANES - n_exp)), constant_values=NEG_INF).reshape(n_layers, 1, LANES)
    wg = w_gate.reshape(n_layers * n_exp, d, -1)
    wu = w_up.reshape(n_layers * n_exp, d, -1)
    wd = w_down.reshape(n_layers * n_exp, -1, d)
    bg = b_gate.reshape(n_layers * n_exp, 1, -1)
    bu = b_up.reshape(n_layers * n_exp, 1, -1)
    bd = b_down.reshape(n_layers * n_exp, 1, d)
    lam_p = jnp.pad(diff_lambda, ((0, 0), (0, 0), (0, LANES - DIFF_QK)))
    subln_p = jnp.tile(diff_subln, (1, LANES // DIFF_V)).reshape(n_layers, 1, LANES)
    sink_p = jnp.broadcast_to(swa_sink[:, :, None], (n_layers, SWA_HEADS, LANES))
    seg = jnp.asarray((np.arange(LANES)[:, None] // DIFF_V == np.arange(LANES)[None, :] // DIFF_V), F32)
    tables = _rope_tables(n_ctx, s_len)

    ada_rows = -(-(b + 1) // SUBLANES) * SUBLANES
    cc = jnp.concatenate([c, c_ctx[None, :], jnp.zeros((ada_rows - b - 1, d), F32)], axis=0)
    mods = _ada(cc, w_ada, b_ada).reshape(n_layers, ada_rows, N_MOD, d)

    cap = t
    n_tiles = (t * TOP_K) // TMOE + n_exp
    xs = jnp.concatenate([ctx, x], axis=1)
    moe = None
    for layer in range(n_layers):
        lambda_init = 0.8 - 0.6 * math.exp(-0.3 * layer)
        xs, (mq, mk, mv, dq, dk, dv, sq, sk, sv) = _pre_attn(
            layer, xs, mods, g_attn.reshape(n_layers, 1, d), w_in_p, mla_q_norm.reshape(n_layers, 1, -1), w_uq_p,
            mla_kv_norm.reshape(n_layers, 1, -1), w_ukv_p, tables, n_ctx, moe)
        dense_steps = 1 + s_len // TQ
        mla_o = _attention(functools.partial(_mla_attn_kernel, n_ctx), mq, mk, mv, [], 256, "mla_attn", dense_steps)
        diff_o = _attention(functools.partial(_diff_attn_kernel, n_ctx, lambda_init), dq, dk, dv,
                            [lam_p[layer:layer + 1], subln_p[layer], seg], 256, "diff_attn", dense_steps)
        swa_o = _attention(functools.partial(_swa_attn_kernel, n_ctx), sq, sk, sv, [sink_p[layer:layer + 1]],
                           512, "swa_attn")
        xs, route, rw, counts, xs_rows = _post_attn(
            layer, xs, mods, mla_o, diff_o, swa_o, w_out_b, g_ffn.reshape(n_layers, 1, d), w_router_p, b_router_p,
            n_ctx, n_exp, cap)
        slots, tile_e, tile_blk, n_used, pad_start, pad_len = _routing_plan(route, counts, n_exp, n_tiles, cap)
        xs_rows = _pad_zero(xs_rows, pad_start, pad_len, n_exp, d // LANES)
        moe = (_moe(layer, xs_rows, tile_e, tile_blk, n_used, n_tiles, wg, bg, wu, bu, wd, bd, n_exp), slots, rw)
    return _final_norm(n_layers - 1, xs, *moe, mods, g_final, n_ctx, s_len)
```

```python
import functools
import math

import jax
import jax.numpy as jnp
import numpy as np
from jax import lax
from jax.experimental import pallas as pl
from jax.experimental.pallas import tpu as pltpu

F32 = jnp.float32
BF16 = jnp.bfloat16
I32 = jnp.int32
HIGHEST = lax.Precision.HIGHEST
LOG2E = math.log2(math.e)

LANES = 128
SUBLANES = 8
VMEM_LIMIT = 56 * 1024 * 1024

GRID_W = 64
ROPE_THETA = 10000.0
EPS = 1e-6
NEG_INF = -1e30
N_MOD = 6

MLA_HEADS, MLA_Q_RANK, MLA_KV_RANK, MLA_NOPE, MLA_ROPE, MLA_V = 4, 256, 128, 64, 32, 64
DIFF_HEADS, DIFF_QK = 4, 32
DIFF_V = 2 * DIFF_QK
SWA_HEADS, SWA_KV_HEADS, SWA_DIM, WINDOW = 8, 2, 64, 128
TOP_K = 4
SWIGLU_LIMIT = 7.0
SWIGLU_ALPHA = 1.702

TM = 256
TQ = 512
TMOE = 512
GATHER_UNROLL = 8

G_CQ, G_CKV, G_KROPE, G_DQ, G_DK, G_DV, G_SQ, G_SK, G_SV, N_GROUPS = 0, 2, 3, 4, 6, 8, 10, 14, 16, 18


def _params(*sem):
    return pltpu.CompilerParams(dimension_semantics=sem, vmem_limit_bytes=VMEM_LIMIT)


def _in_proj_columns():
    src = -np.ones(N_GROUPS * LANES, np.int64)
    o_ckv = MLA_Q_RANK
    o_kr = o_ckv + MLA_KV_RANK
    o_dq = o_kr + MLA_ROPE
    o_dk = o_dq + DIFF_HEADS * 2 * DIFF_QK
    o_dv = o_dk + DIFF_HEADS * 2 * DIFF_QK
    o_sq = o_dv + DIFF_HEADS * DIFF_V
    o_sk = o_sq + SWA_HEADS * SWA_DIM
    o_sv = o_sk + SWA_KV_HEADS * SWA_DIM
    src[G_CQ * LANES:G_CQ * LANES + MLA_Q_RANK] = np.arange(MLA_Q_RANK)
    src[G_CKV * LANES:G_CKV * LANES + MLA_KV_RANK] = o_ckv + np.arange(MLA_KV_RANK)
    src[G_KROPE * LANES + MLA_NOPE:G_KROPE * LANES + MLA_NOPE + MLA_ROPE] = o_kr + np.arange(MLA_ROPE)
    src[G_DQ * LANES:G_DQ * LANES + 256] = o_dq + np.arange(256)
    src[G_DK * LANES:G_DK * LANES + 256] = o_dk + np.arange(256)
    src[G_DV * LANES:G_DV * LANES + 256] = o_dv + np.arange(256)
    src[G_SQ * LANES:G_SQ * LANES + 512] = o_sq + np.arange(512)
    for kv in range(SWA_KV_HEADS):
        for half in range(2):
            lo = half * SWA_DIM
            src[(G_SK + kv) * LANES + lo:(G_SK + kv) * LANES + lo + SWA_DIM] = o_sk + kv * SWA_DIM + np.arange(SWA_DIM)
            src[(G_SV + kv) * LANES + lo:(G_SV + kv) * LANES + lo + SWA_DIM] = o_sv + kv * SWA_DIM + np.arange(SWA_DIM)
    return src


def _gather_columns(w, src):
    cols = jnp.take(w, jnp.asarray(np.maximum(src, 0), I32), axis=-1)
    return jnp.where(jnp.asarray(src >= 0), cols, 0.0)


def _uq_columns():
    src = -np.ones(MLA_HEADS * LANES, np.int64)
    hd = MLA_NOPE + MLA_ROPE
    for h in range(MLA_HEADS):
        src[h * LANES:h * LANES + hd] = h * hd + np.arange(hd)
    return src


def _ukv_columns():
    src = -np.ones(MLA_HEADS * LANES + MLA_HEADS * MLA_V, np.int64)
    hd = MLA_NOPE + MLA_V
    for h in range(MLA_HEADS):
        src[h * LANES:h * LANES + MLA_NOPE] = h * hd + np.arange(MLA_NOPE)
        src[MLA_HEADS * LANES + h * MLA_V:MLA_HEADS * LANES + (h + 1) * MLA_V] = h * hd + MLA_NOPE + np.arange(MLA_V)
    return src


def _rope_tables(n_ctx, s_len):
    rows = s_len // GRID_W

    def axial(rot_dim):
        n_freq = rot_dim // 4
        inv_freq = ROPE_THETA ** (-jnp.arange(n_freq, dtype=F32) / n_freq)
        row_pos = jnp.repeat(jnp.arange(rows, dtype=F32), GRID_W)
        col_pos = jnp.tile(jnp.arange(GRID_W, dtype=F32), rows)
        ang = jnp.concatenate([row_pos[:, None] * inv_freq, col_pos[:, None] * inv_freq], axis=-1)
        return jnp.cos(ang), jnp.sin(ang)

    def expand(cos, sin, lane_rot):
        half = cos.shape[1]
        lane_rot = np.asarray(lane_rot)
        idx = np.maximum(lane_rot, 0) % half
        is_rot = lane_rot >= 0
        lo = is_rot & (lane_rot < half)
        hi = is_rot & (lane_rot >= half)
        c = jnp.where(jnp.asarray(is_rot), cos[:, idx], 1.0)
        s_lo = jnp.where(jnp.asarray(lo), -sin[:, idx], 0.0)
        s_hi = jnp.where(jnp.asarray(hi), sin[:, idx], 0.0)
        ident = [jnp.ones((n_ctx, LANES), F32), jnp.zeros((n_ctx, LANES), F32), jnp.zeros((n_ctx, LANES), F32)]
        return [jnp.concatenate([i, t], axis=0) for i, t in zip(ident, (c, s_lo, s_hi))]

    cos_r, sin_r = axial(MLA_ROPE)
    cos_w, sin_w = axial(SWA_DIM)
    lanes = np.arange(LANES)
    mla_rot = np.where((lanes >= MLA_NOPE) & (lanes < MLA_NOPE + MLA_ROPE), lanes - MLA_NOPE, -1)
    return (expand(cos_r, sin_r, mla_rot) + expand(cos_r, sin_r, lanes % DIFF_QK)
            + expand(cos_w, sin_w, lanes % SWA_DIM))


def _rms(x):
    return x * lax.rsqrt(jnp.mean(x * x, axis=-1, keepdims=True) + EPS)


def _modulate(x, g, shift, scale):
    return (_rms(x) * g) * (1.0 + scale) + shift


def _rope(v, c, s_lo, s_hi, half):
    return v * c + pltpu.roll(v, half, 1) * s_hi + pltpu.roll(v, LANES - half, 1) * s_lo


def _qk(q, k):
    return lax.dot_general(q, k, (((1,), (1,)), ((), ())), preferred_element_type=F32)


def _lane_iota(shape):
    return lax.broadcasted_iota(I32, shape, len(shape) - 1)


def _ada_kernel(c_ref, w_ref, b_ref, o_ref):
    c = c_ref[...]
    a = c * (1.0 / (1.0 + jnp.exp(-c)))
    o_ref[0] = jnp.dot(a, w_ref[0], precision=HIGHEST, preferred_element_type=F32) + b_ref[0]


def _ada(cc, w_ada, b_ada):
    n_layers, d, n_out = w_ada.shape
    rows = cc.shape[0]
    tn = d
    return pl.pallas_call(
        _ada_kernel,
        grid=(n_layers, n_out // tn),
        in_specs=[pl.BlockSpec((rows, d), lambda l, n: (0, 0)),
                  pl.BlockSpec((1, d, tn), lambda l, n: (l, 0, n)),
                  pl.BlockSpec((1, 1, tn), lambda l, n: (l, 0, n))],
        out_specs=pl.BlockSpec((1, rows, tn), lambda l, n: (l, 0, n)),
        out_shape=jax.ShapeDtypeStruct((n_layers, rows, n_out), F32),
        compiler_params=_params("arbitrary", "arbitrary"),
        name="ada",
    )(cc, w_ada, b_ada.reshape(n_layers, 1, n_out))


def _moe_combine(step, n_steps, slot_ref, next_slot_ref, y_ref, rw_ref, ybuf, sem):
    nch = ybuf.shape[1] // (TOP_K * TM)
    cur = step % 2

    def start(slots, buf):
        def body(i, carry):
            for u in range(GATHER_UNROLL):
                r = i * GATHER_UNROLL + u
                for k in range(TOP_K):
                    src = pl.multiple_of(slots[0, 0, k * TM + r] * nch, nch)
                    dst = pl.multiple_of((k * TM + r) * nch, nch)
                    pltpu.make_async_copy(y_ref.at[pl.ds(src, nch)], ybuf.at[buf, pl.ds(dst, nch)], sem.at[buf]).start()
            return carry

        lax.fori_loop(0, TM // GATHER_UNROLL, body, 0)

    @pl.when(step == 0)
    def _():
        start(slot_ref, 0)

    @pl.when(step + 1 < n_steps)
    def _():
        start(next_slot_ref, 1 - cur)

    pltpu.make_async_copy(y_ref.at[pl.ds(0, TOP_K * TM * nch)], ybuf.at[cur], sem.at[cur]).wait()
    rw = rw_ref[...]
    acc = None
    for k in range(TOP_K):
        yk = jnp.concatenate([ybuf[cur, pl.ds(k * TM * nch + c, TM, stride=nch), :] for c in range(nch)], axis=-1)
        term = rw[:, k:k + 1] * yk
        acc = term if acc is None else acc + term
    return acc


def _pre_attn_kernel(has_moe, *refs):
    if has_moe:
        slot_ref, next_slot_ref, y_ref, rw_ref, pmod_ref = refs[:5]
        x2_ref, ybuf, sem = refs[-3:]
        refs = refs[5:-3]
    (x_ref, mod_ref, g_ref, win_ref, qn_ref, wuq_ref, kvn_ref, wukv_ref,
     mc_ref, ml_ref, mh_ref, dc_ref, dl_ref, dh_ref, wc_ref, wl_ref, wh_ref,
     mq_ref, mk_ref, mv_ref, dq_ref, dk_ref, dv_ref, sq_ref, sk_ref, sv_ref) = refs
    x = x_ref[0]
    if has_moe:
        step = pl.program_id(0) * pl.num_programs(1) + pl.program_id(1)
        n_steps = pl.num_programs(0) * pl.num_programs(1)
        x = x + pmod_ref[0, 0][5:6] * _moe_combine(step, n_steps, slot_ref, next_slot_ref, y_ref, rw_ref, ybuf, sem)
        x2_ref[0] = x
    mod = mod_ref[0, 0]
    h = _modulate(x, g_ref[0], mod[0:1], mod[1:2])
    p = jnp.dot(h.astype(BF16), win_ref[0], preferred_element_type=F32)

    def grp(g, n=1):
        return p[:, g * LANES:(g + n) * LANES]

    mla_scale = LOG2E * (MLA_NOPE + MLA_ROPE) ** -0.5
    diff_scale = LOG2E * DIFF_QK ** -0.5
    swa_scale = LOG2E * SWA_DIM ** -0.5
    mla_tab = (mc_ref[...], ml_ref[...], mh_ref[...])
    diff_tab = (dc_ref[...], dl_ref[...], dh_ref[...])
    swa_tab = (wc_ref[...], wl_ref[...], wh_ref[...])

    cq = (_rms(grp(G_CQ, 2)) * qn_ref[0]).astype(BF16)
    q = jnp.dot(cq, wuq_ref[0], preferred_element_type=F32)
    ckv = (_rms(grp(G_CKV)) * kvn_ref[0]).astype(BF16)
    kv = jnp.dot(ckv, wukv_ref[0], preferred_element_type=F32)
    k_rope = _rope(grp(G_KROPE), *mla_tab, MLA_ROPE // 2)
    for hd in range(MLA_HEADS):
        sl = slice(hd * LANES, (hd + 1) * LANES)
        mq_ref[0, :, sl] = (_rope(q[:, sl], *mla_tab, MLA_ROPE // 2) * mla_scale).astype(BF16)
        mk_ref[0, :, sl] = (kv[:, sl] + k_rope).astype(BF16)
    def store_values(ref, vals):
        for g in range(2):
            ref[0, :, 2 * g * LANES:(2 * g + 1) * LANES] = vals[:, g * LANES:(g + 1) * LANES].astype(BF16)
            ref[0, :, (2 * g + 1) * LANES:(2 * g + 2) * LANES] = jnp.ones((TM, LANES), BF16)

    store_values(mv_ref, kv[:, MLA_HEADS * LANES:])

    for g in range(2):
        sl = slice(g * LANES, (g + 1) * LANES)
        dq_ref[0, :, sl] = (_rope(grp(G_DQ + g), *diff_tab, DIFF_QK // 2) * diff_scale).astype(BF16)
        dk_ref[0, :, sl] = _rope(grp(G_DK + g), *diff_tab, DIFF_QK // 2).astype(BF16)
    store_values(dv_ref, grp(G_DV, 2))

    for g in range(4):
        sl = slice(g * LANES, (g + 1) * LANES)
        sq_ref[0, :, sl] = (_rope(grp(G_SQ + g), *swa_tab, SWA_DIM // 2) * swa_scale).astype(BF16)
    for g in range(2):
        sl = slice(g * LANES, (g + 1) * LANES)
        sk_ref[0, :, sl] = _rope(grp(G_SK + g), *swa_tab, SWA_DIM // 2).astype(BF16)
    store_values(sv_ref, grp(G_SV, 2))


def _slot_tiles(slots):
    t = slots.shape[0]
    return slots.reshape(t // TM, TM, TOP_K).transpose(0, 2, 1).reshape(t // TM, 1, TOP_K * TM)


def _pre_attn(layer, xs, mods, g_attn, w_in_p, q_norm, w_uq_p, kv_norm, w_ukv_p, tables, n_ctx, moe=None):
    b, nt, d = xs.shape
    tiles = nt // TM
    n_ctx_tiles = n_ctx // TM
    ctx_row = b

    def tok(w):
        return pl.BlockSpec((1, TM, w), lambda j, bi: (bi, j, 0))

    def lay(shape):
        return pl.BlockSpec((1,) + shape, lambda j, bi: (layer,) + (0,) * len(shape))

    def mod_spec(lyr):
        return pl.BlockSpec((1, 1, N_MOD, d), lambda j, bi: (lyr, jnp.where(j < n_ctx_tiles, ctx_row, bi), 0, 0))

    tab = pl.BlockSpec((TM, LANES), lambda j, bi: (j, 0))
    widths = (512, 512, 512, 256, 256, 512, 512, 256, 512)
    in_specs = [tok(d), mod_spec(layer), lay((1, d)), lay(w_in_p.shape[1:]), lay((1, MLA_Q_RANK)),
                lay(w_uq_p.shape[1:]), lay((1, MLA_KV_RANK)), lay(w_ukv_p.shape[1:])] + [tab] * 9
    out_specs = [tok(w) for w in widths]
    out_shape = [jax.ShapeDtypeStruct((b, nt, w), BF16) for w in widths]
    args = [xs, mods, g_attn, w_in_p, q_norm, w_uq_p, kv_norm, w_ukv_p, *tables]
    scratch = []
    if moe is not None:
        y_rows, slots, rw = moe
        nch = d // LANES

        def next_tile(j, bi):
            wrap = bi + 1 == b
            nj = jnp.minimum(jnp.where(wrap, j + 1, j), tiles - 1)
            return (jnp.where(wrap, 0, bi + 1) * tiles + nj, 0, 0)

        slot_block = (1, 1, TOP_K * TM)
        in_specs = [pl.BlockSpec(slot_block, lambda j, bi: (bi * tiles + j, 0, 0), memory_space=pltpu.SMEM),
                    pl.BlockSpec(slot_block, next_tile, memory_space=pltpu.SMEM),
                    pl.BlockSpec(memory_space=pl.ANY),
                    pl.BlockSpec((TM, LANES), lambda j, bi: (bi * tiles + j, 0)),
                    mod_spec(layer - 1)] + in_specs
        slot_tiles = _slot_tiles(slots)
        args = [slot_tiles, slot_tiles, y_rows, rw, mods] + args
        out_specs = out_specs + [tok(d)]
        out_shape = out_shape + [jax.ShapeDtypeStruct((b, nt, d), F32)]
        scratch = [pltpu.VMEM((2, TOP_K * TM * nch, LANES), F32), pltpu.SemaphoreType.DMA((2,))]
    outs = pl.pallas_call(
        functools.partial(_pre_attn_kernel, moe is not None),
        grid=(tiles, b),
        in_specs=in_specs,
        out_specs=out_specs,
        out_shape=out_shape,
        scratch_shapes=scratch,
        compiler_params=_params("arbitrary", "arbitrary"),
        name="pre_attn",
    )(*args)
    return (outs[-1], outs[:-1]) if moe is not None else (xs, outs)


def _softmax_pv(s, v):
    p = jnp.exp2(s - jnp.max(s, axis=-1, keepdims=True))
    ov = jnp.dot(p.astype(BF16), v, preferred_element_type=F32)
    return ov[:, :LANES] / ov[:, LANES:]


def _dense_steps(n_ctx, nt, run):
    j = pl.program_id(1)

    @pl.when(j == 0)
    def _():
        run(0, n_ctx, n_ctx)

    @pl.when(j > 0)
    def _():
        run(pl.multiple_of(n_ctx + (j - 1) * TQ, TM), TQ, nt)


def _mla_attn_kernel(n_ctx, q_ref, k_ref, v_ref, o_ref):
    def run(row0, rows, nk):
        lane = _lane_iota((rows, LANES))
        outs = []
        for hd in range(MLA_HEADS):
            sl = slice(hd * LANES, (hd + 1) * LANES)
            vs = slice((hd // 2) * 2 * LANES, (hd // 2 + 1) * 2 * LANES)
            s = _qk(q_ref[0, pl.ds(row0, rows), sl], k_ref[0, :nk, sl])
            outs.append(_softmax_pv(s, v_ref[0, :nk, vs]))
        for g in range(2):
            o_ref[0, pl.ds(row0, rows), g * LANES:(g + 1) * LANES] = jnp.where(
                lane < MLA_V, outs[2 * g], outs[2 * g + 1]).astype(BF16)

    _dense_steps(n_ctx, k_ref.shape[1], run)


def _diff_attn_kernel(n_ctx, lambda_init, q_ref, k_ref, v_ref, lam_ref, g_ref, seg_ref, o_ref):
    lam = lam_ref[0]
    lam_full = (jnp.exp(jnp.sum(lam[0:1] * lam[1:2], axis=-1, keepdims=True))
                - jnp.exp(jnp.sum(lam[2:3] * lam[3:4], axis=-1, keepdims=True)) + lambda_init)

    def run(row0, rows, nk):
        lane = _lane_iota((rows, LANES))
        heads = []
        for hd in range(DIFF_HEADS):
            sl = slice((hd // 2) * LANES, (hd // 2 + 1) * LANES)
            q = q_ref[0, pl.ds(row0, rows), sl]
            k = k_ref[0, :nk, sl]
            v = v_ref[0, :nk, (hd // 2) * 2 * LANES:(hd // 2 + 1) * 2 * LANES]
            a = []
            for comp in range(2):
                seg = (hd % 2) * 2 + comp
                qm = jnp.where((lane >= seg * DIFF_QK) & (lane < (seg + 1) * DIFF_QK), q, jnp.zeros_like(q))
                a.append(_softmax_pv(_qk(qm, k), v))
            heads.append(a[0] - lam_full * a[1])
        for g in range(2):
            o = jnp.where(lane < DIFF_V, heads[2 * g], heads[2 * g + 1])
            ms = jnp.dot(o * o, seg_ref[...], precision=HIGHEST, preferred_element_type=F32) * (1.0 / DIFF_V)
            o = o * lax.rsqrt(ms + EPS) * g_ref[...] * (1.0 - lambda_init)
            o_ref[0, pl.ds(row0, rows), g * LANES:(g + 1) * LANES] = o.astype(BF16)

    _dense_steps(n_ctx, k_ref.shape[1], run)


def _swa_attn_kernel(n_ctx, q_ref, k_ref, v_ref, sink_ref, o_ref):
    j = pl.program_id(1)
    nt = k_ref.shape[1]
    band = TM + 2 * WINDOW
    lane = _lane_iota((TM, LANES))

    def run(kv, k, v, allowed):
        for g in (2 * kv, 2 * kv + 1):
            q = q_ref[0, :, g * LANES:(g + 1) * LANES]
            outs = []
            for half in range(2):
                qm = jnp.where((lane >= half * SWA_DIM) & (lane < (half + 1) * SWA_DIM), q, jnp.zeros_like(q))
                s = _qk(qm, k)
                if allowed is not None:
                    s = jnp.where(allowed, s, NEG_INF)
                sink = sink_ref[0, 2 * g + half:2 * g + half + 1, 0:1] * LOG2E
                m = jnp.maximum(jnp.max(s, axis=-1, keepdims=True), sink)
                p = jnp.exp2(s - m)
                ov = jnp.dot(p.astype(BF16), v, preferred_element_type=F32)
                outs.append(ov[:, :LANES] / (ov[:, LANES:] + jnp.exp2(sink - m)))
            o_ref[0, :, g * LANES:(g + 1) * LANES] = jnp.where(lane < SWA_DIM, outs[0], outs[1]).astype(BF16)

    @pl.when(j < n_ctx // TM)
    def _():
        for kv in range(SWA_KV_HEADS):
            ks = slice(kv * LANES, (kv + 1) * LANES)
            vs = slice(2 * kv * LANES, (2 * kv + 2) * LANES)
            run(kv, k_ref[0, :n_ctx, ks], v_ref[0, :n_ctx, vs], None)

    @pl.when(j >= n_ctx // TM)
    def _():
        q0 = j * TM
        w0 = pl.multiple_of(jnp.clip(q0 - WINDOW, n_ctx, nt - band), WINDOW)
        q_pos = q0 + lax.broadcasted_iota(I32, (TM, n_ctx + band), 0)
        col = lax.broadcasted_iota(I32, (TM, n_ctx + band), 1)
        k_pos = w0 + col - n_ctx
        allowed = (col < n_ctx) | (jnp.abs(k_pos - q_pos) <= WINDOW)
        for kv in range(SWA_KV_HEADS):
            ks = slice(kv * LANES, (kv + 1) * LANES)
            vs = slice(2 * kv * LANES, (2 * kv + 2) * LANES)
            k = jnp.concatenate([k_ref[0, :n_ctx, ks], k_ref[0, pl.ds(w0, band), ks]], axis=0)
            v = jnp.concatenate([v_ref[0, :n_ctx, vs], v_ref[0, pl.ds(w0, band), vs]], axis=0)
            run(kv, k, v, allowed)


def _attention(kernel, q, k, v, extra, out_width, name, dense_steps=None):
    b, nt, _ = q.shape

    def whole(width):
        return pl.BlockSpec((1, nt, width), lambda bi, j: (bi, 0, 0))

    def tile(width):
        return pl.BlockSpec((1, TM, width), lambda bi, j: (bi, j, 0))

    q_spec, o_spec, steps = (tile, tile, nt // TM) if dense_steps is None else (whole, whole, dense_steps)
    extra_specs = [pl.BlockSpec(e.shape, lambda bi, j, nd=e.ndim: (0,) * nd) for e in extra]
    return pl.pallas_call(
        kernel,
        grid=(b, steps),
        in_specs=[q_spec(q.shape[2]), whole(k.shape[2]), whole(v.shape[2])] + extra_specs,
        out_specs=o_spec(out_width),
        out_shape=jax.ShapeDtypeStruct((b, nt, out_width), BF16),
        compiler_params=_params("arbitrary", "arbitrary"),
        name=name,
    )(q, k, v, *extra)


def _post_attn_kernel(cap, x_ref, mod_ref, mla_ref, diff_ref, swa_ref, wout_ref, g_ref, wr_ref, br_ref,
                      x1_ref, route_ref, rw_ref, cnt_ref, xs_ref, carry_ref, fbuf, slot_v, slot_s, dsem, ssem):
    step = pl.program_id(0) * pl.num_programs(1) + pl.program_id(1)
    n_steps = pl.num_programs(0) * pl.num_programs(1)
    cur, prev, prev2 = step % 3, (step + 2) % 3, (step + 1) % 3
    par = step % 2
    nch = fbuf.shape[1] // TM

    def slots_landed(sbuf):
        pltpu.make_async_copy(slot_v, slot_s.at[sbuf], ssem.at[sbuf]).wait()

    def scatter(buf, sbuf):
        def body(i, carry):
            for u in range(GATHER_UNROLL):
                r = i * GATHER_UNROLL + u
                src = fbuf.at[buf, pl.ds(pl.multiple_of(r * nch, nch), nch)]
                for k in range(TOP_K):
                    dst = pl.multiple_of(slot_s[sbuf, k, r] * nch, nch)
                    pltpu.make_async_copy(src, xs_ref.at[pl.ds(dst, nch)], dsem.at[buf]).start()
            return carry

        lax.fori_loop(0, TM // GATHER_UNROLL, body, 0)

    def drain(buf):
        for _ in range(TOP_K):
            pltpu.make_async_copy(fbuf.at[buf], xs_ref.at[pl.ds(0, TM * nch)], dsem.at[buf]).wait()

    @pl.when(step == 0)
    def _():
        carry_ref[...] = jnp.zeros_like(carry_ref)

    @pl.when(step >= 3)
    def _():
        drain(cur)

    x = x_ref[0]
    mod = mod_ref[0, 0]
    a = jnp.concatenate([mla_ref[0], diff_ref[0], swa_ref[0]], axis=-1)
    x1 = x + mod[2:3] * jnp.dot(a, wout_ref[0], preferred_element_type=F32)
    x1_ref[0] = x1
    f = _modulate(x1, g_ref[0], mod[3:4], mod[4:5])
    for c in range(nch):
        fbuf[cur, pl.ds(c, TM, stride=nch), :] = f[:, c * LANES:(c + 1) * LANES]

    logits = jnp.dot(f.astype(BF16), wr_ref[0], preferred_element_type=F32) + br_ref[0]
    lane = _lane_iota((TM, LANES))
    vals, hots = [], []
    for _ in range(TOP_K):
        m = jnp.max(logits, axis=-1, keepdims=True)
        idx = jnp.min(jnp.where(logits == m, lane, LANES), axis=-1, keepdims=True)
        hot = lane == idx
        logits = jnp.where(hot, -3e38, logits)
        vals.append(m)
        hots.append((idx, hot))
    es = [jnp.exp(v - vals[0]) for v in vals]
    denom = functools.reduce(jnp.add, es)
    sel = functools.reduce(jnp.add, [jnp.where(hot, 1.0, 0.0) for _, hot in hots])

    r_io = lax.broadcasted_iota(I32, (TM, TM), 0)
    c_io = lax.broadcasted_iota(I32, (TM, TM), 1)
    tril = jnp.where(c_io < r_io, 1.0, 0.0).astype(BF16)
    rank = carry_ref[0:1, :] + jnp.dot(tril, sel.astype(BF16), preferred_element_type=F32)
    carry_ref[0:1, :] = carry_ref[0:1, :] + jnp.sum(sel, axis=0, keepdims=True)
    cnt_ref[...] = jnp.broadcast_to(carry_ref[0:1, :], cnt_ref.shape)

    route = jnp.zeros((TM, LANES), I32)
    slots = jnp.zeros((TM, LANES), I32)
    rw = jnp.zeros((TM, LANES), F32)
    for k, (idx, hot) in enumerate(hots):
        rk = jnp.sum(jnp.where(hot, rank, 0.0), axis=-1, keepdims=True).astype(I32)
        route = jnp.where(lane == k, idx, route)
        route = jnp.where(lane == TOP_K + k, rk, route)
        slots = jnp.where(lane == k, idx * cap + rk, slots)
        rw = jnp.where(lane == k, es[k] / denom, rw)
    route_ref[...] = route
    rw_ref[...] = rw
    @pl.when(step >= 1)
    def _():
        slots_landed(1 - par)

    slot_v[...] = slots.T[0:SUBLANES, :]
    pltpu.make_async_copy(slot_v, slot_s.at[par], ssem.at[par]).start()

    @pl.when(step >= 1)
    def _():
        scatter(prev, 1 - par)

    @pl.when(step == n_steps - 1)
    def _():
        slots_landed(par)
        scatter(cur, par)
        drain(prev2)
        drain(prev)
        drain(cur)


def _post_attn(layer, xs, mods, mla_o, diff_o, swa_o, w_out_b, g_ffn, w_router_p, b_router_p, n_ctx, n_exp, cap):
    b, nt, d = xs.shape
    t = b * nt
    nch = d // LANES
    n_ctx_tiles = n_ctx // TM
    tiles = nt // TM

    def tok(w):
        return pl.BlockSpec((1, TM, w), lambda bi, j: (bi, j, 0))

    def lay(shape):
        return pl.BlockSpec((1,) + shape, lambda bi, j: (layer,) + (0,) * len(shape))

    def flat(rows, w):
        return pl.BlockSpec((rows, w), lambda bi, j: (bi * tiles + j, 0))

    mod_spec = pl.BlockSpec((1, 1, N_MOD, d), lambda bi, j: (layer, jnp.where(j < n_ctx_tiles, b, bi), 0, 0))
    return pl.pallas_call(
        functools.partial(_post_attn_kernel, cap),
        grid=(b, tiles),
        in_specs=[tok(d), mod_spec, tok(256), tok(256), tok(512), lay(w_out_b.shape[1:]), lay((1, d)),
                  lay((d, LANES)), lay((1, LANES))],
        out_specs=[tok(d), flat(TM, LANES), flat(TM, LANES),
                   pl.BlockSpec((SUBLANES, LANES), lambda bi, j: (0, 0)), pl.BlockSpec(memory_space=pl.ANY)],
        out_shape=[jax.ShapeDtypeStruct((b, nt, d), F32),
                   jax.ShapeDtypeStruct((t, LANES), I32), jax.ShapeDtypeStruct((t, LANES), F32),
                   jax.ShapeDtypeStruct((SUBLANES, LANES), F32),
                   jax.ShapeDtypeStruct((n_exp * cap * nch, LANES), F32)],
        scratch_shapes=[pltpu.VMEM((SUBLANES, LANES), F32), pltpu.VMEM((3, TM * nch, LANES), F32),
                        pltpu.VMEM((SUBLANES, TM), I32), pltpu.SMEM((2, SUBLANES, TM), I32),
                        pltpu.SemaphoreType.DMA((3,)), pltpu.SemaphoreType.DMA((2,))],
        compiler_params=_params("arbitrary", "arbitrary"),
        name="post_attn",
    )(xs, mods, mla_o, diff_o, swa_o, w_out_b, g_ffn, w_router_p, b_router_p)


def _pad_zero_kernel(n_exp, nch, start_ref, len_ref, xs_in_ref, xs_ref, zbuf, sem):
    del xs_in_ref
    zbuf[...] = jnp.zeros_like(zbuf)
    bits = (TMOE - 1).bit_length()
    for wait in (False, True):
        for e in range(n_exp):
            for bit in range(bits):
                size = 1 << bit

                @pl.when(((len_ref[e] >> bit) & 1) == 1)
                def _():
                    below = len_ref[e] & (size - 1)
                    dst = pl.multiple_of((start_ref[e] + below) * nch, nch)
                    cp = pltpu.make_async_copy(zbuf.at[pl.ds(0, size * nch)], xs_ref.at[pl.ds(dst, size * nch)], sem)
                    cp.wait() if wait else cp.start()


def _pad_zero(xs_rows, pad_start, pad_len, n_exp, nch):
    return pl.pallas_call(
        functools.partial(_pad_zero_kernel, n_exp, nch),
        grid_spec=pltpu.PrefetchScalarGridSpec(
            num_scalar_prefetch=2,
            grid=(1,),
            in_specs=[pl.BlockSpec(memory_space=pl.ANY)],
            out_specs=pl.BlockSpec(memory_space=pl.ANY),
            scratch_shapes=[pltpu.VMEM((TMOE // 2 * nch, LANES), F32), pltpu.SemaphoreType.DMA]),
        out_shape=jax.ShapeDtypeStruct(xs_rows.shape, F32),
        input_output_aliases={2: 0},
        compiler_params=_params("arbitrary"),
        name="pad_zero",
    )(pad_start, pad_len, xs_rows)


def _moe_kernel(nch, te_ref, tb_ref, nu_ref, x_ref, wg_ref, bg_ref, wu_ref, bu_ref, wd_ref, bd_ref, y_ref,
                wg_b, wu_b, wd_b):
    del tb_ref
    i = pl.program_id(0)

    @pl.when((i == 0) | (te_ref[i] != te_ref[jnp.maximum(i - 1, 0)]))
    def _():
        wg_b[...] = wg_ref[0].astype(BF16)
        wu_b[...] = wu_ref[0].astype(BF16)
        wd_b[...] = wd_ref[0].astype(BF16)

    @pl.when(i < nu_ref[0])
    def _():
        x = jnp.concatenate([x_ref[pl.ds(c, TMOE, stride=nch), :] for c in range(nch)], axis=-1).astype(BF16)
        gate = jnp.dot(x, wg_b[...], preferred_element_type=F32) + bg_ref[0]
        up = jnp.dot(x, wu_b[...], preferred_element_type=F32) + bu_ref[0]
        gate = jnp.minimum(gate, SWIGLU_LIMIT)
        up = jnp.clip(up, -SWIGLU_LIMIT, SWIGLU_LIMIT)
        act = gate * (1.0 / (1.0 + jnp.exp(-SWIGLU_ALPHA * gate))) * (up + 1.0)
        y = jnp.dot(act.astype(BF16), wd_b[...], preferred_element_type=F32) + bd_ref[0]
        for c in range(nch):
            y_ref[pl.ds(c, TMOE, stride=nch), :] = y[:, c * LANES:(c + 1) * LANES]


def _moe(layer, xs_rows, tile_e, tile_blk, n_used, n_tiles, wg, bg, wu, bu, wd, bd, n_exp):
    d, f = wg.shape[1], wg.shape[2]
    nch = d // LANES

    def rows(i, te, tb, nu):
        return (tb[i], 0)

    def exp(i, te, tb, nu):
        return (layer * n_exp + te[i], 0, 0)

    return pl.pallas_call(
        functools.partial(_moe_kernel, nch),
        grid_spec=pltpu.PrefetchScalarGridSpec(
            num_scalar_prefetch=3,
            grid=(n_tiles,),
            in_specs=[pl.BlockSpec((TMOE * nch, LANES), rows),
                      pl.BlockSpec((1, d, f), exp), pl.BlockSpec((1, 1, f), exp),
                      pl.BlockSpec((1, d, f), exp), pl.BlockSpec((1, 1, f), exp),
                      pl.BlockSpec((1, f, d), exp), pl.BlockSpec((1, 1, d), exp)],
            out_specs=pl.BlockSpec((TMOE * nch, LANES), rows),
            scratch_shapes=[pltpu.VMEM((d, f), BF16), pltpu.VMEM((d, f), BF16), pltpu.VMEM((f, d), BF16)]),
        out_shape=jax.ShapeDtypeStruct(xs_rows.shape, F32),
        compiler_params=_params("arbitrary"),
        name="moe",
    )(tile_e, tile_blk, n_used, xs_rows, wg, bg, wu, bu, wd, bd)


def _final_kernel(slot_ref, next_slot_ref, y_ref, rw_ref, mod_ref, x1_ref, g_ref, o_ref, ybuf, sem):
    step = pl.program_id(0) * pl.num_programs(1) + pl.program_id(1)
    n_steps = pl.num_programs(0) * pl.num_programs(1)
    x = x1_ref[0] + mod_ref[0, 0][5:6] * _moe_combine(step, n_steps, slot_ref, next_slot_ref, y_ref, rw_ref, ybuf, sem)
    o_ref[0] = _rms(x) * g_ref[...]


def _final_norm(layer, x1, y_rows, slots, rw, mods, g_final, n_ctx, s_len):
    b, nt, d = x1.shape
    nch = d // LANES
    tiles = nt // TM
    lat_tiles = s_len // TM
    off = n_ctx // TM

    def next_tile(bi, j):
        wrap = j + 1 == lat_tiles
        nb = jnp.minimum(jnp.where(wrap, bi + 1, bi), b - 1)
        return (nb * tiles + off + jnp.where(wrap, 0, j + 1), 0, 0)

    slot_block = (1, 1, TOP_K * TM)
    slot_tiles = _slot_tiles(slots)
    return pl.pallas_call(
        _final_kernel,
        grid=(b, lat_tiles),
        in_specs=[pl.BlockSpec(slot_block, lambda bi, j: (bi * tiles + off + j, 0, 0), memory_space=pltpu.SMEM),
                  pl.BlockSpec(slot_block, next_tile, memory_space=pltpu.SMEM),
                  pl.BlockSpec(memory_space=pl.ANY),
                  pl.BlockSpec((TM, LANES), lambda bi, j: (bi * tiles + off + j, 0)),
                  pl.BlockSpec((1, 1, N_MOD, d), lambda bi, j: (layer, bi, 0, 0)),
                  pl.BlockSpec((1, TM, d), lambda bi, j: (bi, j + off, 0)),
                  pl.BlockSpec((1, d), lambda bi, j: (0, 0))],
        out_specs=pl.BlockSpec((1, TM, d), lambda bi, j: (bi, j, 0)),
        out_shape=jax.ShapeDtypeStruct((b, s_len, d), F32),
        scratch_shapes=[pltpu.VMEM((2, TOP_K * TM * nch, LANES), F32), pltpu.SemaphoreType.DMA((2,))],
        compiler_params=_params("arbitrary", "arbitrary"),
        name="final_norm",
    )(slot_tiles, slot_tiles, y_rows, rw, mods, x1, g_final.reshape(1, d))


def _routing_plan(route, counts, n_exp, n_tiles, cap):
    slots = route[:, 0:TOP_K] * cap + route[:, TOP_K:2 * TOP_K]
    counts = counts[0, :n_exp].astype(I32)
    e_tiles = (counts + TMOE - 1) // TMOE
    ends = jnp.cumsum(e_tiles)
    n_used = ends[-1].reshape(1)
    tile_ids = jnp.minimum(jnp.arange(n_tiles, dtype=I32), n_used[0] - 1)
    tile_e = jnp.minimum(jnp.sum((tile_ids[:, None] >= ends[None, :]).astype(I32), axis=-1), n_exp - 1)
    first = jnp.sum(jnp.where(tile_e[:, None] == jnp.arange(n_exp, dtype=I32), (ends - e_tiles)[None, :], 0), axis=-1)
    tile_blk = tile_e * (cap // TMOE) + tile_ids - first
    pad_start = jnp.arange(n_exp, dtype=I32) * cap + counts
    pad_len = e_tiles * TMOE - counts
    return slots, tile_e, tile_blk, n_used, pad_start, pad_len


def kernel(x, c, ctx, c_ctx, w_ada, b_ada, g_attn, w_in, mla_q_norm, mla_w_uq, mla_kv_norm, mla_w_ukv,
           diff_lambda, diff_subln, swa_sink, w_out, g_ffn, w_router, b_router, w_gate, b_gate, w_up, b_up,
           w_down, b_down, g_final):
    b, s_len, d = x.shape
    n_ctx = ctx.shape[1]
    n_layers = w_ada.shape[0]
    n_exp = w_router.shape[2]
    nt = n_ctx + s_len
    t = b * nt
    assert d % LANES == 0 and n_ctx % TM == 0 and s_len % TQ == 0 and t % TMOE == 0 and s_len % GRID_W == 0
    assert n_exp <= LANES and s_len >= TM + 2 * WINDOW and t // TM >= 3

    w_in_p = _gather_columns(w_in, _in_proj_columns()).astype(BF16)
    w_uq_p = _gather_columns(mla_w_uq, _uq_columns()).astype(BF16)
    w_ukv_p = _gather_columns(mla_w_ukv, _ukv_columns()).astype(BF16)
    w_out_b = w_out.astype(BF16)
    w_router_p = jnp.pad(w_router, ((0, 0), (0, 0), (0, LANES - n_exp))).astype(BF16)
    b_router_p = jnp.pad(b_router, ((0, 0), (0, LANES - n_exp)), constant_values=NEG_INF).reshape(n_layers, 1, LANES)
    wg = w_gate.reshape(n_layers * n_exp, d, -1)
    wu = w_up.reshape(n_layers * n_exp, d, -1)
    wd = w_down.reshape(n_layers * n_exp, -1, d)
    bg = b_gate.reshape(n_layers * n_exp, 1, -1)
    bu = b_up.reshape(n_layers * n_exp, 1, -1)
    bd = b_down.reshape(n_layers * n_exp, 1, d)
    lam_p = jnp.pad(diff_lambda, ((0, 0), (0, 0), (0, LANES - DIFF_QK)))
    subln_p = jnp.tile(diff_subln, (1, LANES // DIFF_V)).reshape(n_layers, 1, LANES)
    sink_p = jnp.broadcast_to(swa_sink[:, :, None], (n_layers, SWA_HEADS, LANES))
    seg = jnp.asarray((np.arange(LANES)[:, None] // DIFF_V == np.arange(LANES)[None, :] // DIFF_V), F32)
    tables = _rope_tables(n_ctx, s_len)

    ada_rows = -(-(b + 1) // SUBLANES) * SUBLANES
    cc = jnp.concatenate([c, c_ctx[None, :], jnp.zeros((ada_rows - b - 1, d), F32)], axis=0)
    mods = _ada(cc, w_ada, b_ada).reshape(n_layers, ada_rows, N_MOD, d)

    cap = t
    n_tiles = (t * TOP_K) // TMOE + n_exp
    xs = jnp.concatenate([ctx, x], axis=1)
    moe = None
    for layer in range(n_layers):
        lambda_init = 0.8 - 0.6 * math.exp(-0.3 * layer)
        xs, (mq, mk, mv, dq, dk, dv, sq, sk, sv) = _pre_attn(
            layer, xs, mods, g_attn.reshape(n_layers, 1, d), w_in_p, mla_q_norm.reshape(n_layers, 1, -1), w_uq_p,
            mla_kv_norm.reshape(n_layers, 1, -1), w_ukv_p, tables, n_ctx, moe)
        dense_steps = 1 + s_len // TQ
        mla_o = _attention(functools.partial(_mla_attn_kernel, n_ctx), mq, mk, mv, [], 256, "mla_attn", dense_steps)
        diff_o = _attention(functools.partial(_diff_attn_kernel, n_ctx, lambda_init), dq, dk, dv,
                            [lam_p[layer:layer + 1], subln_p[layer], seg], 256, "diff_attn", dense_steps)
        swa_o = _attention(functools.partial(_swa_attn_kernel, n_ctx), sq, sk, sv, [sink_p[layer:layer + 1]],
                           512, "swa_attn")
        xs, route, rw, counts, xs_rows = _post_attn(
            layer, xs, mods, mla_o, diff_o, swa_o, w_out_b, g_ffn.reshape(n_layers, 1, d), w_router_p, b_router_p,
            n_ctx, n_exp, cap)
        slots, tile_e, tile_blk, n_used, pad_start, pad_len = _routing_plan(route, counts, n_exp, n_tiles, cap)
        xs_rows = _pad_zero(xs_rows, pad_start, pad_len, n_exp, d // LANES)
        moe = (_moe(layer, xs_rows, tile_e, tile_blk, n_used, n_tiles, wg, bg, wu, bu, wd, bd, n_exp), slots, rw)
    return _final_norm(n_layers - 1, xs, *moe, mods, g_final, n_ctx, s_len)
```

```python
import functools
import math

import jax
import jax.numpy as jnp
import numpy as np
from jax import lax
from jax.experimental import pallas as pl
from jax.experimental.pallas import tpu as pltpu

F32 = jnp.float32
BF16 = jnp.bfloat16
I32 = jnp.int32
HIGHEST = lax.Precision.HIGHEST
LOG2E = math.log2(math.e)

LANES = 128
SUBLANES = 8
VMEM_LIMIT = 56 * 1024 * 1024

GRID_W = 64
ROPE_THETA = 10000.0
EPS = 1e-6
NEG_INF = -1e30
N_MOD = 6

MLA_HEADS, MLA_Q_RANK, MLA_KV_RANK, MLA_NOPE, MLA_ROPE, MLA_V = 4, 256, 128, 64, 32, 64
DIFF_HEADS, DIFF_QK = 4, 32
DIFF_V = 2 * DIFF_QK
SWA_HEADS, SWA_KV_HEADS, SWA_DIM, WINDOW = 8, 2, 64, 128
TOP_K = 4
SWIGLU_LIMIT = 7.0
SWIGLU_ALPHA = 1.702

TM = 256
TQ = 512
TMOE = 512
GATHER_UNROLL = 8

G_CQ, G_CKV, G_KROPE, G_DQ, G_DK, G_DV, G_SQ, G_SK, G_SV, N_GROUPS = 0, 2, 3, 4, 6, 8, 10, 14, 16, 18


def _params(*sem):
    return pltpu.CompilerParams(dimension_semantics=sem, vmem_limit_bytes=VMEM_LIMIT)


def _in_proj_columns():
    src = -np.ones(N_GROUPS * LANES, np.int64)
    o_ckv = MLA_Q_RANK
    o_kr = o_ckv + MLA_KV_RANK
    o_dq = o_kr + MLA_ROPE
    o_dk = o_dq + DIFF_HEADS * 2 * DIFF_QK
    o_dv = o_dk + DIFF_HEADS * 2 * DIFF_QK
    o_sq = o_dv + DIFF_HEADS * DIFF_V
    o_sk = o_sq + SWA_HEADS * SWA_DIM
    o_sv = o_sk + SWA_KV_HEADS * SWA_DIM
    src[G_CQ * LANES:G_CQ * LANES + MLA_Q_RANK] = np.arange(MLA_Q_RANK)
    src[G_CKV * LANES:G_CKV * LANES + MLA_KV_RANK] = o_ckv + np.arange(MLA_KV_RANK)
    src[G_KROPE * LANES + MLA_NOPE:G_KROPE * LANES + MLA_NOPE + MLA_ROPE] = o_kr + np.arange(MLA_ROPE)
    src[G_DQ * LANES:G_DQ * LANES + 256] = o_dq + np.arange(256)
    src[G_DK * LANES:G_DK * LANES + 256] = o_dk + np.arange(256)
    src[G_DV * LANES:G_DV * LANES + 256] = o_dv + np.arange(256)
    src[G_SQ * LANES:G_SQ * LANES + 512] = o_sq + np.arange(512)
    for kv in range(SWA_KV_HEADS):
        for half in range(2):
            lo = half * SWA_DIM
            src[(G_SK + kv) * LANES + lo:(G_SK + kv) * LANES + lo + SWA_DIM] = o_sk + kv * SWA_DIM + np.arange(SWA_DIM)
            src[(G_SV + kv) * LANES + lo:(G_SV + kv) * LANES + lo + SWA_DIM] = o_sv + kv * SWA_DIM + np.arange(SWA_DIM)
    return src


def _gather_columns(w, src):
    cols = jnp.take(w, jnp.asarray(np.maximum(src, 0), I32), axis=-1)
    return jnp.where(jnp.asarray(src >= 0), cols, 0.0)


def _uq_columns():
    src = -np.ones(MLA_HEADS * LANES, np.int64)
    hd = MLA_NOPE + MLA_ROPE
    for h in range(MLA_HEADS):
        src[h * LANES:h * LANES + hd] = h * hd + np.arange(hd)
    return src


def _ukv_columns():
    src = -np.ones(MLA_HEADS * LANES + MLA_HEADS * MLA_V, np.int64)
    hd = MLA_NOPE + MLA_V
    for h in range(MLA_HEADS):
        src[h * LANES:h * LANES + MLA_NOPE] = h * hd + np.arange(MLA_NOPE)
        src[MLA_HEADS * LANES + h * MLA_V:MLA_HEADS * LANES + (h + 1) * MLA_V] = h * hd + MLA_NOPE + np.arange(MLA_V)
    return src


def _rope_tables(n_ctx, s_len):
    rows = s_len // GRID_W

    def axial(rot_dim):
        n_freq = rot_dim // 4
        inv_freq = ROPE_THETA ** (-jnp.arange(n_freq, dtype=F32) / n_freq)
        row_pos = jnp.repeat(jnp.arange(rows, dtype=F32), GRID_W)
        col_pos = jnp.tile(jnp.arange(GRID_W, dtype=F32), rows)
        ang = jnp.concatenate([row_pos[:, None] * inv_freq, col_pos[:, None] * inv_freq], axis=-1)
        return jnp.cos(ang), jnp.sin(ang)

    def expand(cos, sin, lane_rot):
        half = cos.shape[1]
        lane_rot = np.asarray(lane_rot)
        idx = np.maximum(lane_rot, 0) % half
        is_rot = lane_rot >= 0
        lo = is_rot & (lane_rot < half)
        hi = is_rot & (lane_rot >= half)
        c = jnp.where(jnp.asarray(is_rot), cos[:, idx], 1.0)
        s_lo = jnp.where(jnp.asarray(lo), -sin[:, idx], 0.0)
        s_hi = jnp.where(jnp.asarray(hi), sin[:, idx], 0.0)
        ident = [jnp.ones((n_ctx, LANES), F32), jnp.zeros((n_ctx, LANES), F32), jnp.zeros((n_ctx, LANES), F32)]
        return [jnp.concatenate([i, t], axis=0) for i, t in zip(ident, (c, s_lo, s_hi))]

    cos_r, sin_r = axial(MLA_ROPE)
    cos_w, sin_w = axial(SWA_DIM)
    lanes = np.arange(LANES)
    mla_rot = np.where((lanes >= MLA_NOPE) & (lanes < MLA_NOPE + MLA_ROPE), lanes - MLA_NOPE, -1)
    return (expand(cos_r, sin_r, mla_rot) + expand(cos_r, sin_r, lanes % DIFF_QK)
            + expand(cos_w, sin_w, lanes % SWA_DIM))


def _rms(x):
    return x * lax.rsqrt(jnp.mean(x * x, axis=-1, keepdims=True) + EPS)


def _modulate(x, g, shift, scale):
    return (_rms(x) * g) * (1.0 + scale) + shift


def _rope(v, c, s_lo, s_hi, half):
    return v * c + pltpu.roll(v, half, 1) * s_hi + pltpu.roll(v, LANES - half, 1) * s_lo


def _qk(q, k):
    return lax.dot_general(q, k, (((1,), (1,)), ((), ())), preferred_element_type=F32)


def _lane_iota(shape):
    return lax.broadcasted_iota(I32, shape, len(shape) - 1)


def _ada_kernel(c_ref, w_ref, b_ref, o_ref):
    c = c_ref[...]
    a = c * (1.0 / (1.0 + jnp.exp(-c)))
    o_ref[0] = jnp.dot(a, w_ref[0], precision=HIGHEST, preferred_element_type=F32) + b_ref[0]


def _ada(cc, w_ada, b_ada):
    n_layers, d, n_out = w_ada.shape
    rows = cc.shape[0]
    tn = d
    return pl.pallas_call(
        _ada_kernel,
        grid=(n_layers, n_out // tn),
        in_specs=[pl.BlockSpec((rows, d), lambda l, n: (0, 0)),
                  pl.BlockSpec((1, d, tn), lambda l, n: (l, 0, n)),
                  pl.BlockSpec((1, 1, tn), lambda l, n: (l, 0, n))],
        out_specs=pl.BlockSpec((1, rows, tn), lambda l, n: (l, 0, n)),
        out_shape=jax.ShapeDtypeStruct((n_layers, rows, n_out), F32),
        compiler_params=_params("arbitrary", "arbitrary"),
        name="ada",
    )(cc, w_ada, b_ada.reshape(n_layers, 1, n_out))


def _moe_combine(step, n_steps, slot_ref, next_slot_ref, y_ref, rw_ref, ybuf, sem):
    nch = ybuf.shape[1] // (TOP_K * TM)
    cur = step % 2

    def start(slots, buf):
        def body(i, carry):
            for u in range(GATHER_UNROLL):
                r = i * GATHER_UNROLL + u
                for k in range(TOP_K):
                    src = pl.multiple_of(slots[0, 0, k * TM + r] * nch, nch)
                    dst = pl.multiple_of((k * TM + r) * nch, nch)
                    pltpu.make_async_copy(y_ref.at[pl.ds(src, nch)], ybuf.at[buf, pl.ds(dst, nch)],
                                          sem.at[buf]).start(priority=k % 2)
            return carry

        lax.fori_loop(0, TM // GATHER_UNROLL, body, 0)

    @pl.when(step == 0)
    def _():
        start(slot_ref, 0)

    @pl.when(step + 1 < n_steps)
    def _():
        start(next_slot_ref, 1 - cur)

    pltpu.make_async_copy(y_ref.at[pl.ds(0, TOP_K * TM * nch)], ybuf.at[cur], sem.at[cur]).wait()
    rw = rw_ref[...]
    acc = None
    for k in range(TOP_K):
        yk = jnp.concatenate([ybuf[cur, pl.ds(k * TM * nch + c, TM, stride=nch), :] for c in range(nch)], axis=-1)
        term = rw[:, k:k + 1] * yk
        acc = term if acc is None else acc + term
    return acc


def _pre_attn_kernel(has_moe, *refs):
    if has_moe:
        slot_ref, next_slot_ref, y_ref, rw_ref, pmod_ref = refs[:5]
        x2_ref, ybuf, sem = refs[-3:]
        refs = refs[5:-3]
    (x_ref, mod_ref, g_ref, win_ref, qn_ref, wuq_ref, kvn_ref, wukv_ref,
     mc_ref, ml_ref, mh_ref, dc_ref, dl_ref, dh_ref, wc_ref, wl_ref, wh_ref,
     mq_ref, mk_ref, mv_ref, dq_ref, dk_ref, dv_ref, sq_ref, sk_ref, sv_ref) = refs
    x = x_ref[0]
    if has_moe:
        step = pl.program_id(0) * pl.num_programs(1) + pl.program_id(1)
        n_steps = pl.num_programs(0) * pl.num_programs(1)
        x = x + pmod_ref[0, 0][5:6] * _moe_combine(step, n_steps, slot_ref, next_slot_ref, y_ref, rw_ref, ybuf, sem)
        x2_ref[0] = x
    mod = mod_ref[0, 0]
    h = _modulate(x, g_ref[0], mod[0:1], mod[1:2])
    p = jnp.dot(h.astype(BF16), win_ref[0], preferred_element_type=F32)

    def grp(g, n=1):
        return p[:, g * LANES:(g + n) * LANES]

    mla_scale = LOG2E * (MLA_NOPE + MLA_ROPE) ** -0.5
    diff_scale = LOG2E * DIFF_QK ** -0.5
    swa_scale = LOG2E * SWA_DIM ** -0.5
    mla_tab = (mc_ref[...], ml_ref[...], mh_ref[...])
    diff_tab = (dc_ref[...], dl_ref[...], dh_ref[...])
    swa_tab = (wc_ref[...], wl_ref[...], wh_ref[...])

    cq = (_rms(grp(G_CQ, 2)) * qn_ref[0]).astype(BF16)
    q = jnp.dot(cq, wuq_ref[0], preferred_element_type=F32)
    ckv = (_rms(grp(G_CKV)) * kvn_ref[0]).astype(BF16)
    kv = jnp.dot(ckv, wukv_ref[0], preferred_element_type=F32)
    k_rope = _rope(grp(G_KROPE), *mla_tab, MLA_ROPE // 2)
    for hd in range(MLA_HEADS):
        sl = slice(hd * LANES, (hd + 1) * LANES)
        mq_ref[0, :, sl] = (_rope(q[:, sl], *mla_tab, MLA_ROPE // 2) * mla_scale).astype(BF16)
        mk_ref[0, :, sl] = (kv[:, sl] + k_rope).astype(BF16)
    def store_values(ref, vals):
        for g in range(2):
            ref[0, :, 2 * g * LANES:(2 * g + 1) * LANES] = vals[:, g * LANES:(g + 1) * LANES].astype(BF16)
            ref[0, :, (2 * g + 1) * LANES:(2 * g + 2) * LANES] = jnp.ones((TM, LANES), BF16)

    store_values(mv_ref, kv[:, MLA_HEADS * LANES:])

    for g in range(2):
        sl = slice(g * LANES, (g + 1) * LANES)
        dq_ref[0, :, sl] = (_rope(grp(G_DQ + g), *diff_tab, DIFF_QK // 2) * diff_scale).astype(BF16)
        dk_ref[0, :, sl] = _rope(grp(G_DK + g), *diff_tab, DIFF_QK // 2).astype(BF16)
    store_values(dv_ref, grp(G_DV, 2))

    for g in range(4):
        sl = slice(g * LANES, (g + 1) * LANES)
        sq_ref[0, :, sl] = (_rope(grp(G_SQ + g), *swa_tab, SWA_DIM // 2) * swa_scale).astype(BF16)
    for g in range(2):
        sl = slice(g * LANES, (g + 1) * LANES)
        sk_ref[0, :, sl] = _rope(grp(G_SK + g), *swa_tab, SWA_DIM // 2).astype(BF16)
    store_values(sv_ref, grp(G_SV, 2))


def _slot_tiles(slots):
    t = slots.shape[0]
    return slots.reshape(t // TM, TM, TOP_K).transpose(0, 2, 1).reshape(t // TM, 1, TOP_K * TM)


def _pre_attn(layer, xs, mods, g_attn, w_in_p, q_norm, w_uq_p, kv_norm, w_ukv_p, tables, n_ctx, moe=None):
    b, nt, d = xs.shape
    tiles = nt // TM
    n_ctx_tiles = n_ctx // TM
    ctx_row = b

    def tok(w):
        return pl.BlockSpec((1, TM, w), lambda j, bi: (bi, j, 0))

    def lay(shape):
        return pl.BlockSpec((1,) + shape, lambda j, bi: (layer,) + (0,) * len(shape))

    def mod_spec(lyr):
        return pl.BlockSpec((1, 1, N_MOD, d), lambda j, bi: (lyr, jnp.where(j < n_ctx_tiles, ctx_row, bi), 0, 0))

    tab = pl.BlockSpec((TM, LANES), lambda j, bi: (j, 0))
    widths = (512, 512, 512, 256, 256, 512, 512, 256, 512)
    in_specs = [tok(d), mod_spec(layer), lay((1, d)), lay(w_in_p.shape[1:]), lay((1, MLA_Q_RANK)),
                lay(w_uq_p.shape[1:]), lay((1, MLA_KV_RANK)), lay(w_ukv_p.shape[1:])] + [tab] * 9
    out_specs = [tok(w) for w in widths]
    out_shape = [jax.ShapeDtypeStruct((b, nt, w), BF16) for w in widths]
    args = [xs, mods, g_attn, w_in_p, q_norm, w_uq_p, kv_norm, w_ukv_p, *tables]
    scratch = []
    if moe is not None:
        y_rows, slots, rw = moe
        nch = d // LANES

        def next_tile(j, bi):
            wrap = bi + 1 == b
            nj = jnp.minimum(jnp.where(wrap, j + 1, j), tiles - 1)
            return (jnp.where(wrap, 0, bi + 1) * tiles + nj, 0, 0)

        slot_block = (1, 1, TOP_K * TM)
        in_specs = [pl.BlockSpec(slot_block, lambda j, bi: (bi * tiles + j, 0, 0), memory_space=pltpu.SMEM),
                    pl.BlockSpec(slot_block, next_tile, memory_space=pltpu.SMEM),
                    pl.BlockSpec(memory_space=pl.ANY),
                    pl.BlockSpec((TM, LANES), lambda j, bi: (bi * tiles + j, 0)),
                    mod_spec(layer - 1)] + in_specs
        slot_tiles = _slot_tiles(slots)
        args = [slot_tiles, slot_tiles, y_rows, rw, mods] + args
        out_specs = out_specs + [tok(d)]
        out_shape = out_shape + [jax.ShapeDtypeStruct((b, nt, d), F32)]
        scratch = [pltpu.VMEM((2, TOP_K * TM * nch, LANES), F32), pltpu.SemaphoreType.DMA((2,))]
    outs = pl.pallas_call(
        functools.partial(_pre_attn_kernel, moe is not None),
        grid=(tiles, b),
        in_specs=in_specs,
        out_specs=out_specs,
        out_shape=out_shape,
        scratch_shapes=scratch,
        compiler_params=_params("arbitrary", "arbitrary"),
        name="pre_attn",
    )(*args)
    return (outs[-1], outs[:-1]) if moe is not None else (xs, outs)


def _softmax_pv(s, v):
    p = jnp.exp2(s - jnp.max(s, axis=-1, keepdims=True))
    ov = jnp.dot(p.astype(BF16), v, preferred_element_type=F32)
    return ov[:, :LANES] / ov[:, LANES:]


def _dense_steps(n_ctx, nt, run):
    j = pl.program_id(1)

    @pl.when(j == 0)
    def _():
        run(0, n_ctx, n_ctx)

    @pl.when(j > 0)
    def _():
        run(pl.multiple_of(n_ctx + (j - 1) * TQ, TM), TQ, nt)


def _mla_attn_kernel(n_ctx, q_ref, k_ref, v_ref, o_ref):
    def run(row0, rows, nk):
        lane = _lane_iota((rows, LANES))
        outs = []
        for hd in range(MLA_HEADS):
            sl = slice(hd * LANES, (hd + 1) * LANES)
            vs = slice((hd // 2) * 2 * LANES, (hd // 2 + 1) * 2 * LANES)
            s = _qk(q_ref[0, pl.ds(row0, rows), sl], k_ref[0, :nk, sl])
            outs.append(_softmax_pv(s, v_ref[0, :nk, vs]))
        for g in range(2):
            o_ref[0, pl.ds(row0, rows), g * LANES:(g + 1) * LANES] = jnp.where(
                lane < MLA_V, outs[2 * g], outs[2 * g + 1]).astype(BF16)

    _dense_steps(n_ctx, k_ref.shape[1], run)


def _diff_attn_kernel(n_ctx, lambda_init, q_ref, k_ref, v_ref, lam_ref, g_ref, seg_ref, o_ref):
    lam = lam_ref[0]
    lam_full = (jnp.exp(jnp.sum(lam[0:1] * lam[1:2], axis=-1, keepdims=True))
                - jnp.exp(jnp.sum(lam[2:3] * lam[3:4], axis=-1, keepdims=True)) + lambda_init)

    def run(row0, rows, nk):
        lane = _lane_iota((rows, LANES))
        heads = []
        for hd in range(DIFF_HEADS):
            sl = slice((hd // 2) * LANES, (hd // 2 + 1) * LANES)
            q = q_ref[0, pl.ds(row0, rows), sl]
            k = k_ref[0, :nk, sl]
            v = v_ref[0, :nk, (hd // 2) * 2 * LANES:(hd // 2 + 1) * 2 * LANES]
            a = []
            for comp in range(2):
                seg = (hd % 2) * 2 + comp
                qm = jnp.where((lane >= seg * DIFF_QK) & (lane < (seg + 1) * DIFF_QK), q, jnp.zeros_like(q))
                a.append(_softmax_pv(_qk(qm, k), v))
            heads.append(a[0] - lam_full * a[1])
        for g in range(2):
            o = jnp.where(lane < DIFF_V, heads[2 * g], heads[2 * g + 1])
            ms = jnp.dot(o * o, seg_ref[...], precision=HIGHEST, preferred_element_type=F32) * (1.0 / DIFF_V)
            o = o * lax.rsqrt(ms + EPS) * g_ref[...] * (1.0 - lambda_init)
            o_ref[0, pl.ds(row0, rows), g * LANES:(g + 1) * LANES] = o.astype(BF16)

    _dense_steps(n_ctx, k_ref.shape[1], run)


def _swa_attn_kernel(n_ctx, q_ref, k_ref, v_ref, sink_ref, o_ref):
    j = pl.program_id(1)
    nt = k_ref.shape[1]
    band = TM + 2 * WINDOW
    lane = _lane_iota((TM, LANES))

    def run(kv, k, v, allowed):
        for g in (2 * kv, 2 * kv + 1):
            q = q_ref[0, :, g * LANES:(g + 1) * LANES]
            outs = []
            for half in range(2):
                qm = jnp.where((lane >= half * SWA_DIM) & (lane < (half + 1) * SWA_DIM), q, jnp.zeros_like(q))
                s = _qk(qm, k)
                if allowed is not None:
                    s = jnp.where(allowed, s, NEG_INF)
                sink = sink_ref[0, 2 * g + half:2 * g + half + 1, 0:1] * LOG2E
                m = jnp.maximum(jnp.max(s, axis=-1, keepdims=True), sink)
                p = jnp.exp2(s - m)
                ov = jnp.dot(p.astype(BF16), v, preferred_element_type=F32)
                outs.append(ov[:, :LANES] / (ov[:, LANES:] + jnp.exp2(sink - m)))
            o_ref[0, :, g * LANES:(g + 1) * LANES] = jnp.where(lane < SWA_DIM, outs[0], outs[1]).astype(BF16)

    @pl.when(j < n_ctx // TM)
    def _():
        for kv in range(SWA_KV_HEADS):
            ks = slice(kv * LANES, (kv + 1) * LANES)
            vs = slice(2 * kv * LANES, (2 * kv + 2) * LANES)
            run(kv, k_ref[0, :n_ctx, ks], v_ref[0, :n_ctx, vs], None)

    @pl.when(j >= n_ctx // TM)
    def _():
        q0 = j * TM
        w0 = pl.multiple_of(jnp.clip(q0 - WINDOW, n_ctx, nt - band), WINDOW)
        q_pos = q0 + lax.broadcasted_iota(I32, (TM, n_ctx + band), 0)
        col = lax.broadcasted_iota(I32, (TM, n_ctx + band), 1)
        k_pos = w0 + col - n_ctx
        allowed = (col < n_ctx) | (jnp.abs(k_pos - q_pos) <= WINDOW)
        for kv in range(SWA_KV_HEADS):
            ks = slice(kv * LANES, (kv + 1) * LANES)
            vs = slice(2 * kv * LANES, (2 * kv + 2) * LANES)
            k = jnp.concatenate([k_ref[0, :n_ctx, ks], k_ref[0, pl.ds(w0, band), ks]], axis=0)
            v = jnp.concatenate([v_ref[0, :n_ctx, vs], v_ref[0, pl.ds(w0, band), vs]], axis=0)
            run(kv, k, v, allowed)


def _attention(kernel, q, k, v, extra, out_width, name, dense_steps=None):
    b, nt, _ = q.shape

    def whole(width):
        return pl.BlockSpec((1, nt, width), lambda bi, j: (bi, 0, 0))

    def tile(width):
        return pl.BlockSpec((1, TM, width), lambda bi, j: (bi, j, 0))

    q_spec, o_spec, steps = (tile, tile, nt // TM) if dense_steps is None else (whole, whole, dense_steps)
    extra_specs = [pl.BlockSpec(e.shape, lambda bi, j, nd=e.ndim: (0,) * nd) for e in extra]
    return pl.pallas_call(
        kernel,
        grid=(b, steps),
        in_specs=[q_spec(q.shape[2]), whole(k.shape[2]), whole(v.shape[2])] + extra_specs,
        out_specs=o_spec(out_width),
        out_shape=jax.ShapeDtypeStruct((b, nt, out_width), BF16),
        compiler_params=_params("arbitrary", "arbitrary"),
        name=name,
    )(q, k, v, *extra)


def _post_attn_kernel(cap, x_ref, mod_ref, mla_ref, diff_ref, swa_ref, wout_ref, g_ref, wr_ref, br_ref,
                      x1_ref, route_ref, rw_ref, cnt_ref, xs_ref, carry_ref, fbuf, slot_v, slot_s, dsem, ssem):
    step = pl.program_id(0) * pl.num_programs(1) + pl.program_id(1)
    n_steps = pl.num_programs(0) * pl.num_programs(1)
    cur, prev, prev2 = step % 3, (step + 2) % 3, (step + 1) % 3
    par = step % 2
    nch = fbuf.shape[1] // TM

    def slots_landed(sbuf):
        pltpu.make_async_copy(slot_v, slot_s.at[sbuf], ssem.at[sbuf]).wait()

    def scatter(buf, sbuf):
        def body(i, carry):
            for u in range(GATHER_UNROLL):
                r = i * GATHER_UNROLL + u
                src = fbuf.at[buf, pl.ds(pl.multiple_of(r * nch, nch), nch)]
                for k in range(TOP_K):
                    dst = pl.multiple_of(slot_s[sbuf, k, r] * nch, nch)
                    pltpu.make_async_copy(src, xs_ref.at[pl.ds(dst, nch)], dsem.at[buf]).start(priority=k % 2)
            return carry

        lax.fori_loop(0, TM // GATHER_UNROLL, body, 0)

    def drain(buf):
        for _ in range(TOP_K):
            pltpu.make_async_copy(fbuf.at[buf], xs_ref.at[pl.ds(0, TM * nch)], dsem.at[buf]).wait()

    @pl.when(step == 0)
    def _():
        carry_ref[...] = jnp.zeros_like(carry_ref)

    @pl.when(step >= 3)
    def _():
        drain(cur)

    x = x_ref[0]
    mod = mod_ref[0, 0]
    a = jnp.concatenate([mla_ref[0], diff_ref[0], swa_ref[0]], axis=-1)
    x1 = x + mod[2:3] * jnp.dot(a, wout_ref[0], preferred_element_type=F32)
    x1_ref[0] = x1
    f = _modulate(x1, g_ref[0], mod[3:4], mod[4:5])
    for c in range(nch):
        fbuf[cur, pl.ds(c, TM, stride=nch), :] = f[:, c * LANES:(c + 1) * LANES]

    logits = jnp.dot(f.astype(BF16), wr_ref[0], preferred_element_type=F32) + br_ref[0]
    lane = _lane_iota((TM, LANES))
    vals, hots = [], []
    for _ in range(TOP_K):
        m = jnp.max(logits, axis=-1, keepdims=True)
        idx = jnp.min(jnp.where(logits == m, lane, LANES), axis=-1, keepdims=True)
        hot = lane == idx
        logits = jnp.where(hot, -3e38, logits)
        vals.append(m)
        hots.append((idx, hot))
    es = [jnp.exp(v - vals[0]) for v in vals]
    denom = functools.reduce(jnp.add, es)
    sel = functools.reduce(jnp.add, [jnp.where(hot, 1.0, 0.0) for _, hot in hots])

    r_io = lax.broadcasted_iota(I32, (TM, TM), 0)
    c_io = lax.broadcasted_iota(I32, (TM, TM), 1)
    tril = jnp.where(c_io < r_io, 1.0, 0.0).astype(BF16)
    rank = carry_ref[0:1, :] + jnp.dot(tril, sel.astype(BF16), preferred_element_type=F32)
    carry_ref[0:1, :] = carry_ref[0:1, :] + jnp.sum(sel, axis=0, keepdims=True)
    cnt_ref[...] = jnp.broadcast_to(carry_ref[0:1, :], cnt_ref.shape)

    route = jnp.zeros((TM, LANES), I32)
    slots = jnp.zeros((TM, LANES), I32)
    rw = jnp.zeros((TM, LANES), F32)
    for k, (idx, hot) in enumerate(hots):
        rk = jnp.sum(jnp.where(hot, rank, 0.0), axis=-1, keepdims=True).astype(I32)
        route = jnp.where(lane == k, idx, route)
        route = jnp.where(lane == TOP_K + k, rk, route)
        slots = jnp.where(lane == k, idx * cap + rk, slots)
        rw = jnp.where(lane == k, es[k] / denom, rw)
    route_ref[...] = route
    rw_ref[...] = rw
    @pl.when(step >= 1)
    def _():
        slots_landed(1 - par)

    slot_v[...] = slots.T[0:SUBLANES, :]
    pltpu.make_async_copy(slot_v, slot_s.at[par], ssem.at[par]).start()

    @pl.when(step >= 1)
    def _():
        scatter(prev, 1 - par)

    @pl.when(step == n_steps - 1)
    def _():
        slots_landed(par)
        scatter(cur, par)
        drain(prev2)
        drain(prev)
        drain(cur)


def _post_attn(layer, xs, mods, mla_o, diff_o, swa_o, w_out_b, g_ffn, w_router_p, b_router_p, n_ctx, n_exp, cap):
    b, nt, d = xs.shape
    t = b * nt
    nch = d // LANES
    n_ctx_tiles = n_ctx // TM
    tiles = nt // TM

    def tok(w):
        return pl.BlockSpec((1, TM, w), lambda bi, j: (bi, j, 0))

    def lay(shape):
        return pl.BlockSpec((1,) + shape, lambda bi, j: (layer,) + (0,) * len(shape))

    def flat(rows, w):
        return pl.BlockSpec((rows, w), lambda bi, j: (bi * tiles + j, 0))

    mod_spec = pl.BlockSpec((1, 1, N_MOD, d), lambda bi, j: (layer, jnp.where(j < n_ctx_tiles, b, bi), 0, 0))
    return pl.pallas_call(
        functools.partial(_post_attn_kernel, cap),
        grid=(b, tiles),
        in_specs=[tok(d), mod_spec, tok(256), tok(256), tok(512), lay(w_out_b.shape[1:]), lay((1, d)),
                  lay((d, LANES)), lay((1, LANES))],
        out_specs=[tok(d), flat(TM, LANES), flat(TM, LANES),
                   pl.BlockSpec((SUBLANES, LANES), lambda bi, j: (0, 0)), pl.BlockSpec(memory_space=pl.ANY)],
        out_shape=[jax.ShapeDtypeStruct((b, nt, d), F32),
                   jax.ShapeDtypeStruct((t, LANES), I32), jax.ShapeDtypeStruct((t, LANES), F32),
                   jax.ShapeDtypeStruct((SUBLANES, LANES), F32),
                   jax.ShapeDtypeStruct((n_exp * cap * nch, LANES), F32)],
        scratch_shapes=[pltpu.VMEM((SUBLANES, LANES), F32), pltpu.VMEM((3, TM * nch, LANES), F32),
                        pltpu.VMEM((SUBLANES, TM), I32), pltpu.SMEM((2, SUBLANES, TM), I32),
                        pltpu.SemaphoreType.DMA((3,)), pltpu.SemaphoreType.DMA((2,))],
        compiler_params=_params("arbitrary", "arbitrary"),
        name="post_attn",
    )(xs, mods, mla_o, diff_o, swa_o, w_out_b, g_ffn, w_router_p, b_router_p)


def _pad_zero_kernel(n_exp, nch, start_ref, len_ref, xs_in_ref, xs_ref, zbuf, sem):
    del xs_in_ref
    zbuf[...] = jnp.zeros_like(zbuf)
    bits = (TMOE - 1).bit_length()
    for wait in (False, True):
        for e in range(n_exp):
            for bit in range(bits):
                size = 1 << bit

                @pl.when(((len_ref[e] >> bit) & 1) == 1)
                def _():
                    below = len_ref[e] & (size - 1)
                    dst = pl.multiple_of((start_ref[e] + below) * nch, nch)
                    cp = pltpu.make_async_copy(zbuf.at[pl.ds(0, size * nch)], xs_ref.at[pl.ds(dst, size * nch)], sem)
                    cp.wait() if wait else cp.start()


def _pad_zero(xs_rows, pad_start, pad_len, n_exp, nch):
    return pl.pallas_call(
        functools.partial(_pad_zero_kernel, n_exp, nch),
        grid_spec=pltpu.PrefetchScalarGridSpec(
            num_scalar_prefetch=2,
            grid=(1,),
            in_specs=[pl.BlockSpec(memory_space=pl.ANY)],
            out_specs=pl.BlockSpec(memory_space=pl.ANY),
            scratch_shapes=[pltpu.VMEM((TMOE // 2 * nch, LANES), F32), pltpu.SemaphoreType.DMA]),
        out_shape=jax.ShapeDtypeStruct(xs_rows.shape, F32),
        input_output_aliases={2: 0},
        compiler_params=_params("arbitrary"),
        name="pad_zero",
    )(pad_start, pad_len, xs_rows)


def _moe_kernel(nch, te_ref, tb_ref, nu_ref, x_ref, wg_ref, bg_ref, wu_ref, bu_ref, wd_ref, bd_ref, y_ref,
                wg_b, wu_b, wd_b):
    del tb_ref
    i = pl.program_id(0)

    @pl.when((i == 0) | (te_ref[i] != te_ref[jnp.maximum(i - 1, 0)]))
    def _():
        wg_b[...] = wg_ref[0].astype(BF16)
        wu_b[...] = wu_ref[0].astype(BF16)
        wd_b[...] = wd_ref[0].astype(BF16)

    @pl.when(i < nu_ref[0])
    def _():
        x = jnp.concatenate([x_ref[pl.ds(c, TMOE, stride=nch), :] for c in range(nch)], axis=-1).astype(BF16)
        gate = jnp.dot(x, wg_b[...], preferred_element_type=F32) + bg_ref[0]
        up = jnp.dot(x, wu_b[...], preferred_element_type=F32) + bu_ref[0]
        gate = jnp.minimum(gate, SWIGLU_LIMIT)
        up = jnp.clip(up, -SWIGLU_LIMIT, SWIGLU_LIMIT)
        act = gate * (1.0 / (1.0 + jnp.exp(-SWIGLU_ALPHA * gate))) * (up + 1.0)
        y = jnp.dot(act.astype(BF16), wd_b[...], preferred_element_type=F32) + bd_ref[0]
        for c in range(nch):
            y_ref[pl.ds(c, TMOE, stride=nch), :] = y[:, c * LANES:(c + 1) * LANES]


def _moe(layer, xs_rows, tile_e, tile_blk, n_used, n_tiles, wg, bg, wu, bu, wd, bd, n_exp):
    d, f = wg.shape[1], wg.shape[2]
    nch = d // LANES

    def rows(i, te, tb, nu):
        return (tb[i], 0)

    def exp(i, te, tb, nu):
        return (layer * n_exp + te[i], 0, 0)

    return pl.pallas_call(
        functools.partial(_moe_kernel, nch),
        grid_spec=pltpu.PrefetchScalarGridSpec(
            num_scalar_prefetch=3,
            grid=(n_tiles,),
            in_specs=[pl.BlockSpec((TMOE * nch, LANES), rows),
                      pl.BlockSpec((1, d, f), exp), pl.BlockSpec((1, 1, f), exp),
                      pl.BlockSpec((1, d, f), exp), pl.BlockSpec((1, 1, f), exp),
                      pl.BlockSpec((1, f, d), exp), pl.BlockSpec((1, 1, d), exp)],
            out_specs=pl.BlockSpec((TMOE * nch, LANES), rows),
            scratch_shapes=[pltpu.VMEM((d, f), BF16), pltpu.VMEM((d, f), BF16), pltpu.VMEM((f, d), BF16)]),
        out_shape=jax.ShapeDtypeStruct(xs_rows.shape, F32),
        compiler_params=_params("arbitrary"),
        name="moe",
    )(tile_e, tile_blk, n_used, xs_rows, wg, bg, wu, bu, wd, bd)


def _final_kernel(slot_ref, next_slot_ref, y_ref, rw_ref, mod_ref, x1_ref, g_ref, o_ref, ybuf, sem):
    step = pl.program_id(0) * pl.num_programs(1) + pl.program_id(1)
    n_steps = pl.num_programs(0) * pl.num_programs(1)
    x = x1_ref[0] + mod_ref[0, 0][5:6] * _moe_combine(step, n_steps, slot_ref, next_slot_ref, y_ref, rw_ref, ybuf, sem)
    o_ref[0] = _rms(x) * g_ref[...]


def _final_norm(layer, x1, y_rows, slots, rw, mods, g_final, n_ctx, s_len):
    b, nt, d = x1.shape
    nch = d // LANES
    tiles = nt // TM
    lat_tiles = s_len // TM
    off = n_ctx // TM

    def next_tile(bi, j):
        wrap = j + 1 == lat_tiles
        nb = jnp.minimum(jnp.where(wrap, bi + 1, bi), b - 1)
        return (nb * tiles + off + jnp.where(wrap, 0, j + 1), 0, 0)

    slot_block = (1, 1, TOP_K * TM)
    slot_tiles = _slot_tiles(slots)
    return pl.pallas_call(
        _final_kernel,
        grid=(b, lat_tiles),
        in_specs=[pl.BlockSpec(slot_block, lambda bi, j: (bi * tiles + off + j, 0, 0), memory_space=pltpu.SMEM),
                  pl.BlockSpec(slot_block, next_tile, memory_space=pltpu.SMEM),
                  pl.BlockSpec(memory_space=pl.ANY),
                  pl.BlockSpec((TM, LANES), lambda bi, j: (bi * tiles + off + j, 0)),
                  pl.BlockSpec((1, 1, N_MOD, d), lambda bi, j: (layer, bi, 0, 0)),
                  pl.BlockSpec((1, TM, d), lambda bi, j: (bi, j + off, 0)),
                  pl.BlockSpec((1, d), lambda bi, j: (0, 0))],
        out_specs=pl.BlockSpec((1, TM, d), lambda bi, j: (bi, j, 0)),
        out_shape=jax.ShapeDtypeStruct((b, s_len, d), F32),
        scratch_shapes=[pltpu.VMEM((2, TOP_K * TM * nch, LANES), F32), pltpu.SemaphoreType.DMA((2,))],
        compiler_params=_params("arbitrary", "arbitrary"),
        name="final_norm",
    )(slot_tiles, slot_tiles, y_rows, rw, mods, x1, g_final.reshape(1, d))


def _routing_plan(route, counts, n_exp, n_tiles, cap):
    slots = route[:, 0:TOP_K] * cap + route[:, TOP_K:2 * TOP_K]
    counts = counts[0, :n_exp].astype(I32)
    e_tiles = (counts + TMOE - 1) // TMOE
    ends = jnp.cumsum(e_tiles)
    n_used = ends[-1].reshape(1)
    tile_ids = jnp.minimum(jnp.arange(n_tiles, dtype=I32), n_used[0] - 1)
    tile_e = jnp.minimum(jnp.sum((tile_ids[:, None] >= ends[None, :]).astype(I32), axis=-1), n_exp - 1)
    first = jnp.sum(jnp.where(tile_e[:, None] == jnp.arange(n_exp, dtype=I32), (ends - e_tiles)[None, :], 0), axis=-1)
    tile_blk = tile_e * (cap // TMOE) + tile_ids - first
    pad_start = jnp.arange(n_exp, dtype=I32) * cap + counts
    pad_len = e_tiles * TMOE - counts
    return slots, tile_e, tile_blk, n_used, pad_start, pad_len


def kernel(x, c, ctx, c_ctx, w_ada, b_ada, g_attn, w_in, mla_q_norm, mla_w_uq, mla_kv_norm, mla_w_ukv,
           diff_lambda, diff_subln, swa_sink, w_out, g_ffn, w_router, b_router, w_gate, b_gate, w_up, b_up,
           w_down, b_down, g_final):
    b, s_len, d = x.shape
    n_ctx = ctx.shape[1]
    n_layers = w_ada.shape[0]
    n_exp = w_router.shape[2]
    nt = n_ctx + s_len
    t = b * nt
    assert d % LANES == 0 and n_ctx % TM == 0 and s_len % TQ == 0 and t % TMOE == 0 and s_len % GRID_W == 0
    assert n_exp <= LANES and s_len >= TM + 2 * WINDOW and t // TM >= 3

    w_in_p = _gather_columns(w_in, _in_proj_columns()).astype(BF16)
    w_uq_p = _gather_columns(mla_w_uq, _uq_columns()).astype(BF16)
    w_ukv_p = _gather_columns(mla_w_ukv, _ukv_columns()).astype(BF16)
    w_out_b = w_out.astype(BF16)
    w_router_p = jnp.pad(w_router, ((0, 0), (0, 0), (0, LANES - n_exp))).astype(BF16)
    b_router_p = jnp.pad(b_router, ((0, 0), (0, LANES - n_exp)), constant_values=NEG_INF).reshape(n_layers, 1, LANES)
    wg = w_gate.reshape(n_layers * n_exp, d, -1)
    wu = w_up.reshape(n_layers * n_exp, d, -1)
    wd = w_down.reshape(n_layers * n_exp, -1, d)
    bg = b_gate.reshape(n_layers * n_exp, 1, -1)
    bu = b_up.reshape(n_layers * n_exp, 1, -1)
    bd = b_down.reshape(n_layers * n_exp, 1, d)
    lam_p = jnp.pad(diff_lambda, ((0, 0), (0, 0), (0, LANES - DIFF_QK)))
    subln_p = jnp.tile(diff_subln, (1, LANES // DIFF_V)).reshape(n_layers, 1, LANES)
    sink_p = jnp.broadcast_to(swa_sink[:, :, None], (n_layers, SWA_HEADS, LANES))
    seg = jnp.asarray((np.arange(LANES)[:, None] // DIFF_V == np.arange(LANES)[None, :] // DIFF_V), F32)
    tables = _rope_tables(n_ctx, s_len)

    ada_rows = -(-(b + 1) // SUBLANES) * SUBLANES
    cc = jnp.concatenate([c, c_ctx[None, :], jnp.zeros((ada_rows - b - 1, d), F32)], axis=0)
    mods = _ada(cc, w_ada, b_ada).reshape(n_layers, ada_rows, N_MOD, d)

    cap = t
    n_tiles = (t * TOP_K) // TMOE + n_exp
    xs = jnp.concatenate([ctx, x], axis=1)
    moe = None
    for layer in range(n_layers):
        lambda_init = 0.8 - 0.6 * math.exp(-0.3 * layer)
        xs, (mq, mk, mv, dq, dk, dv, sq, sk, sv) = _pre_attn(
            layer, xs, mods, g_attn.reshape(n_layers, 1, d), w_in_p, mla_q_norm.reshape(n_layers, 1, -1), w_uq_p,
            mla_kv_norm.reshape(n_layers, 1, -1), w_ukv_p, tables, n_ctx, moe)
        dense_steps = 1 + s_len // TQ
        mla_o = _attention(functools.partial(_mla_attn_kernel, n_ctx), mq, mk, mv, [], 256, "mla_attn", dense_steps)
        diff_o = _attention(functools.partial(_diff_attn_kernel, n_ctx, lambda_init), dq, dk, dv,
                            [lam_p[layer:layer + 1], subln_p[layer], seg], 256, "diff_attn", dense_steps)
        swa_o = _attention(functools.partial(_swa_attn_kernel, n_ctx), sq, sk, sv, [sink_p[layer:layer + 1]],
                           512, "swa_attn")
        xs, route, rw, counts, xs_rows = _post_attn(
            layer, xs, mods, mla_o, diff_o, swa_o, w_out_b, g_ffn.reshape(n_layers, 1, d), w_router_p, b_router_p,
            n_ctx, n_exp, cap)
        slots, tile_e, tile_blk, n_used, pad_start, pad_len = _routing_plan(route, counts, n_exp, n_tiles, cap)
        xs_rows = _pad_zero(xs_rows, pad_start, pad_len, n_exp, d // LANES)
        moe = (_moe(layer, xs_rows, tile_e, tile_blk, n_used, n_tiles, wg, bg, wu, bu, wd, bd, n_exp), slots, rw)
    return _final_norm(n_layers - 1, xs, *moe, mods, g_final, n_ctx, s_len)
```
